```python
import math
import jax, jax.numpy as jnp
from jax import lax
import numpy as np

D_MODEL = 2048
BATCH = 4
SEQ = 2048
DEPTH = 4
DEC_BATCH = 8
DEC_SEQ = 1
PAST_LEN = 16384
PAGE_SIZE = 128

HEAD_DIM = 128
ROPE_DIM = HEAD_DIM // 4
ROPE_THETA = 500000.0
NORM_EPS = 1e-6
Q_BLOCK = 128
PLE_DIM = 256
ATTN_SCALE = HEAD_DIM ** -0.5
TINY = 1e-30
A_HEADS = D_MODEL // (2 * HEAD_DIM)
A_PATTERNS = ((128, 1), (512, 4), (2048, 16))
A_GROUPS = len(A_PATTERNS)
A_WIN_MAX = max(w for w, _ in A_PATTERNS)
A_WIDTH = A_HEADS * HEAD_DIM
B_CHUNK = 128
B_WIDTH = D_MODEL - A_WIDTH
B_GROUPS = 8
B_GROUP_DIM = B_WIDTH // B_GROUPS
C_HEADS = D_MODEL // HEAD_DIM
C_KV_HEADS = 4
C_GQA = C_HEADS // C_KV_HEADS
C_WIDTH = C_HEADS * HEAD_DIM
C_KV_WIDTH = C_KV_HEADS * HEAD_DIM
CMP_LEN = 32
CMP_STRIDE = 16
SEL_BLOCK = 64
SEL_TOP = 16
C_WIN = 512
FORCE_BONUS = 1000.0
AB_SIZES = (A_GROUPS * A_WIDTH, A_WIDTH, A_WIDTH, A_WIDTH, B_WIDTH, B_WIDTH, B_WIDTH)
C_SIZES = (C_WIDTH,) + (C_KV_WIDTH,) * 6 + (3 * C_HEADS, C_WIDTH)

kernel_name = 'hybrid_dilated_gmlp_nsa_step'


def rms_norm(x, g):
    xf = x.astype(jnp.float32)
    y = xf * lax.rsqrt(jnp.mean(xf * xf, axis=-1, keepdims=True) + NORM_EPS)
    return (y * g.astype(jnp.float32)).astype(x.dtype)


def layer_norm(x, g):
    xf = x.astype(jnp.float32)
    xc = xf - jnp.mean(xf, axis=-1, keepdims=True)
    y = xc * lax.rsqrt(jnp.mean(xc * xc, axis=-1, keepdims=True) + NORM_EPS)
    return (y * g.astype(jnp.float32)).astype(x.dtype)


def split_points(sizes):
    pts, acc = [], 0
    for s in sizes[:-1]:
        acc += s
        pts.append(acc)
    return pts


def query_block(S):
    return Q_BLOCK if S % Q_BLOCK == 0 else S


def to_blocks(a, qb):
    B, S = a.shape[0], a.shape[1]
    return jnp.moveaxis(a.reshape((B, S // qb, qb) + a.shape[2:]), 1, 0)


def from_blocks(a):
    a = jnp.moveaxis(a, 0, 1)
    return a.reshape((a.shape[0], a.shape[1] * a.shape[2]) + a.shape[3:])


def rope_partial(x, pos):
    half = ROPE_DIM // 2
    inv = jnp.exp(-math.log(ROPE_THETA) * jnp.arange(half, dtype=jnp.float32) / half)
    ang = pos.astype(jnp.float32)[:, None] * inv[None, :]
    bshape = (1, pos.shape[0]) + (1,) * (x.ndim - 3) + (half,)
    cos = jnp.cos(ang).reshape(bshape)
    sin = jnp.sin(ang).reshape(bshape)
    xr = x[..., :ROPE_DIM].astype(jnp.float32)
    x1, x2 = xr[..., :half], xr[..., half:]
    rot = jnp.concatenate([x1 * cos - x2 * sin, x2 * cos + x1 * sin], axis=-1).astype(x.dtype)
    return jnp.concatenate([rot, x[..., ROPE_DIM:]], axis=-1)


def dilated_attention(q, k_all, v_all, q_off):
    S = q.shape[1]
    qb = query_block(S)

    def one_block(args):
        q_blk, blk = args
        rows = q_off + blk * qb + jnp.arange(qb)
        maxes, dens, nums = [], [], []
        for g, (window, dil) in enumerate(A_PATTERNS):
            idx = rows[:, None] - dil * jnp.arange(window // dil + 1)[None, :]
            valid = idx >= 0
            idx = jnp.maximum(idx, 0)
            k_g = k_all[:, idx]
            v_g = v_all[:, idx]
            s = jnp.einsum('bqhd,bqmhd->bqhm', q_blk[:, :, g], k_g).astype(jnp.float32) * ATTN_SCALE
            s = jnp.where(valid[None, :, None, :], s, -jnp.inf)
            m = jnp.max(s, axis=-1)
            e = jnp.exp(s - m[..., None])
            maxes.append(m)
            dens.append(jnp.sum(e, axis=-1))
            nums.append(jnp.einsum('bqhm,bqmhd->bqhd', e, v_g.astype(jnp.float32)))
        m_all = jnp.stack(maxes)
        w = jnp.exp(m_all - jnp.max(m_all, axis=0))
        num = jnp.sum(w[..., None] * jnp.stack(nums), axis=0)
        den = jnp.sum(w * jnp.stack(dens), axis=0)
        return (num / den[..., None]).astype(q.dtype)

    out = lax.map(one_block, (to_blocks(q, qb), jnp.arange(S // qb)))
    return from_blocks(out)


def band_attention(q, k_all, v_all, q_off, window):
    S = q.shape[1]
    qb = query_block(S)
    pad = ((0, 0), (window, 0), (0, 0), (0, 0))
    k_pad = jnp.pad(k_all, pad)
    v_pad = jnp.pad(v_all, pad)
    span = window + qb

    def one_block(args):
        q_blk, blk = args
        q0 = q_off + blk * qb
        k_b = lax.dynamic_slice_in_dim(k_pad, q0, span, axis=1)
        v_b = lax.dynamic_slice_in_dim(v_pad, q0, span, axis=1)
        krow = q0 - window + jnp.arange(span)
        qrow = q0 + jnp.arange(qb)
        valid = (krow[None, :] >= 0) & (krow[None, :] <= qrow[:, None]) & (krow[None, :] >= qrow[:, None] - window)
        s = jnp.einsum('bqhgd,bkhd->bqhgk', q_blk, k_b).astype(jnp.float32) * ATTN_SCALE
        s = jnp.where(valid[None, :, None, None, :], s, -jnp.inf)
        p = jax.nn.softmax(s, axis=-1)
        return jnp.einsum('bqhgk,bkhd->bqhgd', p, v_b.astype(jnp.float32))

    return from_blocks(lax.map(one_block, (to_blocks(q, qb), jnp.arange(S // qb))))


def compress(rows, w):
    B, L = rows.shape[0], rows.shape[1]
    n = L // CMP_STRIDE
    seg = rows[:, :n * CMP_STRIDE].reshape(B, n, CMP_STRIDE, C_KV_HEADS, HEAD_DIM)
    first = jnp.einsum('bnrhd,rh->bnhd', seg, w[:CMP_STRIDE])
    second = jnp.einsum('bnrhd,rh->bnhd', seg, w[CMP_STRIDE:])
    return first[:, :-1] + second[:, 1:]


def block_view(rows):
    B, L, H, D = rows.shape
    Lp = -(-L // SEL_BLOCK) * SEL_BLOCK
    rows = jnp.pad(rows, ((0, 0), (0, Lp - L), (0, 0), (0, 0)))
    return rows.reshape(B, Lp // SEL_BLOCK, SEL_BLOCK, H, D).transpose(0, 3, 1, 2, 4)


def make_fetch(k_new, v_new, n_past_blocks, pool, page_table, li):
    kb_new, vb_new = block_view(k_new), block_view(v_new)
    nb_new = kb_new.shape[2]
    bi = jnp.arange(k_new.shape[0])[:, None, None, None]
    hi = jnp.arange(C_KV_HEADS)[None, None, :, None]

    def fetch(idx):
        j_new = jnp.clip(idx - n_past_blocks, 0, nb_new - 1)
        k_n, v_n = kb_new[bi, hi, j_new], vb_new[bi, hi, j_new]
        if pool is None:
            return k_n, v_n
        bpp = pool.shape[3] // SEL_BLOCK
        j_past = jnp.clip(idx, 0, n_past_blocks - 1)
        phys = page_table[bi, j_past // bpp][..., None]
        off = (j_past % bpp)[..., None] * SEL_BLOCK + jnp.arange(SEL_BLOCK)
        h5 = hi[..., None]
        k_p = pool[phys, li, 0, off, h5]
        v_p = pool[phys, li, 1, off, h5]
        is_past = (idx < n_past_blocks)[..., None, None]
        return jnp.where(is_past, k_p, k_n), jnp.where(is_past, v_p, v_n)

    return fetch


def compressed_and_selected(q, q_rot, pos, k_cmp, v_cmp, total_len, fetch):
    S = q.shape[1]
    qb = query_block(S)
    n_cmp = k_cmp.shape[1]
    n_sel = -(-total_len // SEL_BLOCK)
    top = min(SEL_TOP, n_sel)
    blk_last = jnp.arange(n_cmp) * CMP_STRIDE + CMP_LEN - 1
    ci = jnp.arange(n_cmp)[:, None]
    sj = jnp.arange(n_sel)[None, :]
    shared = jnp.minimum(ci * CMP_STRIDE + CMP_LEN, (sj + 1) * SEL_BLOCK) - jnp.maximum(ci * CMP_STRIDE, sj * SEL_BLOCK)
    cmp_to_sel = jnp.maximum(shared, 0).astype(jnp.float32) / CMP_LEN
    sel_ids = jnp.arange(n_sel)[None, :]

    def one_block(args):
        q_blk, qr_blk, p_blk = args
        vis = blk_last[None, :] <= p_blk[:, None]
        s = jnp.einsum('bqhgd,bnhd->bqhgn', q_blk, k_cmp).astype(jnp.float32) * ATTN_SCALE
        s = jnp.where(vis[None, :, None, None, :], s, -jnp.inf)
        m = jnp.max(s, axis=-1, keepdims=True)
        m = jnp.where(jnp.isfinite(m), m, 0.0)
        e = jnp.exp(s - m)
        prob = e / jnp.maximum(jnp.sum(e, axis=-1, keepdims=True), TINY)
        o_cmp = jnp.einsum('bqhgn,bnhd->bqhgd', prob, v_cmp.astype(jnp.float32))
        imp = jnp.einsum('bqhgn,nj->bqhj', prob, cmp_to_sel)
        cur = (p_blk // SEL_BLOCK)[:, None]
        forced = (sel_ids == 0) | (sel_ids == cur) | (sel_ids == cur - 1)
        score = jnp.where(forced[:, None, :], imp + FORCE_BONUS, imp)
        score = jnp.where((sel_ids <= cur)[:, None, :], score, -jnp.inf)
        val, idx = lax.top_k(score, top)
        ok = jnp.isfinite(val)
        k_b, v_b = fetch(idx)
        kpos = idx[..., None] * SEL_BLOCK + jnp.arange(SEL_BLOCK)
        valid = ok[..., None] & (kpos <= p_blk[None, :, None, None, None])
        s2 = jnp.einsum('bqhgd,bqhkrd->bqhgkr', qr_blk, k_b).astype(jnp.float32) * ATTN_SCALE
        s2 = jnp.where(valid[:, :, :, None], s2, -jnp.inf)
        sh = s2.shape
        p2 = jax.nn.softmax(s2.reshape(sh[:4] + (-1,)), axis=-1).reshape(sh)
        o_sel = jnp.einsum('bqhgkr,bqhkrd->bqhgd', p2, v_b.astype(jnp.float32))
        return o_cmp, o_sel

    o_cmp, o_sel = lax.map(one_block, (to_blocks(q, qb), to_blocks(q_rot, qb), pos.reshape(S // qb, qb)))
    return from_blocks(o_cmp), from_blocks(o_sel)


def chunk_spatial_mix(vn, ws, bs):
    B, S, _ = vn.shape
    Sp = -(-S // B_CHUNK) * B_CHUNK
    vp = jnp.pad(vn, ((0, 0), (0, Sp - S), (0, 0))).reshape(B, Sp // B_CHUNK, B_CHUNK, B_GROUPS, B_GROUP_DIM)
    w = ws * jnp.tril(jnp.ones((B_CHUNK, B_CHUNK), ws.dtype))
    mixed = jnp.einsum('gts,bnsgc->bntgc', w, vp) + jnp.transpose(bs)[None, None, :, :, None]
    return mixed.reshape(B, Sp, B_WIDTH)[:, :S]


def ab_mixer(h, li, pos, a_past, prm):
    B, S, _ = h.shape
    xn = rms_norm(h, prm['ln_ab'][li])
    q, k, v, z_a, u_b, v_b, z_b = jnp.split(xn @ prm['w_in_ab'][li], split_points(AB_SIZES), axis=-1)
    q = rope_partial(rms_norm(q.reshape(B, S, A_GROUPS, A_HEADS, HEAD_DIM), prm['qn_a'][li]), pos)
    k = rope_partial(rms_norm(k.reshape(B, S, A_HEADS, HEAD_DIM), prm['kn_a'][li]), pos)
    v = v.reshape(B, S, A_HEADS, HEAD_DIM)
    if a_past is None:
        k_all, v_all, q_off = k, v, 0
        keep = min(A_WIN_MAX, S)
        kv_rows = jnp.stack([k[:, S - keep:], v[:, S - keep:]], axis=1)
    else:
        k_all = jnp.concatenate([a_past[:, 0], k], axis=1)
        v_all = jnp.concatenate([a_past[:, 1], v], axis=1)
        q_off = a_past.shape[2]
        kv_rows = jnp.stack([k, v], axis=1)
    o_a = dilated_attention(q, k_all, v_all, q_off).reshape(B, S, A_WIDTH)
    v_n = layer_norm(v_b, prm['ln_sgu'][li])
    o_b = u_b * chunk_spatial_mix(v_n, prm['ws_b'][li], prm['bs_b'][li])
    y = jnp.concatenate([o_a * jax.nn.silu(z_a), o_b * jax.nn.silu(z_b)], axis=-1) @ prm['w_out_ab'][li]
    chunk_start = ((S - 1) // B_CHUNK) * B_CHUNK
    return y, kv_rows, v_n[:, chunk_start:]


def nsa_mixer(h, li, pos, past, prm):
    B, S, _ = h.shape
    xn = rms_norm(h, prm['ln_c'][li])
    q, kc, vc, ks, vs, kw, vw, g, z = jnp.split(xn @ prm['w_in_c'][li], split_points(C_SIZES), axis=-1)
    kv_shape = (B, S, C_KV_HEADS, HEAD_DIM)
    q = rms_norm(q.reshape(B, S, C_KV_HEADS, C_GQA, HEAD_DIM), prm['qn_c'][li])
    q_rot = rope_partial(q, pos)
    kc, vc, vs, vw = kc.reshape(kv_shape), vc.reshape(kv_shape), vs.reshape(kv_shape), vw.reshape(kv_shape)
    ks = rope_partial(rms_norm(ks.reshape(kv_shape), prm['kn_c'][li, 1]), pos)
    kw = rope_partial(rms_norm(kw.reshape(kv_shape), prm['kn_c'][li, 2]), pos)
    if past is None:
        kc_all, vc_all, n_past = kc, vc, 0
        fetch = make_fetch(ks, vs, 0, None, None, li)
        kw_all, vw_all, w_off = kw, vw, 0
        keep = min(C_WIN, S)
        win_rows = jnp.stack([kw[:, S - keep:], vw[:, S - keep:]], axis=1)
    else:
        pages = past['cmp'][past['page_table'], li]
        n_past = pages.shape[1] * pages.shape[3]
        kc_all = jnp.concatenate([pages[:, :, 0].reshape(B, n_past, C_KV_HEADS, HEAD_DIM), kc], axis=1)
        vc_all = jnp.concatenate([pages[:, :, 1].reshape(B, n_past, C_KV_HEADS, HEAD_DIM), vc], axis=1)
        fetch = make_fetch(ks, vs, n_past // SEL_BLOCK, past['sel'], past['page_table'], li)
        wbuf = past['win'][:, li]
        kw_all = jnp.concatenate([wbuf[:, 0], kw], axis=1)
        vw_all = jnp.concatenate([wbuf[:, 1], vw], axis=1)
        w_off = wbuf.shape[2]
        win_rows = jnp.stack([kw, vw], axis=1)
    k_cmp = rms_norm(compress(kc_all, prm['cw_k'][li]), prm['kn_c'][li, 0])
    v_cmp = compress(vc_all, prm['cw_v'][li])
    o_cmp, o_sel = compressed_and_selected(q, q_rot, pos, k_cmp, v_cmp, n_past + S, fetch)
    o_win = band_attention(q_rot, kw_all, vw_all, w_off, C_WIN)
    gate = jax.nn.sigmoid(g.astype(jnp.float32)).reshape(B, S, 3, C_KV_HEADS, C_GQA, 1)
    o = gate[:, :, 0] * o_cmp + gate[:, :, 1] * o_sel + gate[:, :, 2] * o_win
    y = (o.reshape(B, S, C_WIDTH).astype(h.dtype) * jax.nn.silu(z)) @ prm['w_out_c'][li]
    return y, jnp.stack([kc, vc], axis=1), jnp.stack([ks, vs], axis=1), win_rows


def run_trunk(x, p, start, past, prm):
    B, S, _ = x.shape
    pos = start + jnp.arange(S, dtype=jnp.int32)
    h = x
    a_rows, b_rows, cmp_rows, sel_rows, win_rows = [], [], [], [], []
    for i in range(DEPTH):
        li = i // 2
        if i % 2 == 0:
            a_past = None if past is None else past['a'][:, li]
            mix, kv_a, v_b = ab_mixer(h, li, pos, a_past, prm)
            a_rows.append(kv_a)
            b_rows.append(v_b)
        else:
            mix, kv_c, kv_s, kv_w = nsa_mixer(h, li, pos, past, prm)
            cmp_rows.append(kv_c)
            sel_rows.append(kv_s)
            win_rows.append(kv_w)
        h = h + mix
        h = h + jax.nn.sigmoid(h @ prm['w_ple_gate'][i]) * (p[i] @ prm['w_ple'][i])
    return (h, jnp.stack(a_rows, axis=1), jnp.stack(b_rows, axis=1), jnp.stack(cmp_rows, axis=1),
            jnp.stack(sel_rows, axis=1), jnp.stack(win_rows, axis=1))


def setup_inputs(seed: int = 0) -> dict:
    key = jax.random.key(seed)
    keys = iter(jax.random.split(key, 48))

    def normal(shape, scale=1.0):
        return scale * jax.random.normal(next(keys), shape, jnp.float32)

    def gain(shape, noise=0.05):
        return 1.0 + normal(shape, noise)

    n_ab, n_c = (DEPTH + 1) // 2, DEPTH // 2
    n_pages = PAST_LEN // PAGE_SIZE
    n_used = DEC_BATCH * n_pages
    n_phys = n_used + max(1, n_used // 4)
    win_a, win_c = min(A_WIN_MAX, PAST_LEN), min(C_WIN, PAST_LEN)
    page_table = jax.random.permutation(next(keys), n_phys)[:n_used].reshape(DEC_BATCH, n_pages).astype(jnp.int32)
    return {
        'x_prompt': normal((BATCH, SEQ, D_MODEL)),
        'x_sample': normal((DEC_BATCH, DEC_SEQ, D_MODEL)),
        'cache_a_kv': normal((DEC_BATCH, n_ab, 2, win_a, A_HEADS, HEAD_DIM)),
        'cache_c_cmp_kv': normal((n_phys, n_c, 2, PAGE_SIZE, C_KV_HEADS, HEAD_DIM)),
        'cache_c_sel_kv': normal((n_phys, n_c, 2, PAGE_SIZE, C_KV_HEADS, HEAD_DIM)),
        'cache_c_win_kv': normal((DEC_BATCH, n_c, 2, win_c, C_KV_HEADS, HEAD_DIM)),
        'page_table': page_table,
        'p_prompt': normal((DEPTH, BATCH, SEQ, PLE_DIM)),
        'p_sample': normal((DEPTH, DEC_BATCH, DEC_SEQ, PLE_DIM)),
        'ln_ab': gain((n_ab, D_MODEL)),
        'w_in_ab': normal((n_ab, D_MODEL, sum(AB_SIZES)), D_MODEL ** -0.5),
        'qn_a': gain((n_ab, HEAD_DIM)),
        'kn_a': gain((n_ab, HEAD_DIM)),
        'ln_sgu': gain((n_ab, B_WIDTH)),
        'ws_b': normal((n_ab, B_GROUPS, B_CHUNK, B_CHUNK), 0.5 * B_CHUNK ** -0.5),
        'bs_b': gain((n_ab, B_GROUPS, B_CHUNK), 0.1),
        'w_out_ab': normal((n_ab, D_MODEL, D_MODEL), 0.5 * D_MODEL ** -0.5),
        'ln_c': gain((n_c, D_MODEL)),
        'w_in_c': normal((n_c, D_MODEL, sum(C_SIZES)), D_MODEL ** -0.5),
        'qn_c': gain((n_c, HEAD_DIM)),
        'kn_c': gain((n_c, 3, HEAD_DIM)),
        'cw_k': gain((n_c, CMP_LEN, C_KV_HEADS), 0.2) / CMP_LEN,
        'cw_v': gain((n_c, CMP_LEN, C_KV_HEADS), 0.2) / CMP_LEN,
        'w_out_c': normal((n_c, C_WIDTH, D_MODEL), 0.5 * C_WIDTH ** -0.5),
        'w_ple': normal((DEPTH, PLE_DIM, D_MODEL), 0.5 * PLE_DIM ** -0.5),
        'w_ple_gate': normal((DEPTH, D_MODEL, D_MODEL), D_MODEL ** -0.5),
    }


def reference(x_prompt, x_sample, cache_a_kv, cache_c_cmp_kv, cache_c_sel_kv, cache_c_win_kv, page_table,
              p_prompt, p_sample, ln_ab, w_in_ab, qn_a, kn_a, ln_sgu, ws_b, bs_b, w_out_ab,
              ln_c, w_in_c, qn_c, kn_c, cw_k, cw_v, w_out_c, w_ple, w_ple_gate):
    prm = dict(ln_ab=ln_ab, w_in_ab=w_in_ab, qn_a=qn_a, kn_a=kn_a, ln_sgu=ln_sgu, ws_b=ws_b, bs_b=bs_b,
               w_out_ab=w_out_ab, ln_c=ln_c, w_in_c=w_in_c, qn_c=qn_c, kn_c=kn_c, cw_k=cw_k, cw_v=cw_v,
               w_out_c=w_out_c, w_ple=w_ple, w_ple_gate=w_ple_gate)
    y_prompt, a_p, b_p, cmp_p, sel_p, win_p = run_trunk(x_prompt, p_prompt, 0, None, prm)
    past = dict(a=cache_a_kv, cmp=cache_c_cmp_kv, sel=cache_c_sel_kv, win=cache_c_win_kv, page_table=page_table)
    past_len = page_table.shape[1] * cache_c_cmp_kv.shape[3]
    y_sample, a_s, b_s, cmp_s, sel_s, win_s = run_trunk(x_sample, p_sample, past_len, past, prm)
    return (y_prompt, y_sample, a_p, a_s, b_p, b_s, cmp_p, cmp_s, sel_p, sel_s, win_p, win_s)
```

```python
import functools
import math

import jax
import jax.numpy as jnp
from jax import lax
from jax.experimental import pallas as pl
from jax.experimental.pallas import tpu as pltpu

F32 = jnp.float32
BF = jnp.bfloat16

D_MODEL = 2048
HEAD_DIM = 128
ROPE_DIM = HEAD_DIM // 4
ROPE_HALF = ROPE_DIM // 2
ROPE_THETA = 500000.0
NORM_EPS = 1e-6
Q_BLOCK = 128
PLE_DIM = 256
ATTN_SCALE = HEAD_DIM ** -0.5
TINY = 1e-30
A_HEADS = D_MODEL // (2 * HEAD_DIM)
A_PATTERNS = ((128, 1), (512, 4), (2048, 16))
A_GROUPS = len(A_PATTERNS)
A_WIDTH = A_HEADS * HEAD_DIM
B_CHUNK = 128
B_WIDTH = D_MODEL - A_WIDTH
B_GROUPS = 8
C_HEADS = D_MODEL // HEAD_DIM
C_KV_HEADS = 4
C_GQA = C_HEADS // C_KV_HEADS
C_WIDTH = C_HEADS * HEAD_DIM
C_KV_WIDTH = C_KV_HEADS * HEAD_DIM
CMP_LEN = 32
CMP_STRIDE = 16
SEL_BLOCK = 64
SEL_TOP = 16
C_WIN = 512
FORCE_BONUS = 1000.0
NEG_INF = float("-inf")

LANES = 128
SUBLANES = 8
MM_TILE_M = 1024
MM_TILE_N = 512
ROW_TILE = 512
SAMPLE_ROWS = 16
VMEM_LIMIT = 56 * 1024 * 1024

AB_Q, AB_K, AB_V, AB_ZA, AB_UB, AB_VB, AB_ZB = 0, 3072, 4096, 5120, 6144, 7168, 8192
C_Q, C_KC, C_VC, C_KS, C_VS, C_KW, C_VW, C_G, C_Z = 0, 2048, 2560, 3072, 3584, 4096, 4608, 5120, 5168
C_QKV_COLS = 5120


def _params(*sem):
    return pltpu.CompilerParams(dimension_semantics=sem, vmem_limit_bytes=VMEM_LIMIT)


def _dot(a, b):
    return jnp.dot(a, b, preferred_element_type=F32)


def _dot_nt(a, b):
    return lax.dot_general(a, b, (((1,), (1,)), ((), ())), preferred_element_type=F32)


def _sigmoid(x):
    return jax.nn.sigmoid(x)


def _silu(x):
    return x * _sigmoid(x)


def _rope_table_kernel(cos_ref, sa_ref, sb_ref, *, start, step):
    rows = cos_ref.shape[0]
    r = lax.broadcasted_iota(jnp.int32, (rows, HEAD_DIM), 0)
    lane = lax.broadcasted_iota(jnp.int32, (rows, HEAD_DIM), 1)
    pos = (start + r * step).astype(F32)
    j = (lane & (ROPE_HALF - 1)).astype(F32)
    inv = jnp.exp(-math.log(ROPE_THETA) * j / ROPE_HALF)
    ang = pos * inv
    c, s = jnp.cos(ang), jnp.sin(ang)
    in_rope = lane < ROPE_DIM
    cos_ref[...] = jnp.where(in_rope, c, 1.0)
    sa_ref[...] = jnp.where(in_rope & (lane >= ROPE_HALF), s, 0.0)
    sb_ref[...] = jnp.where(lane < ROPE_HALF, -s, 0.0)


def _rope_tables(rows, start, step):
    shp = jax.ShapeDtypeStruct((rows, HEAD_DIM), F32)
    return pl.pallas_call(
        functools.partial(_rope_table_kernel, start=start, step=step),
        out_shape=(shp, shp, shp), name="rope_tables")()


def _rms_kernel(x_ref, g_ref, o_ref):
    x = x_ref[...]
    y = x * lax.rsqrt(jnp.mean(x * x, axis=-1, keepdims=True) + NORM_EPS)
    o_ref[...] = (y * g_ref[...]).astype(o_ref.dtype)


def _rms_rows(x, gains, li, out_dtype):
    M, D = x.shape
    tm = min(M, ROW_TILE)
    return pl.pallas_call(
        _rms_kernel, grid=(M // tm,),
        in_specs=[pl.BlockSpec((tm, D), lambda i: (i, 0)),
                  pl.BlockSpec((None, 1, D), lambda i: (li, 0, 0))],
        out_specs=pl.BlockSpec((tm, D), lambda i: (i, 0)),
        out_shape=jax.ShapeDtypeStruct((M, D), out_dtype),
        compiler_params=_params("arbitrary"), name="rms_rows",
    )(x, gains.reshape(gains.shape[0], 1, D))


def _head_norm_kernel(x_ref, g_ref, cos_ref, sa_ref, sb_ref, *out_refs, heads, rope, plain):
    g = g_ref[...]
    for j in range(heads):
        cols = slice(j * HEAD_DIM, (j + 1) * HEAD_DIM)
        x = x_ref[:, cols]
        y = x * lax.rsqrt(jnp.mean(x * x, axis=-1, keepdims=True) + NORM_EPS) * g
        if plain:
            out_refs[0][:, cols] = y
        if rope:
            up = pltpu.roll(y, ROPE_HALF, 1)
            down = pltpu.roll(y, HEAD_DIM - ROPE_HALF, 1)
            out_refs[-1][:, cols] = y * cos_ref[...] + up * sa_ref[...] + down * sb_ref[...]


def _head_norm(x, col0, ncols, gain, tables, *, rope=True, plain=False):
    M = x.shape[0]
    cos, sa, sb = tables
    trows = cos.shape[0]
    tm = min(M, ROW_TILE, trows)
    tc = 512
    heads = tc // HEAD_DIM
    nt = trows // tm
    n_out = int(rope) + int(plain)
    tspec = pl.BlockSpec((tm, HEAD_DIM), lambda i, j: (i % nt, 0))
    ospec = pl.BlockSpec((tm, tc), lambda i, j: (i, j))
    outs = pl.pallas_call(
        functools.partial(_head_norm_kernel, heads=heads, rope=rope, plain=plain),
        grid=(M // tm, ncols // tc),
        in_specs=[pl.BlockSpec((tm, tc), lambda i, j: (i, j + col0 // tc)),
                  pl.BlockSpec((1, HEAD_DIM), lambda i, j: (0, 0)), tspec, tspec, tspec],
        out_specs=[ospec] * n_out,
        out_shape=[jax.ShapeDtypeStruct((M, ncols), F32)] * n_out,
        compiler_params=_params("arbitrary", "arbitrary"), name="head_norm",
    )(x, gain, cos, sa, sb)
    return outs if n_out > 1 else outs[0]


def _mm_kernel(x_ref, w_ref, o_ref, wb_ref):
    @pl.when(pl.program_id(1) == 0)
    def _():
        wb_ref[...] = w_ref[...].astype(BF)

    o_ref[...] = _dot(x_ref[...].astype(BF), wb_ref[...])


def _matmul(x, w, li, col0, ncols, tn=MM_TILE_N):
    M, K = x.shape
    tm = min(M, MM_TILE_M)
    if li is None:
        wspec = pl.BlockSpec((K, tn), lambda n, m: (0, n + col0 // tn))
    else:
        wspec = pl.BlockSpec((None, K, tn), lambda n, m: (li, 0, n + col0 // tn))
    return pl.pallas_call(
        _mm_kernel, grid=(ncols // tn, M // tm),
        in_specs=[pl.BlockSpec((tm, K), lambda n, m: (m, 0)), wspec],
        out_specs=pl.BlockSpec((tm, tn), lambda n, m: (m, n)),
        out_shape=jax.ShapeDtypeStruct((M, ncols), F32),
        scratch_shapes=[pltpu.VMEM((K, tn), BF)],
        compiler_params=_params("arbitrary", "arbitrary"), name="matmul",
    )(x, w)


def _proj_res_kernel(*refs, ks, with_bf):
    n = len(ks)
    lhs = refs[:n]
    w_ref, res_ref, o_ref = refs[n:n + 3]
    wb_ref = refs[-1]

    @pl.when(pl.program_id(1) == 0)
    def _():
        wb_ref[...] = w_ref[...].astype(BF)

    acc = res_ref[...]
    off = 0
    for r, k in zip(lhs, ks):
        acc = acc + _dot(r[...].astype(BF), wb_ref[off:off + k, :])
        off += k
    o_ref[...] = acc
    if with_bf:
        refs[n + 3][...] = acc.astype(BF)


def _proj_residual(lhs_list, w, li, res, with_bf):
    M, N = res.shape
    ks = tuple(a.shape[1] for a in lhs_list)
    K = sum(ks)
    tm, tn = min(M, MM_TILE_M), MM_TILE_N
    ospec = pl.BlockSpec((tm, tn), lambda n, m: (m, n))
    out_shape = [jax.ShapeDtypeStruct((M, N), F32)]
    if with_bf:
        out_shape.append(jax.ShapeDtypeStruct((M, N), BF))
    outs = pl.pallas_call(
        functools.partial(_proj_res_kernel, ks=ks, with_bf=with_bf),
        grid=(N // tn, M // tm),
        in_specs=[pl.BlockSpec((tm, k), lambda n, m: (m, 0)) for k in ks]
        + [pl.BlockSpec((None, K, tn), lambda n, m: (li, 0, n)), ospec],
        out_specs=[ospec] * len(out_shape), out_shape=out_shape,
        scratch_shapes=[pltpu.VMEM((K, tn), BF)],
        compiler_params=_params("arbitrary", "arbitrary"), name="proj_residual",
    )(*lhs_list, w, res)
    return outs if with_bf else (outs[0], outs[0])


def _ple_kernel(hl_ref, wg_ref, p_ref, wp_ref, h_ref, o_ref, wgb_ref, wpb_ref):
    @pl.when(pl.program_id(1) == 0)
    def _():
        wgb_ref[...] = wg_ref[...].astype(BF)
        wpb_ref[...] = wp_ref[...].astype(BF)

    gate = _sigmoid(_dot(hl_ref[...].astype(BF), wgb_ref[...]))
    pp = _dot(p_ref[...].astype(BF), wpb_ref[...])
    o_ref[...] = h_ref[...] + gate * pp


def _ple(h, h_lhs, p, w_gate, w_ple, layer):
    M, N = h.shape
    K, KP = h_lhs.shape[1], p.shape[1]
    tm, tn = min(M, MM_TILE_M), MM_TILE_N
    ospec = pl.BlockSpec((tm, tn), lambda n, m: (m, n))
    return pl.pallas_call(
        _ple_kernel, grid=(N // tn, M // tm),
        in_specs=[pl.BlockSpec((tm, K), lambda n, m: (m, 0)),
                  pl.BlockSpec((None, K, tn), lambda n, m: (layer, 0, n)),
                  pl.BlockSpec((tm, KP), lambda n, m: (m, 0)),
                  pl.BlockSpec((None, KP, tn), lambda n, m: (layer, 0, n)), ospec],
        out_specs=ospec, out_shape=jax.ShapeDtypeStruct((M, N), F32),
        scratch_shapes=[pltpu.VMEM((K, tn), BF), pltpu.VMEM((KP, tn), BF)],
        compiler_params=_params("arbitrary", "arbitrary"), name="ple",
    )(h_lhs, w_gate, p, w_ple, h)


def _dilated_kernel(q0_ref, q1_ref, q2_ref, k_ref, v_ref, z_ref, o_ref, num_ref, m_ref, l_ref, *, S):
    blk = Q_BLOCK
    row = lax.broadcasted_iota(jnp.int32, (blk, blk), 0)
    col = lax.broadcasted_iota(jnp.int32, (blk, blk), 1)
    q_refs = (q0_ref, q1_ref, q2_ref)
    for g, (window, d) in enumerate(A_PATTERNS):
        assert window // d == blk
        nblk = S // d // blk
        q_ref = q_refs[g]

        def unit(u, carry, g=g, d=d, nblk=nblk, q_ref=q_ref):
            c = u // nblk
            i = u - c * nblk
            start = c + d * blk * i
            rows = pl.ds(start, blk, stride=d) if d > 1 else pl.ds(pl.multiple_of(start, blk), blk)
            q = q_ref[rows, :].astype(BF)
            kc = k_ref[rows, :].astype(BF)
            vc = v_ref[rows, :].astype(BF)
            s_c = jnp.where(col <= row, _dot_nt(q, kc) * ATTN_SCALE, NEG_INF)
            m = jnp.max(s_c, axis=-1, keepdims=True)
            if nblk > 1:
                pstart = jnp.maximum(start - d * blk, c)
                prows = pl.ds(pstart, blk, stride=d) if d > 1 else pl.ds(pl.multiple_of(pstart, blk), blk)
                kp = k_ref[prows, :].astype(BF)
                vp = v_ref[prows, :].astype(BF)
                s_p = jnp.where((col >= row) & (i > 0), _dot_nt(q, kp) * ATTN_SCALE, NEG_INF)
                m = jnp.maximum(m, jnp.max(s_p, axis=-1, keepdims=True))
            e_c = jnp.exp(s_c - m)
            l = jnp.sum(e_c, axis=-1, keepdims=True)
            num = _dot(e_c.astype(BF), vc)
            if nblk > 1:
                e_p = jnp.exp(s_p - m)
                l = l + jnp.sum(e_p, axis=-1, keepdims=True)
                num = num + _dot(e_p.astype(BF), vp)
            num_ref[g, rows, :] = num
            m_ref[g, rows, :] = jnp.broadcast_to(m, (blk, HEAD_DIM))
            l_ref[g, rows, :] = jnp.broadcast_to(l, (blk, HEAD_DIM))
            return carry

        lax.fori_loop(0, d * nblk, unit, 0)

    def merge(i, carry):
        rows = pl.ds(pl.multiple_of(i * blk, blk), blk)
        ms = [m_ref[g, rows, :] for g in range(A_GROUPS)]
        m_all = jnp.maximum(jnp.maximum(ms[0], ms[1]), ms[2])
        ws = [jnp.exp(m - m_all) for m in ms]
        num = ws[0] * num_ref[0, rows, :] + ws[1] * num_ref[1, rows, :] + ws[2] * num_ref[2, rows, :]
        den = ws[0] * l_ref[0, rows, :] + ws[1] * l_ref[1, rows, :] + ws[2] * l_ref[2, rows, :]
        o_ref[rows, :] = ((num / den) * _silu(z_ref[rows, :])).astype(o_ref.dtype)
        return carry

    lax.fori_loop(0, S // blk, merge, 0)


def _dilated_prompt(q_r, k_r, proj, B, S):
    H = A_HEADS
    hb = lambda off: off // HEAD_DIM

    def col_spec(base):
        return pl.BlockSpec((S, HEAD_DIM), lambda b, h: (b, base + h))

    return pl.pallas_call(
        functools.partial(_dilated_kernel, S=S), grid=(B, H),
        in_specs=[col_spec(0), col_spec(H), col_spec(2 * H), col_spec(0),
                  col_spec(hb(AB_V)), col_spec(hb(AB_ZA))],
        out_specs=col_spec(0),
        out_shape=jax.ShapeDtypeStruct((B * S, A_WIDTH), BF),
        scratch_shapes=[pltpu.VMEM((A_GROUPS, S, HEAD_DIM), F32)] * 3,
        compiler_params=_params("arbitrary", "arbitrary"), name="dilated_prompt",
    )(q_r, q_r, q_r, k_r, proj, proj)


def _layer_norm(v, g):
    xc = v - jnp.mean(v, axis=-1, keepdims=True)
    return xc * lax.rsqrt(jnp.mean(xc * xc, axis=-1, keepdims=True) + NORM_EPS) * g


def _gmlp_kernel(u_ref, v_ref, z_ref, g_ref, ws_ref, bst_ref, vn_ref, o_ref):
    vn = _layer_norm(v_ref[...], g_ref[...])
    vn_ref[...] = vn
    row = lax.broadcasted_iota(jnp.int32, (B_CHUNK, B_CHUNK), 0)
    col = lax.broadcasted_iota(jnp.int32, (B_CHUNK, B_CHUNK), 1)
    for g in range(B_GROUPS):
        cols = slice(g * LANES, (g + 1) * LANES)
        w = jnp.where(row >= col, ws_ref[g], 0.0).astype(BF)
        mixed = _dot(w, vn[:, cols].astype(BF)) + bst_ref[:, g:g + 1]
        o_ref[:, cols] = ((u_ref[:, cols] * mixed) * _silu(z_ref[:, cols])).astype(o_ref.dtype)


def _gmlp_prompt(proj, ln_sgu, ws_b, bs_b, li):
    M = proj.shape[0]
    wb = B_WIDTH
    blk = lambda off: pl.BlockSpec((B_CHUNK, wb), lambda i: (i, off // wb))
    bst = jnp.swapaxes(bs_b, 1, 2)
    return pl.pallas_call(
        _gmlp_kernel, grid=(M // B_CHUNK,),
        in_specs=[blk(AB_UB), blk(AB_VB), blk(AB_ZB),
                  pl.BlockSpec((None, 1, wb), lambda i: (li, 0, 0)),
                  pl.BlockSpec((None, B_GROUPS, B_CHUNK, B_CHUNK), lambda i: (li, 0, 0, 0)),
                  pl.BlockSpec((None, B_CHUNK, B_GROUPS), lambda i: (li, 0, 0))],
        out_specs=[pl.BlockSpec((B_CHUNK, wb), lambda i: (i, 0))] * 2,
        out_shape=[jax.ShapeDtypeStruct((M, wb), F32), jax.ShapeDtypeStruct((M, wb), BF)],
        compiler_params=_params("arbitrary"), name="gmlp_prompt",
    )(proj, proj, proj, ln_sgu.reshape(-1, 1, wb), ws_b, bst)


def _compress_rows(x_ref, w_ref, wbase, h, nseg):
    first = jnp.zeros((nseg, HEAD_DIM), F32)
    second = jnp.zeros((nseg, HEAD_DIM), F32)
    for r in range(CMP_STRIDE):
        rows = x_ref[pl.ds(r, nseg, stride=CMP_STRIDE), :]
        first = first + rows * w_ref[wbase + r * C_KV_HEADS + h]
        second = second + rows * w_ref[wbase + (CMP_STRIDE + r) * C_KV_HEADS + h]
    c = first + pltpu.roll(second, nseg - 1, 0)
    seg = lax.broadcasted_iota(jnp.int32, (nseg, HEAD_DIM), 0)
    return jnp.where(seg < nseg - 1, c, 0.0)


def _compress_kernel(cwk_ref, cwv_ref, kc_ref, vc_ref, g_ref, ko_ref, vo_ref, *, li, nseg):
    h = pl.program_id(1)
    wbase = li * CMP_LEN * C_KV_HEADS
    kc = _compress_rows(kc_ref, cwk_ref, wbase, h, nseg)
    ko_ref[...] = kc * lax.rsqrt(jnp.mean(kc * kc, axis=-1, keepdims=True) + NORM_EPS) * g_ref[...]
    vo_ref[...] = _compress_rows(vc_ref, cwv_ref, wbase, h, nseg)


def _compress_prompt(proj, cw_k, cw_v, gain, li, B, S):
    nseg = S // CMP_STRIDE
    hb = lambda off: off // HEAD_DIM
    smem = pl.BlockSpec(memory_space=pltpu.SMEM)
    ospec = pl.BlockSpec((None, None, nseg, HEAD_DIM), lambda b, h: (b, h, 0, 0))
    oshape = jax.ShapeDtypeStruct((B, C_KV_HEADS, nseg, HEAD_DIM), F32)
    return pl.pallas_call(
        functools.partial(_compress_kernel, li=li, nseg=nseg), grid=(B, C_KV_HEADS),
        in_specs=[smem, smem,
                  pl.BlockSpec((S, HEAD_DIM), lambda b, h: (b, hb(C_KC) + h)),
                  pl.BlockSpec((S, HEAD_DIM), lambda b, h: (b, hb(C_VC) + h)),
                  pl.BlockSpec((1, HEAD_DIM), lambda b, h: (0, 0))],
        out_specs=[ospec, ospec], out_shape=[oshape, oshape],
        compiler_params=_params("arbitrary", "arbitrary"), name="compress_prompt",
    )(cw_k.reshape(-1), cw_v.reshape(-1), proj, proj, gain)


def _cmp_to_sel(nrows, ncols, seg_mask):
    n = lax.broadcasted_iota(jnp.int32, (nrows, ncols), 0) & seg_mask
    j = lax.broadcasted_iota(jnp.int32, (nrows, ncols), 1)
    shared = (jnp.minimum(n * CMP_STRIDE + CMP_LEN, (j + 1) * SEL_BLOCK)
              - jnp.maximum(n * CMP_STRIDE, j * SEL_BLOCK))
    return jnp.maximum(shared, 0).astype(F32) / CMP_LEN


def _rank_select(score, n_sel, lane):
    rank = jnp.zeros(score.shape, F32)
    for jp in range(n_sel):
        cv = score[:, jp:jp + 1]
        ahead = (cv > score) | ((cv == score) & (lane > jp))
        rank = rank + jnp.where(ahead, 1.0, 0.0)
    return (rank < SEL_TOP) & (score > NEG_INF)


def _flash_loop(q4, k_ref, v_ref, t_lo, t_hi, tk, bias_fn):
    rows = q4.shape[0]

    def body(t, carry):
        m, l, acc = carry
        k0 = pl.multiple_of(t * tk, tk)
        kt = k_ref[pl.ds(k0, tk), :].astype(BF)
        vt = v_ref[pl.ds(k0, tk), :].astype(BF)
        s = _dot_nt(q4, kt) * ATTN_SCALE + bias_fn(k0)
        m_new = jnp.maximum(m, jnp.max(s, axis=-1, keepdims=True))
        m_safe = jnp.where(m_new == NEG_INF, 0.0, m_new)
        alpha = jnp.exp(m - m_safe)
        e = jnp.exp(s - m_safe)
        l = alpha * l + jnp.sum(e, axis=-1, keepdims=True)
        acc = alpha * acc + _dot(e.astype(BF), vt)
        return m_new, l, acc

    init = (jnp.full((rows, 1), NEG_INF, F32), jnp.zeros((rows, 1), F32), jnp.zeros((rows, HEAD_DIM), F32))
    _, l, acc = lax.fori_loop(t_lo, t_hi, body, init)
    return acc / l


def _stack_heads(ref):
    return jnp.concatenate([ref[:, g * HEAD_DIM:(g + 1) * HEAD_DIM] for g in range(C_GQA)], axis=0)


def _nsa_kernel(qn_ref, qr_ref, kcmp_ref, vcmp_ref, ks_ref, vs_ref, kw_ref, vw_ref, g_ref, z_ref, o_ref,
                *, n_sel, sel_tk, win_tk):
    qb = Q_BLOCK
    q0 = pl.program_id(2) * qb
    nseg = kcmp_ref.shape[0]
    rows4 = C_GQA * qb
    p_q = q0 + lax.broadcasted_iota(jnp.int32, (qb, 1), 0)
    p_4 = q0 + (lax.broadcasted_iota(jnp.int32, (rows4, 1), 0) & (qb - 1))

    qn4 = _stack_heads(qn_ref).astype(BF)
    seg = lax.broadcasted_iota(jnp.int32, (rows4, nseg), 1)
    vis = seg * CMP_STRIDE + (CMP_LEN - 1) <= p_4
    s = jnp.where(vis, _dot_nt(qn4, kcmp_ref[...].astype(BF)) * ATTN_SCALE, NEG_INF)
    m = jnp.max(s, axis=-1, keepdims=True)
    m = jnp.where(m == NEG_INF, 0.0, m)
    e = jnp.exp(s - m)
    prob = e / jnp.maximum(jnp.sum(e, axis=-1, keepdims=True), TINY)
    prob_b = prob.astype(BF)
    o_cmp = _dot(prob_b, vcmp_ref[...].astype(BF))

    prob_cat = jnp.concatenate([prob_b[g * qb:(g + 1) * qb, :] for g in range(C_GQA)], axis=1)
    imp = _dot(prob_cat, _cmp_to_sel(C_GQA * nseg, LANES, nseg - 1).astype(BF))
    lane = lax.broadcasted_iota(jnp.int32, (qb, LANES), 1)
    cur = p_q // SEL_BLOCK
    forced = (lane == 0) | (lane == cur) | (lane == cur - 1)
    score = jnp.where(forced, imp + FORCE_BONUS, imp)
    score = jnp.where(lane <= cur, score, NEG_INF)
    sel = jnp.where(_rank_select(score, n_sel, lane), 1.0, 0.0).astype(BF)

    qr4 = _stack_heads(qr_ref).astype(BF)

    def sel_bias(k0):
        kpos = k0 + lax.broadcasted_iota(jnp.int32, (1, sel_tk), 1)
        blk_of_key = k0 // SEL_BLOCK + lax.broadcasted_iota(jnp.int32, (LANES, sel_tk), 1) // SEL_BLOCK
        expand = jnp.where(lax.broadcasted_iota(jnp.int32, (LANES, sel_tk), 0) == blk_of_key, 1.0, 0.0)
        chosen = _dot(sel, expand.astype(BF))
        ok = (chosen > 0.5) & (kpos <= p_q)
        b = jnp.where(ok, 0.0, NEG_INF)
        return jnp.concatenate([b] * C_GQA, axis=0)

    n_sel_tiles = (q0 + qb + sel_tk - 1) // sel_tk
    o_sel = _flash_loop(qr4, ks_ref, vs_ref, 0, n_sel_tiles, sel_tk, sel_bias)

    def win_bias(k0):
        kpos = k0 + lax.broadcasted_iota(jnp.int32, (1, win_tk), 1)
        ok = (kpos <= p_q) & (kpos >= p_q - C_WIN)
        b = jnp.where(ok, 0.0, NEG_INF)
        return jnp.concatenate([b] * C_GQA, axis=0)

    t_lo = jnp.maximum(q0 - C_WIN, 0) // win_tk
    o_win = _flash_loop(qr4, kw_ref, vw_ref, t_lo, (q0 + qb) // win_tk, win_tk, win_bias)

    for g in range(C_GQA):
        rs = slice(g * qb, (g + 1) * qb)
        cs = slice(g * HEAD_DIM, (g + 1) * HEAD_DIM)
        gate = [_sigmoid(g_ref[:, br * C_GQA + g:br * C_GQA + g + 1]) for br in range(3)]
        o = gate[0] * o_cmp[rs] + gate[1] * o_sel[rs] + gate[2] * o_win[rs]
        o_ref[:, cs] = (o * _silu(z_ref[:, cs])).astype(o_ref.dtype)


def _nsa_prompt(qn, qr, kcmp, vcmp, ks_r, kw_r, proj, gates, z, B, S):
    nq = S // Q_BLOCK
    nseg = kcmp.shape[2]
    gw = C_GQA * HEAD_DIM
    hb = lambda off: off // HEAD_DIM
    qspec = pl.BlockSpec((Q_BLOCK, gw), lambda b, h, i: (b * nq + i, h))
    cspec = pl.BlockSpec((None, None, nseg, HEAD_DIM), lambda b, h, i: (b, h, 0, 0))

    def kv_spec(base):
        return pl.BlockSpec((S, HEAD_DIM), lambda b, h, i: (b, base + h))

    return pl.pallas_call(
        functools.partial(_nsa_kernel, n_sel=-(-S // SEL_BLOCK), sel_tk=256, win_tk=128),
        grid=(B, C_KV_HEADS, nq),
        in_specs=[qspec, qspec, cspec, cspec, kv_spec(0), kv_spec(hb(C_VS)), kv_spec(0), kv_spec(hb(C_VW)),
                  pl.BlockSpec((Q_BLOCK, LANES), lambda b, h, i: (b * nq + i, h)), qspec],
        out_specs=qspec, out_shape=jax.ShapeDtypeStruct((B * S, C_WIDTH), BF),
        compiler_params=_params("arbitrary", "arbitrary", "arbitrary"), name="nsa_prompt",
    )(qn, qr, kcmp, vcmp, ks_r, proj, kw_r, proj, gates, z)


def _gate_weights(w_in_c, li):
    wg = w_in_c[li, :, C_G:C_Z].reshape(D_MODEL, 3, C_KV_HEADS, C_GQA)
    wg = jnp.transpose(wg, (0, 2, 1, 3)).reshape(D_MODEL, C_KV_HEADS, 3 * C_GQA)
    wg = jnp.pad(wg, ((0, 0), (0, 0), (0, LANES - 3 * C_GQA)))
    return wg.reshape(D_MODEL, C_KV_HEADS * LANES)


def _ab_layer_prompt(h, hb, li, layer, p, prm, tables, B, S):
    xn = _rms_rows(h, prm["ln_ab"], li, BF)
    proj = _matmul(xn, prm["w_in_ab"], li, 0, prm["w_in_ab"].shape[2])
    q_r = _head_norm(proj, AB_Q, A_GROUPS * A_WIDTH, prm["qn_a"][li][None], tables)
    k_r = _head_norm(proj, AB_K, A_WIDTH, prm["kn_a"][li][None], tables)
    ga = _dilated_prompt(q_r, k_r, proj, B, S)
    vn, gb = _gmlp_prompt(proj, prm["ln_sgu"], prm["ws_b"], prm["bs_b"], li)
    h1, h1b = _proj_residual([ga, gb], prm["w_out_ab"], li, h, True)
    h2 = _ple(h1, h1b, p, prm["w_ple_gate"], prm["w_ple"], layer)
    kv = jnp.stack([k_r.reshape(B, S, A_HEADS, HEAD_DIM),
                    proj[:, AB_V:AB_V + A_WIDTH].reshape(B, S, A_HEADS, HEAD_DIM)], axis=1)
    chunk_start = ((S - 1) // B_CHUNK) * B_CHUNK
    return h2, kv, vn.reshape(B, S, B_WIDTH)[:, chunk_start:]


def _c_layer_prompt(h, li, layer, p, prm, tables, B, S):
    w = prm["w_in_c"]
    xn = _rms_rows(h, prm["ln_c"], li, BF)
    proj = _matmul(xn, w, li, 0, C_QKV_COLS)
    z = _matmul(xn, w[li, :, C_Z:], None, 0, C_WIDTH)
    gates = _matmul(xn, _gate_weights(w, li), None, 0, C_KV_HEADS * LANES)
    qn, qr = _head_norm(proj, C_Q, C_WIDTH, prm["qn_c"][li][None], tables, plain=True)
    ks_r = _head_norm(proj, C_KS, C_KV_WIDTH, prm["kn_c"][li, 1][None], tables)
    kw_r = _head_norm(proj, C_KW, C_KV_WIDTH, prm["kn_c"][li, 2][None], tables)
    kcmp, vcmp = _compress_prompt(proj, prm["cw_k"], prm["cw_v"], prm["kn_c"][li, 0][None], li, B, S)
    lhs = _nsa_prompt(qn, qr, kcmp, vcmp, ks_r, kw_r, proj, gates, z, B, S)
    h1, h1b = _proj_residual([lhs], prm["w_out_c"], li, h, True)
    h2 = _ple(h1, h1b, p, prm["w_ple_gate"], prm["w_ple"], layer)
    kvs = (B, S, C_KV_HEADS, HEAD_DIM)
    col = lambda off: proj[:, off:off + C_KV_WIDTH].reshape(kvs)
    keep = min(C_WIN, S)
    cmp_rows = jnp.stack([col(C_KC), col(C_VC)], axis=1)
    sel_rows = jnp.stack([ks_r.reshape(kvs), col(C_VS)], axis=1)
    win_rows = jnp.stack([kw_r.reshape(kvs)[:, S - keep:], col(C_VW)[:, S - keep:]], axis=1)
    return h2, cmp_rows, sel_rows, win_rows


def _prompt_trunk(x, p, prm):
    B, S, _ = x.shape
    tables = _rope_tables(S, 0, 1)
    h = x.reshape(B * S, D_MODEL)
    a_rows, b_rows, cmp_rows, sel_rows, win_rows = [], [], [], [], []
    depth = p.shape[0]
    for i in range(depth):
        li = i // 2
        pi = p[i].reshape(B * S, PLE_DIM)
        if i % 2 == 0:
            h, kv, vb = _ab_layer_prompt(h, None, li, i, pi, prm, tables, B, S)
            a_rows.append(kv)
            b_rows.append(vb)
        else:
            h, c, s, w = _c_layer_prompt(h, li, i, pi, prm, tables, B, S)
            cmp_rows.append(c)
            sel_rows.append(s)
            win_rows.append(w)
    st = lambda xs: jnp.stack(xs, axis=1)
    return h.reshape(B, S, D_MODEL), st(a_rows), st(b_rows), st(cmp_rows), st(sel_rows), st(win_rows)


def _pad_rows(x, rows):
    return jnp.concatenate([x, jnp.zeros((rows - x.shape[0], x.shape[1]), x.dtype)], axis=0)


def _bf_round(x):
    return x.astype(BF).astype(F32)


def _dilated_sample_kernel(q_ref, kn_ref, pj_ref, kp_ref, vp_ref, o_ref, *, P):
    h = pl.program_id(1)
    qs = [q_ref[pl.ds(g * A_HEADS + h, 1), :] for g in range(A_GROUPS)]
    qmat = _pad_rows(jnp.concatenate(qs, axis=0), SUBLANES)
    k_new = kn_ref[pl.ds(h, 1), :]
    v_new = pj_ref[pl.ds(AB_V // HEAD_DIM + h, 1), :]
    z = pj_ref[pl.ds(AB_ZA // HEAD_DIM + h, 1), :]
    grp = lax.broadcasted_iota(jnp.int32, (SUBLANES, P), 0)
    delta = P - lax.broadcasted_iota(jnp.int32, (SUBLANES, P), 1)
    dil = jnp.where(grp == 0, A_PATTERNS[0][1], jnp.where(grp == 1, A_PATTERNS[1][1], A_PATTERNS[2][1]))
    win = jnp.where(grp == 0, A_PATTERNS[0][0], jnp.where(grp == 1, A_PATTERNS[1][0], A_PATTERNS[2][0]))
    valid = ((delta & (dil - 1)) == 0) & (delta <= win) & (grp < A_GROUPS)
    s = jnp.where(valid, _dot_nt(qmat.astype(BF), kp_ref[...].astype(BF)) * ATTN_SCALE, NEG_INF)
    s_new = jnp.sum(_bf_round(qmat) * _bf_round(k_new), axis=-1, keepdims=True) * ATTN_SCALE
    m = jnp.maximum(jnp.max(s, axis=-1, keepdims=True), s_new)
    e = jnp.exp(s - m)
    e_new = jnp.exp(s_new - m)
    den = jnp.sum(e, axis=-1, keepdims=True) + e_new
    num = _dot(e.astype(BF), vp_ref[...].astype(BF)) + _bf_round(e_new) * _bf_round(v_new)
    live = lax.broadcasted_iota(jnp.int32, (SUBLANES, 1), 0) < A_GROUPS
    m_all = jnp.max(jnp.where(live, m, NEG_INF), axis=0, keepdims=True)
    w = jnp.where(live, jnp.exp(m - m_all), 0.0)
    num_t = jnp.sum(w * num, axis=0, keepdims=True)
    den_t = jnp.sum(w * den, axis=0, keepdims=True)
    o_ref[pl.ds(h, 1), :] = (num_t / den_t) * _silu(z)


def _dilated_sample(q_r, k_r, proj, cache, li):
    DB, P = cache.shape[0], cache.shape[3]
    cache = cache.reshape(cache.shape[:4] + (A_WIDTH,))
    row3 = lambda x: x.reshape(DB, x.shape[1] // HEAD_DIM, HEAD_DIM)
    full = lambda x: pl.BlockSpec((None,) + x.shape[1:], lambda b, h: (b, 0, 0))
    q3, k3, p3 = row3(q_r), row3(k_r), row3(proj)
    cspec = lambda kv: pl.BlockSpec((None, None, None, P, HEAD_DIM), lambda b, h: (b, li, kv, 0, h))
    out = pl.pallas_call(
        functools.partial(_dilated_sample_kernel, P=P), grid=(DB, A_HEADS),
        in_specs=[full(q3), full(k3), full(p3), cspec(0), cspec(1)],
        out_specs=pl.BlockSpec((None, A_HEADS, HEAD_DIM), lambda b, h: (b, 0, 0)),
        out_shape=jax.ShapeDtypeStruct((DB, A_HEADS, HEAD_DIM), F32),
        compiler_params=_params("arbitrary", "arbitrary"), name="dilated_sample",
    )(q3, k3, p3, cache, cache)
    return out.reshape(DB, A_WIDTH)


def _gmlp_sample_kernel(u_ref, v_ref, z_ref, g_ref, w0_ref, b0_ref, vn_ref, o_ref):
    vn = _layer_norm(v_ref[...], g_ref[...])
    vn_ref[...] = vn
    mixed = _bf_round(w0_ref[...]) * _bf_round(vn) + b0_ref[...]
    o_ref[...] = (u_ref[...] * mixed) * _silu(z_ref[...])


def _gmlp_sample(proj, ln_sgu, ws_b, bs_b, li):
    M, wb = proj.shape[0], B_WIDTH
    blk = lambda off: pl.BlockSpec((M, wb), lambda i: (0, off // wb))
    vec = pl.BlockSpec((1, wb), lambda i: (0, 0))
    w0 = jnp.repeat(ws_b[li, :, 0, 0], wb // B_GROUPS)[None]
    b0 = jnp.repeat(bs_b[li, :, 0], wb // B_GROUPS)[None]
    shp = jax.ShapeDtypeStruct((M, wb), F32)
    return pl.pallas_call(
        _gmlp_sample_kernel, in_specs=[blk(AB_UB), blk(AB_VB), blk(AB_ZB), vec, vec, vec],
        out_specs=[pl.BlockSpec((M, wb), lambda i: (0, 0))] * 2, out_shape=[shp, shp], grid=(1,),
        compiler_params=_params("arbitrary"), name="gmlp_sample",
    )(proj, proj, proj, ln_sgu[li][None], w0, b0)


def _page_compress_kernel(pt_ref, cwk_ref, cwv_ref, *refs, pages, li):
    o_ref = refs[pages]
    per_page = o_ref.shape[1] // pages
    wbase = li * CMP_LEN * C_KV_HEADS
    seg_stride = CMP_STRIDE * C_KV_HEADS
    for j in range(pages):
        page = refs[j]
        for kv, w_ref in enumerate((cwk_ref, cwv_ref)):
            for h in range(C_KV_HEADS):
                first = jnp.zeros((per_page, HEAD_DIM), F32)
                second = jnp.zeros((per_page, HEAD_DIM), F32)
                for r in range(CMP_STRIDE):
                    rows = page[kv, pl.ds(r * C_KV_HEADS + h, per_page, stride=seg_stride), :]
                    first = first + rows * w_ref[wbase + r * C_KV_HEADS + h]
                    second = second + rows * w_ref[wbase + (CMP_STRIDE + r) * C_KV_HEADS + h]
                out_rows = slice(j * per_page, (j + 1) * per_page)
                out_cols = slice(h * HEAD_DIM, (h + 1) * HEAD_DIM)
                o_ref[2 * kv, out_rows, out_cols] = first
                o_ref[2 * kv + 1, out_rows, out_cols] = second


def _page_compress(pool, page_table, cw_k, cw_v, li, pages=4):
    DB, n_pages = page_table.shape
    page = pool.shape[3]
    pool = pool.reshape(pool.shape[:3] + (page * C_KV_HEADS, HEAD_DIM))
    per_page = page // CMP_STRIDE
    smem = pl.BlockSpec(memory_space=pltpu.SMEM)

    def pspec(j):
        return pl.BlockSpec((None, None, 2, page * C_KV_HEADS, HEAD_DIM),
                            lambda b, i, pt: (pt[b * n_pages + i * pages + j], li, 0, 0, 0))

    return pl.pallas_call(
        functools.partial(_page_compress_kernel, pages=pages, li=li),
        grid_spec=pltpu.PrefetchScalarGridSpec(
            num_scalar_prefetch=1, grid=(DB, n_pages // pages),
            in_specs=[smem, smem] + [pspec(j) for j in range(pages)],
            out_specs=pl.BlockSpec((None, 4, per_page * pages, C_KV_WIDTH), lambda b, i, pt: (b, 0, i, 0))),
        out_shape=jax.ShapeDtypeStruct((DB, 4, n_pages * per_page, C_KV_WIDTH), F32),
        compiler_params=_params("arbitrary", "arbitrary"), name="page_compress",
    )(page_table.reshape(-1), cw_k.reshape(-1), cw_v.reshape(-1), *([pool] * pages))


def _cmp_sample_kernel(fs_ref, q_ref, g_ref, o_ref, ix_ref, *, pos, n_sel, nj):
    h = pl.program_id(1)
    nseg = fs_ref.shape[1]
    seg_col = lax.broadcasted_iota(jnp.int32, (nseg, HEAD_DIM), 0)
    complete = seg_col < nseg - 1
    kc = jnp.where(complete, fs_ref[0] + pltpu.roll(fs_ref[1], nseg - 1, 0), 0.0)
    vc = jnp.where(complete, fs_ref[2] + pltpu.roll(fs_ref[3], nseg - 1, 0), 0.0)
    kc = kc * lax.rsqrt(jnp.mean(kc * kc, axis=-1, keepdims=True) + NORM_EPS) * g_ref[...]
    q8 = _pad_rows(q_ref[pl.ds(h * C_GQA, C_GQA), :], SUBLANES)
    seg = lax.broadcasted_iota(jnp.int32, (SUBLANES, nseg), 1)
    vis = seg * CMP_STRIDE + (CMP_LEN - 1) <= pos
    s = jnp.where(vis, _dot_nt(q8.astype(BF), kc.astype(BF)) * ATTN_SCALE, NEG_INF)
    m = jnp.max(s, axis=-1, keepdims=True)
    m = jnp.where(m == NEG_INF, 0.0, m)
    e = jnp.exp(s - m)
    prob = e / jnp.maximum(jnp.sum(e, axis=-1, keepdims=True), TINY)
    live = lax.broadcasted_iota(jnp.int32, (SUBLANES, 1), 0) < C_GQA
    prob_b = jnp.where(live, prob, 0.0).astype(BF)
    o_cmp = _dot(prob_b, vc.astype(BF))
    o_ref[pl.ds(h * C_GQA, C_GQA), :] = o_cmp[:C_GQA]

    imp = jnp.sum(_dot(prob_b, _cmp_to_sel(nseg, nj, -1).astype(BF)), axis=0, keepdims=True)
    lane = lax.broadcasted_iota(jnp.int32, (1, nj), 1)
    cur = pos // SEL_BLOCK
    forced = (lane == 0) | (lane == cur) | (lane == cur - 1)
    score = jnp.where(forced, imp + FORCE_BONUS, imp)
    score = jnp.where((lane <= cur) & (lane < n_sel), score, NEG_INF)

    ii = lax.broadcasted_iota(jnp.int32, (nj, nj), 0)
    jj = lax.broadcasted_iota(jnp.int32, (nj, nj), 1)
    s_row = jnp.broadcast_to(score, (nj, nj))
    s_col = jnp.sum(jnp.where(ii == jj, s_row, 0.0), axis=-1, keepdims=True)
    ahead = (s_row > s_col) | ((s_row == s_col) & (jj < ii))
    rank = jnp.sum(jnp.where(ahead, 1.0, 0.0), axis=-1, keepdims=True)
    slot = lax.broadcasted_iota(jnp.int32, (nj, LANES), 1).astype(F32)
    blk_id = lax.broadcasted_iota(jnp.int32, (nj, LANES), 0)
    hit = rank == slot
    idx = jnp.sum(jnp.where(hit, blk_id, 0), axis=0, keepdims=True)
    ok = jnp.sum(jnp.where(hit & (s_col > NEG_INF), 1, 0), axis=0, keepdims=True)
    ix_ref[...] = jnp.concatenate([idx, ok, jnp.zeros((SUBLANES - 2, LANES), jnp.int32)], axis=0)


def _cmp_sample(fs, qn, gain, pos):
    DB, _, nseg, _ = fs.shape
    total = pos + 1
    n_sel = -(-total // SEL_BLOCK)
    nj = -(-n_sel // LANES) * LANES
    q3 = qn.reshape(DB, C_HEADS, HEAD_DIM)
    return pl.pallas_call(
        functools.partial(_cmp_sample_kernel, pos=pos, n_sel=n_sel, nj=nj), grid=(DB, C_KV_HEADS),
        in_specs=[pl.BlockSpec((None, 4, nseg, HEAD_DIM), lambda b, h: (b, 0, 0, h)),
                  pl.BlockSpec((None, C_HEADS, HEAD_DIM), lambda b, h: (b, 0, 0)),
                  pl.BlockSpec((1, HEAD_DIM), lambda b, h: (0, 0))],
        out_specs=[pl.BlockSpec((None, C_HEADS, HEAD_DIM), lambda b, h: (b, 0, 0)),
                   pl.BlockSpec((None, None, SUBLANES, LANES), lambda b, h: (b, h, 0, 0))],
        out_shape=[jax.ShapeDtypeStruct((DB, C_HEADS, HEAD_DIM), F32),
                   jax.ShapeDtypeStruct((DB, C_KV_HEADS, SUBLANES, LANES), jnp.int32)],
        compiler_params=_params("arbitrary", "arbitrary"), name="cmp_sample",
    )(fs, q3, gain)


def _sel_sample_kernel(pt_ref, ix_ref, ok_ref, q_ref, kn_ref, pj_ref, *refs, nb, pos, n_past):
    k_refs, v_refs = refs[:nb], refs[nb:2 * nb]
    o_ref, m_ref, l_ref, acc_ref = refs[2 * nb:]
    b, h, i = pl.program_id(0), pl.program_id(1), pl.program_id(2)

    @pl.when(i == 0)
    def _():
        m_ref[...] = jnp.full(m_ref.shape, NEG_INF, F32)
        l_ref[...] = jnp.zeros(l_ref.shape, F32)
        acc_ref[...] = jnp.zeros(acc_ref.shape, F32)

    q8 = _pad_rows(q_ref[pl.ds(h * C_GQA, C_GQA), :], SUBLANES).astype(BF)
    k_new = kn_ref[pl.ds(h, 1), :]
    v_new = pj_ref[pl.ds(C_VS // HEAD_DIM + h, 1), :]
    first_row = lax.broadcasted_iota(jnp.int32, (SEL_BLOCK, HEAD_DIM), 0) == 0
    r = lax.broadcasted_iota(jnp.int32, (1, SEL_BLOCK), 1)
    for j in range(nb):
        slot = (b * C_KV_HEADS + h) * SEL_TOP + i * nb + j
        blk = ix_ref[slot]
        is_past = blk < n_past
        kt = jnp.where(is_past, k_refs[j][...], jnp.where(first_row, k_new, 0.0))
        vt = jnp.where(is_past, v_refs[j][...], jnp.where(first_row, v_new, 0.0))
        ok = (blk * SEL_BLOCK + r <= pos) & (ok_ref[slot] > 0)
        s = jnp.where(ok, _dot_nt(q8, kt.astype(BF)) * ATTN_SCALE, NEG_INF)
        m_old = m_ref[...]
        m_new = jnp.maximum(m_old, jnp.max(s, axis=-1, keepdims=True))
        m_safe = jnp.where(m_new == NEG_INF, 0.0, m_new)
        alpha = jnp.exp(m_old - m_safe)
        e = jnp.exp(s - m_safe)
        l_ref[...] = alpha * l_ref[...] + jnp.sum(e, axis=-1, keepdims=True)
        acc_ref[...] = alpha * acc_ref[...] + _dot(e.astype(BF), vt.astype(BF))
        m_ref[...] = m_new

    @pl.when(i == pl.num_programs(2) - 1)
    def _():
        o_ref[pl.ds(h * C_GQA, C_GQA), :] = (acc_ref[...] / l_ref[...])[:C_GQA]


def _sel_sample(pool, page_table, idx, ok, qr, ks_r, proj, li, pos, nb=4):
    DB, n_pages = page_table.shape
    page = pool.shape[3]
    bpp = page // SEL_BLOCK
    n_past = n_pages * bpp
    pool = pool.reshape(pool.shape[:4] + (C_KV_WIDTH,))
    row3 = lambda x: x.reshape(DB, x.shape[1] // HEAD_DIM, HEAD_DIM)
    q3, k3, p3 = row3(qr), row3(ks_r), row3(proj)
    full = lambda x: pl.BlockSpec((None,) + x.shape[1:], lambda b, h, i, pt, ix, okf: (b, 0, 0))

    def bspec(j, kv):
        def imap(b, h, i, pt, ix, okf):
            blk = jnp.clip(ix[(b * C_KV_HEADS + h) * SEL_TOP + i * nb + j], 0, n_past - 1)
            return (pt[b * n_pages + blk // bpp], li, kv, blk % bpp, h)
        return pl.BlockSpec((None, None, None, SEL_BLOCK, HEAD_DIM), imap)

    out = pl.pallas_call(
        functools.partial(_sel_sample_kernel, nb=nb, pos=pos, n_past=n_past),
        grid_spec=pltpu.PrefetchScalarGridSpec(
            num_scalar_prefetch=3, grid=(DB, C_KV_HEADS, SEL_TOP // nb),
            in_specs=[full(q3), full(k3), full(p3)] + [bspec(j, 0) for j in range(nb)]
            + [bspec(j, 1) for j in range(nb)],
            out_specs=pl.BlockSpec((None, C_HEADS, HEAD_DIM), lambda b, h, i, pt, ix, okf: (b, 0, 0)),
            scratch_shapes=[pltpu.VMEM((SUBLANES, 1), F32), pltpu.VMEM((SUBLANES, 1), F32),
                            pltpu.VMEM((SUBLANES, HEAD_DIM), F32)]),
        out_shape=jax.ShapeDtypeStruct((DB, C_HEADS, HEAD_DIM), F32),
        compiler_params=_params("arbitrary", "arbitrary", "arbitrary"), name="sel_sample",
    )(page_table.reshape(-1), idx, ok, q3, k3, p3, *([pool] * (2 * nb)))
    return out


def _win_sample_kernel(q_ref, kn_ref, pj_ref, kp_ref, vp_ref, oc_ref, os_ref, g_ref, z_ref, o_ref):
    h = pl.program_id(1)
    heads = pl.ds(h * C_GQA, C_GQA)
    q8 = _pad_rows(q_ref[heads, :], SUBLANES)
    k_new = kn_ref[pl.ds(h, 1), :]
    v_new = pj_ref[pl.ds(C_VW // HEAD_DIM + h, 1), :]
    s = _dot_nt(q8.astype(BF), kp_ref[...].astype(BF)) * ATTN_SCALE
    s_new = jnp.sum(_bf_round(q8) * _bf_round(k_new), axis=-1, keepdims=True) * ATTN_SCALE
    m = jnp.maximum(jnp.max(s, axis=-1, keepdims=True), s_new)
    e = jnp.exp(s - m)
    e_new = jnp.exp(s_new - m)
    den = jnp.sum(e, axis=-1, keepdims=True) + e_new
    p_past = (e / den).astype(BF)
    o_win = _dot(p_past, vp_ref[...].astype(BF)) + _bf_round(e_new / den) * _bf_round(v_new)
    g = g_ref[pl.ds(h, 1), :]
    for a in range(C_GQA):
        gate = [_sigmoid(g[:, br * C_GQA + a:br * C_GQA + a + 1]) for br in range(3)]
        row = pl.ds(h * C_GQA + a, 1)
        o = gate[0] * oc_ref[row, :] + gate[1] * os_ref[row, :] + gate[2] * o_win[a:a + 1]
        o_ref[row, :] = o * _silu(z_ref[row, :])


def _win_sample(cache, qr, kw_r, proj, o_cmp, o_sel, gates, z, li):
    DB, W = cache.shape[0], cache.shape[3]
    assert W <= C_WIN
    cache = cache.reshape(cache.shape[:4] + (C_KV_WIDTH,))
    row3 = lambda x: x.reshape(DB, x.shape[1] // HEAD_DIM, HEAD_DIM)
    full = lambda x: pl.BlockSpec((None,) + x.shape[1:], lambda b, h: (b, 0, 0))
    ins = [row3(qr), row3(kw_r), row3(proj)]
    tail = [o_cmp, o_sel, row3(gates), row3(z)]
    cspec = lambda kv: pl.BlockSpec((None, None, None, W, HEAD_DIM), lambda b, h: (b, li, kv, 0, h))
    out = pl.pallas_call(
        _win_sample_kernel, grid=(DB, C_KV_HEADS),
        in_specs=[full(x) for x in ins] + [cspec(0), cspec(1)] + [full(x) for x in tail],
        out_specs=pl.BlockSpec((None, C_HEADS, HEAD_DIM), lambda b, h: (b, 0, 0)),
        out_shape=jax.ShapeDtypeStruct((DB, C_HEADS, HEAD_DIM), F32),
        compiler_params=_params("arbitrary", "arbitrary"), name="win_sample",
    )(*ins, cache, cache, *tail)
    return out.reshape(DB, C_WIDTH)


def _ab_layer_sample(h, li, layer, p, prm, tables, cache_a):
    DB = h.shape[0]
    xn = _rms_rows(h, prm["ln_ab"], li, F32)
    proj = _matmul(xn, prm["w_in_ab"], li, 0, prm["w_in_ab"].shape[2])
    q_r = _head_norm(proj, AB_Q, A_GROUPS * A_WIDTH, prm["qn_a"][li][None], tables)
    k_r = _head_norm(proj, AB_K, A_WIDTH, prm["kn_a"][li][None], tables)
    ga = _dilated_sample(q_r, k_r, proj, cache_a, li)
    vn, gb = _gmlp_sample(proj, prm["ln_sgu"], prm["ws_b"], prm["bs_b"], li)
    h1, _ = _proj_residual([ga, gb], prm["w_out_ab"], li, h, False)
    h2 = _ple(h1, h1, p, prm["w_ple_gate"], prm["w_ple"], layer)
    kv = jnp.stack([k_r.reshape(DB, 1, A_HEADS, HEAD_DIM),
                    proj[:, AB_V:AB_V + A_WIDTH].reshape(DB, 1, A_HEADS, HEAD_DIM)], axis=1)
    return h2, kv, vn.reshape(DB, 1, B_WIDTH)


def _c_layer_sample(h, li, layer, p, prm, tables, past, pos):
    DB = h.shape[0]
    w = prm["w_in_c"]
    xn = _rms_rows(h, prm["ln_c"], li, F32)
    proj = _matmul(xn, w, li, 0, C_QKV_COLS)
    z = _matmul(xn, w[li, :, C_Z:], None, 0, C_WIDTH)
    gates = _matmul(xn, _gate_weights(w, li), None, 0, C_KV_HEADS * LANES)
    qn, qr = _head_norm(proj, C_Q, C_WIDTH, prm["qn_c"][li][None], tables, plain=True)
    ks_r = _head_norm(proj, C_KS, C_KV_WIDTH, prm["kn_c"][li, 1][None], tables)
    kw_r = _head_norm(proj, C_KW, C_KV_WIDTH, prm["kn_c"][li, 2][None], tables)
    fs = _page_compress(past["cmp"], past["page_table"], prm["cw_k"], prm["cw_v"], li)
    o_cmp, picks = _cmp_sample(fs, qn, prm["kn_c"][li, 0][None], pos)
    idx = picks[:, :, 0, :SEL_TOP].reshape(-1)
    ok = picks[:, :, 1, :SEL_TOP].reshape(-1)
    o_sel = _sel_sample(past["sel"], past["page_table"], idx, ok, qr, ks_r, proj, li, pos)
    lhs = _win_sample(past["win"], qr, kw_r, proj, o_cmp, o_sel, gates, z, li)
    h1, _ = _proj_residual([lhs], prm["w_out_c"], li, h, False)
    h2 = _ple(h1, h1, p, prm["w_ple_gate"], prm["w_ple"], layer)
    kvs = (DB, 1, C_KV_HEADS, HEAD_DIM)
    col = lambda off: proj[:, off:off + C_KV_WIDTH].reshape(kvs)
    cmp_rows = jnp.stack([col(C_KC), col(C_VC)], axis=1)
    sel_rows = jnp.stack([ks_r.reshape(kvs), col(C_VS)], axis=1)
    win_rows = jnp.stack([kw_r.reshape(kvs), col(C_VW)], axis=1)
    return h2, cmp_rows, sel_rows, win_rows


def _sample_trunk(x, p, past, prm):
    DB, S, _ = x.shape
    assert S == 1
    pos = past["page_table"].shape[1] * past["cmp"].shape[3]
    assert pos % CMP_STRIDE == 0 and past["cmp"].shape[3] % SEL_BLOCK == 0
    tables = _rope_tables(DB, pos, 0)
    h = x.reshape(DB, D_MODEL)
    a_rows, b_rows, cmp_rows, sel_rows, win_rows = [], [], [], [], []
    for i in range(p.shape[0]):
        li = i // 2
        pi = p[i].reshape(DB, PLE_DIM)
        if i % 2 == 0:
            h, kv, vb = _ab_layer_sample(h, li, i, pi, prm, tables, past["a"])
            a_rows.append(kv)
            b_rows.append(vb)
        else:
            h, c, s, w = _c_layer_sample(h, li, i, pi, prm, tables, past, pos)
            cmp_rows.append(c)
            sel_rows.append(s)
            win_rows.append(w)
    st = lambda xs: jnp.stack(xs, axis=1)
    return h.reshape(DB, S, D_MODEL), st(a_rows), st(b_rows), st(cmp_rows), st(sel_rows), st(win_rows)


def kernel(x_prompt, x_sample, cache_a_kv, cache_c_cmp_kv, cache_c_sel_kv, cache_c_win_kv, page_table,
           p_prompt, p_sample, ln_ab, w_in_ab, qn_a, kn_a, ln_sgu, ws_b, bs_b, w_out_ab,
           ln_c, w_in_c, qn_c, kn_c, cw_k, cw_v, w_out_c, w_ple, w_ple_gate):
    prm = dict(ln_ab=ln_ab, w_in_ab=w_in_ab, qn_a=qn_a, kn_a=kn_a, ln_sgu=ln_sgu, ws_b=ws_b, bs_b=bs_b,
               w_out_ab=w_out_ab, ln_c=ln_c, w_in_c=w_in_c, qn_c=qn_c, kn_c=kn_c, cw_k=cw_k, cw_v=cw_v,
               w_out_c=w_out_c, w_ple=w_ple, w_ple_gate=w_ple_gate)
    y_p, a_p, b_p, cmp_p, sel_p, win_p = _prompt_trunk(x_prompt, p_prompt, prm)
    past = dict(a=cache_a_kv, cmp=cache_c_cmp_kv, sel=cache_c_sel_kv, win=cache_c_win_kv, page_table=page_table)
    y_s, a_s, b_s, cmp_s, sel_s, win_s = _sample_trunk(x_sample, p_sample, past, prm)
    return (y_p, y_s, a_p, a_s, b_p, b_s, cmp_p, cmp_s, sel_p, sel_s, win_p, win_s)
```

```python
import functools
import math

import jax
import jax.numpy as jnp
from jax import lax
from jax.experimental import pallas as pl
from jax.experimental.pallas import tpu as pltpu

F32 = jnp.float32
BF = jnp.bfloat16

D_MODEL = 2048
HEAD_DIM = 128
ROPE_DIM = HEAD_DIM // 4
ROPE_HALF = ROPE_DIM // 2
ROPE_THETA = 500000.0
NORM_EPS = 1e-6
Q_BLOCK = 128
PLE_DIM = 256
ATTN_SCALE = HEAD_DIM ** -0.5
TINY = 1e-30
A_HEADS = D_MODEL // (2 * HEAD_DIM)
A_PATTERNS = ((128, 1), (512, 4), (2048, 16))
A_GROUPS = len(A_PATTERNS)
A_WIDTH = A_HEADS * HEAD_DIM
B_CHUNK = 128
B_WIDTH = D_MODEL - A_WIDTH
B_GROUPS = 8
C_HEADS = D_MODEL // HEAD_DIM
C_KV_HEADS = 4
C_GQA = C_HEADS // C_KV_HEADS
C_WIDTH = C_HEADS * HEAD_DIM
C_KV_WIDTH = C_KV_HEADS * HEAD_DIM
CMP_LEN = 32
CMP_STRIDE = 16
SEL_BLOCK = 64
SEL_TOP = 16
C_WIN = 512
FORCE_BONUS = 1000.0
NEG_INF = float("-inf")

LANES = 128
SUBLANES = 8
MM_TILE_M = 1024
MM_TILE_N = 512
ROW_TILE = 512
SAMPLE_ROWS = 16
VMEM_LIMIT = 56 * 1024 * 1024
DILATED_UNROLL = 4

AB_Q, AB_K, AB_V, AB_ZA, AB_UB, AB_VB, AB_ZB = 0, 3072, 4096, 5120, 6144, 7168, 8192
C_Q, C_KC, C_VC, C_KS, C_VS, C_KW, C_VW, C_G, C_Z = 0, 2048, 2560, 3072, 3584, 4096, 4608, 5120, 5168
C_QKV_COLS = 5120


def _params(*sem):
    return pltpu.CompilerParams(dimension_semantics=sem, vmem_limit_bytes=VMEM_LIMIT)


def _dot(a, b):
    return jnp.dot(a, b, preferred_element_type=F32)


def _dot_nt(a, b):
    return lax.dot_general(a, b, (((1,), (1,)), ((), ())), preferred_element_type=F32)


def _sigmoid(x):
    return jax.nn.sigmoid(x)


def _silu(x):
    return x * _sigmoid(x)


def _rope_table_kernel(cos_ref, sa_ref, sb_ref, *, start, step):
    rows = cos_ref.shape[0]
    r = lax.broadcasted_iota(jnp.int32, (rows, HEAD_DIM), 0)
    lane = lax.broadcasted_iota(jnp.int32, (rows, HEAD_DIM), 1)
    pos = (start + r * step).astype(F32)
    j = (lane & (ROPE_HALF - 1)).astype(F32)
    inv = jnp.exp(-math.log(ROPE_THETA) * j / ROPE_HALF)
    ang = pos * inv
    c, s = jnp.cos(ang), jnp.sin(ang)
    in_rope = lane < ROPE_DIM
    cos_ref[...] = jnp.where(in_rope, c, 1.0)
    sa_ref[...] = jnp.where(in_rope & (lane >= ROPE_HALF), s, 0.0)
    sb_ref[...] = jnp.where(lane < ROPE_HALF, -s, 0.0)


def _rope_tables(rows, start, step):
    shp = jax.ShapeDtypeStruct((rows, HEAD_DIM), F32)
    return pl.pallas_call(
        functools.partial(_rope_table_kernel, start=start, step=step),
        out_shape=(shp, shp, shp), name="rope_tables")()


def _rms_kernel(x_ref, g_ref, o_ref):
    x = x_ref[...]
    y = x * lax.rsqrt(jnp.mean(x * x, axis=-1, keepdims=True) + NORM_EPS)
    o_ref[...] = (y * g_ref[...]).astype(o_ref.dtype)


def _rms_rows(x, gains, li, out_dtype):
    M, D = x.shape
    tm = min(M, ROW_TILE)
    return pl.pallas_call(
        _rms_kernel, grid=(M // tm,),
        in_specs=[pl.BlockSpec((tm, D), lambda i: (i, 0)),
                  pl.BlockSpec((None, 1, D), lambda i: (li, 0, 0))],
        out_specs=pl.BlockSpec((tm, D), lambda i: (i, 0)),
        out_shape=jax.ShapeDtypeStruct((M, D), out_dtype),
        compiler_params=_params("arbitrary"), name="rms_rows",
    )(x, gains.reshape(gains.shape[0], 1, D))


def _head_norm_kernel(x_ref, g_ref, cos_ref, sa_ref, sb_ref, *out_refs, heads, rope, plain):
    g = g_ref[...]
    for j in range(heads):
        cols = slice(j * HEAD_DIM, (j + 1) * HEAD_DIM)
        x = x_ref[:, cols]
        y = x * lax.rsqrt(jnp.mean(x * x, axis=-1, keepdims=True) + NORM_EPS) * g
        if plain:
            out_refs[0][:, cols] = y
        if rope:
            up = pltpu.roll(y, ROPE_HALF, 1)
            down = pltpu.roll(y, HEAD_DIM - ROPE_HALF, 1)
            out_refs[-1][:, cols] = y * cos_ref[...] + up * sa_ref[...] + down * sb_ref[...]


def _head_norm(x, col0, ncols, gain, tables, *, rope=True, plain=False):
    M = x.shape[0]
    cos, sa, sb = tables
    trows = cos.shape[0]
    tm = min(M, ROW_TILE, trows)
    tc = 512
    heads = tc // HEAD_DIM
    nt = trows // tm
    n_out = int(rope) + int(plain)
    tspec = pl.BlockSpec((tm, HEAD_DIM), lambda i, j: (i % nt, 0))
    ospec = pl.BlockSpec((tm, tc), lambda i, j: (i, j))
    outs = pl.pallas_call(
        functools.partial(_head_norm_kernel, heads=heads, rope=rope, plain=plain),
        grid=(M // tm, ncols // tc),
        in_specs=[pl.BlockSpec((tm, tc), lambda i, j: (i, j + col0 // tc)),
                  pl.BlockSpec((1, HEAD_DIM), lambda i, j: (0, 0)), tspec, tspec, tspec],
        out_specs=[ospec] * n_out,
        out_shape=[jax.ShapeDtypeStruct((M, ncols), F32)] * n_out,
        compiler_params=_params("arbitrary", "arbitrary"), name="head_norm",
    )(x, gain, cos, sa, sb)
    return outs if n_out > 1 else outs[0]


def _mm_kernel(x_ref, w_ref, o_ref, wb_ref):
    @pl.when(pl.program_id(1) == 0)
    def _():
        wb_ref[...] = w_ref[...].astype(BF)

    o_ref[...] = _dot(x_ref[...].astype(BF), wb_ref[...])


def _matmul(x, w, li, col0, ncols, tn=MM_TILE_N):
    M, K = x.shape
    tm = min(M, MM_TILE_M)
    if li is None:
        wspec = pl.BlockSpec((K, tn), lambda n, m: (0, n + col0 // tn))
    else:
        wspec = pl.BlockSpec((None, K, tn), lambda n, m: (li, 0, n + col0 // tn))
    return pl.pallas_call(
        _mm_kernel, grid=(ncols // tn, M // tm),
        in_specs=[pl.BlockSpec((tm, K), lambda n, m: (m, 0)), wspec],
        out_specs=pl.BlockSpec((tm, tn), lambda n, m: (m, n)),
        out_shape=jax.ShapeDtypeStruct((M, ncols), F32),
        scratch_shapes=[pltpu.VMEM((K, tn), BF)],
        compiler_params=_params("arbitrary", "arbitrary"), name="matmul",
    )(x, w)


def _proj_res_kernel(*refs, ks, with_bf):
    n = len(ks)
    lhs = refs[:n]
    w_ref, res_ref, o_ref = refs[n:n + 3]
    wb_ref = refs[-1]

    @pl.when(pl.program_id(1) == 0)
    def _():
        wb_ref[...] = w_ref[...].astype(BF)

    acc = res_ref[...]
    off = 0
    for r, k in zip(lhs, ks):
        acc = acc + _dot(r[...].astype(BF), wb_ref[off:off + k, :])
        off += k
    o_ref[...] = acc
    if with_bf:
        refs[n + 3][...] = acc.astype(BF)


def _proj_residual(lhs_list, w, li, res, with_bf):
    M, N = res.shape
    ks = tuple(a.shape[1] for a in lhs_list)
    K = sum(ks)
    tm, tn = min(M, MM_TILE_M), MM_TILE_N
    ospec = pl.BlockSpec((tm, tn), lambda n, m: (m, n))
    out_shape = [jax.ShapeDtypeStruct((M, N), F32)]
    if with_bf:
        out_shape.append(jax.ShapeDtypeStruct((M, N), BF))
    outs = pl.pallas_call(
        functools.partial(_proj_res_kernel, ks=ks, with_bf=with_bf),
        grid=(N // tn, M // tm),
        in_specs=[pl.BlockSpec((tm, k), lambda n, m: (m, 0)) for k in ks]
        + [pl.BlockSpec((None, K, tn), lambda n, m: (li, 0, n)), ospec],
        out_specs=[ospec] * len(out_shape), out_shape=out_shape,
        scratch_shapes=[pltpu.VMEM((K, tn), BF)],
        compiler_params=_params("arbitrary", "arbitrary"), name="proj_residual",
    )(*lhs_list, w, res)
    return outs if with_bf else (outs[0], outs[0])


def _ple_kernel(hl_ref, wg_ref, p_ref, wp_ref, h_ref, o_ref, wgb_ref, wpb_ref):
    @pl.when(pl.program_id(1) == 0)
    def _():
        wgb_ref[...] = wg_ref[...].astype(BF)
        wpb_ref[...] = wp_ref[...].astype(BF)

    gate = _sigmoid(_dot(hl_ref[...].astype(BF), wgb_ref[...]))
    pp = _dot(p_ref[...].astype(BF), wpb_ref[...])
    o_ref[...] = h_ref[...] + gate * pp


def _ple(h, h_lhs, p, w_gate, w_ple, layer):
    M, N = h.shape
    K, KP = h_lhs.shape[1], p.shape[1]
    tm, tn = min(M, MM_TILE_M), MM_TILE_N
    ospec = pl.BlockSpec((tm, tn), lambda n, m: (m, n))
    return pl.pallas_call(
        _ple_kernel, grid=(N // tn, M // tm),
        in_specs=[pl.BlockSpec((tm, K), lambda n, m: (m, 0)),
                  pl.BlockSpec((None, K, tn), lambda n, m: (layer, 0, n)),
                  pl.BlockSpec((tm, KP), lambda n, m: (m, 0)),
                  pl.BlockSpec((None, KP, tn), lambda n, m: (layer, 0, n)), ospec],
        out_specs=ospec, out_shape=jax.ShapeDtypeStruct((M, N), F32),
        scratch_shapes=[pltpu.VMEM((K, tn), BF), pltpu.VMEM((KP, tn), BF)],
        compiler_params=_params("arbitrary", "arbitrary"), name="ple",
    )(h_lhs, w_gate, p, w_ple, h)


def _dilated_kernel(q0_ref, q1_ref, q2_ref, k_ref, v_ref, z_ref, o_ref, num_ref, m_ref, l_ref, *, S):
    blk = Q_BLOCK
    row = lax.broadcasted_iota(jnp.int32, (blk, blk), 0)
    col = lax.broadcasted_iota(jnp.int32, (blk, blk), 1)
    q_refs = (q0_ref, q1_ref, q2_ref)
    for g, (window, d) in enumerate(A_PATTERNS):
        assert window // d == blk
        nblk = S // d // blk
        q_ref = q_refs[g]

        def unit(u, carry, g=g, d=d, nblk=nblk, q_ref=q_ref):
            c = u // nblk
            i = u - c * nblk
            start = c + d * blk * i
            rows = pl.ds(start, blk, stride=d) if d > 1 else pl.ds(pl.multiple_of(start, blk), blk)
            q = q_ref[rows, :].astype(BF)
            kc = k_ref[rows, :].astype(BF)
            vc = v_ref[rows, :].astype(BF)
            s_c = jnp.where(col <= row, _dot_nt(q, kc) * ATTN_SCALE, NEG_INF)
            m = jnp.max(s_c, axis=-1, keepdims=True)
            if nblk > 1:
                pstart = jnp.maximum(start - d * blk, c)
                prows = pl.ds(pstart, blk, stride=d) if d > 1 else pl.ds(pl.multiple_of(pstart, blk), blk)
                kp = k_ref[prows, :].astype(BF)
                vp = v_ref[prows, :].astype(BF)
                s_p = jnp.where((col >= row) & (i > 0), _dot_nt(q, kp) * ATTN_SCALE, NEG_INF)
                m = jnp.maximum(m, jnp.max(s_p, axis=-1, keepdims=True))
            e_c = jnp.exp(s_c - m)
            l = jnp.sum(e_c, axis=-1, keepdims=True)
            num = _dot(e_c.astype(BF), vc)
            if nblk > 1:
                e_p = jnp.exp(s_p - m)
                l = l + jnp.sum(e_p, axis=-1, keepdims=True)
                num = num + _dot(e_p.astype(BF), vp)
            num_ref[g, rows, :] = num
            m_ref[g, rows, :] = jnp.broadcast_to(m, (blk, HEAD_DIM))
            l_ref[g, rows, :] = jnp.broadcast_to(l, (blk, HEAD_DIM))
            return carry

        lax.fori_loop(0, d * nblk, unit, 0, unroll=DILATED_UNROLL)

    def merge(i, carry):
        rows = pl.ds(pl.multiple_of(i * blk, blk), blk)
        ms = [m_ref[g, rows, :] for g in range(A_GROUPS)]
        m_all = jnp.maximum(jnp.maximum(ms[0], ms[1]), ms[2])
        ws = [jnp.exp(m - m_all) for m in ms]
        num = ws[0] * num_ref[0, rows, :] + ws[1] * num_ref[1, rows, :] + ws[2] * num_ref[2, rows, :]
        den = ws[0] * l_ref[0, rows, :] + ws[1] * l_ref[1, rows, :] + ws[2] * l_ref[2, rows, :]
        o_ref[rows, :] = ((num / den) * _silu(z_ref[rows, :])).astype(o_ref.dtype)
        return carry

    lax.fori_loop(0, S // blk, merge, 0, unroll=2)


def _dilated_prompt(q_r, k_r, proj, B, S):
    H = A_HEADS
    hb = lambda off: off // HEAD_DIM

    def col_spec(base):
        return pl.BlockSpec((S, HEAD_DIM), lambda b, h: (b, base + h))

    return pl.pallas_call(
        functools.partial(_dilated_kernel, S=S), grid=(B, H),
        in_specs=[col_spec(0), col_spec(H), col_spec(2 * H), col_spec(0),
                  col_spec(hb(AB_V)), col_spec(hb(AB_ZA))],
        out_specs=col_spec(0),
        out_shape=jax.ShapeDtypeStruct((B * S, A_WIDTH), BF),
        scratch_shapes=[pltpu.VMEM((A_GROUPS, S, HEAD_DIM), F32)] * 3,
        compiler_params=_params("arbitrary", "arbitrary"), name="dilated_prompt",
    )(q_r, q_r, q_r, k_r, proj, proj)


def _layer_norm(v, g):
    xc = v - jnp.mean(v, axis=-1, keepdims=True)
    return xc * lax.rsqrt(jnp.mean(xc * xc, axis=-1, keepdims=True) + NORM_EPS) * g


def _gmlp_kernel(u_ref, v_ref, z_ref, g_ref, ws_ref, bst_ref, vn_ref, o_ref):
    vn = _layer_norm(v_ref[...], g_ref[...])
    vn_ref[...] = vn
    row = lax.broadcasted_iota(jnp.int32, (B_CHUNK, B_CHUNK), 0)
    col = lax.broadcasted_iota(jnp.int32, (B_CHUNK, B_CHUNK), 1)
    for g in range(B_GROUPS):
        cols = slice(g * LANES, (g + 1) * LANES)
        w = jnp.where(row >= col, ws_ref[g], 0.0).astype(BF)
        mixed = _dot(w, vn[:, cols].astype(BF)) + bst_ref[:, g:g + 1]
        o_ref[:, cols] = ((u_ref[:, cols] * mixed) * _silu(z_ref[:, cols])).astype(o_ref.dtype)


def _gmlp_prompt(proj, ln_sgu, ws_b, bs_b, li):
    M = proj.shape[0]
    wb = B_WIDTH
    blk = lambda off: pl.BlockSpec((B_CHUNK, wb), lambda i: (i, off // wb))
    bst = jnp.swapaxes(bs_b, 1, 2)
    return pl.pallas_call(
        _gmlp_kernel, grid=(M // B_CHUNK,),
        in_specs=[blk(AB_UB), blk(AB_VB), blk(AB_ZB),
                  pl.BlockSpec((None, 1, wb), lambda i: (li, 0, 0)),
                  pl.BlockSpec((None, B_GROUPS, B_CHUNK, B_CHUNK), lambda i: (li, 0, 0, 0)),
                  pl.BlockSpec((None, B_CHUNK, B_GROUPS), lambda i: (li, 0, 0))],
        out_specs=[pl.BlockSpec((B_CHUNK, wb), lambda i: (i, 0))] * 2,
        out_shape=[jax.ShapeDtypeStruct((M, wb), F32), jax.ShapeDtypeStruct((M, wb), BF)],
        compiler_params=_params("arbitrary"), name="gmlp_prompt",
    )(proj, proj, proj, ln_sgu.reshape(-1, 1, wb), ws_b, bst)


def _compress_rows(x_ref, w_ref, wbase, h, nseg):
    first = jnp.zeros((nseg, HEAD_DIM), F32)
    second = jnp.zeros((nseg, HEAD_DIM), F32)
    for r in range(CMP_STRIDE):
        rows = x_ref[pl.ds(r, nseg, stride=CMP_STRIDE), :]
        first = first + rows * w_ref[wbase + r * C_KV_HEADS + h]
        second = second + rows * w_ref[wbase + (CMP_STRIDE + r) * C_KV_HEADS + h]
    c = first + pltpu.roll(second, nseg - 1, 0)
    seg = lax.broadcasted_iota(jnp.int32, (nseg, HEAD_DIM), 0)
    return jnp.where(seg < nseg - 1, c, 0.0)


def _compress_kernel(cwk_ref, cwv_ref, kc_ref, vc_ref, g_ref, ko_ref, vo_ref, *, li, nseg):
    h = pl.program_id(1)
    wbase = li * CMP_LEN * C_KV_HEADS
    kc = _compress_rows(kc_ref, cwk_ref, wbase, h, nseg)
    ko_ref[...] = kc * lax.rsqrt(jnp.mean(kc * kc, axis=-1, keepdims=True) + NORM_EPS) * g_ref[...]
    vo_ref[...] = _compress_rows(vc_ref, cwv_ref, wbase, h, nseg)


def _compress_prompt(proj, cw_k, cw_v, gain, li, B, S):
    nseg = S // CMP_STRIDE
    hb = lambda off: off // HEAD_DIM
    smem = pl.BlockSpec(memory_space=pltpu.SMEM)
    ospec = pl.BlockSpec((None, None, nseg, HEAD_DIM), lambda b, h: (b, h, 0, 0))
    oshape = jax.ShapeDtypeStruct((B, C_KV_HEADS, nseg, HEAD_DIM), F32)
    return pl.pallas_call(
        functools.partial(_compress_kernel, li=li, nseg=nseg), grid=(B, C_KV_HEADS),
        in_specs=[smem, smem,
                  pl.BlockSpec((S, HEAD_DIM), lambda b, h: (b, hb(C_KC) + h)),
                  pl.BlockSpec((S, HEAD_DIM), lambda b, h: (b, hb(C_VC) + h)),
                  pl.BlockSpec((1, HEAD_DIM), lambda b, h: (0, 0))],
        out_specs=[ospec, ospec], out_shape=[oshape, oshape],
        compiler_params=_params("arbitrary", "arbitrary"), name="compress_prompt",
    )(cw_k.reshape(-1), cw_v.reshape(-1), proj, proj, gain)


def _cmp_to_sel(nrows, ncols, seg_mask):
    n = lax.broadcasted_iota(jnp.int32, (nrows, ncols), 0) & seg_mask
    j = lax.broadcasted_iota(jnp.int32, (nrows, ncols), 1)
    shared = (jnp.minimum(n * CMP_STRIDE + CMP_LEN, (j + 1) * SEL_BLOCK)
              - jnp.maximum(n * CMP_STRIDE, j * SEL_BLOCK))
    return jnp.maximum(shared, 0).astype(F32) / CMP_LEN


def _cmp_to_sel_t(nrows, ncols, seg_mask):
    j = lax.broadcasted_iota(jnp.int32, (nrows, ncols), 0)
    n = lax.broadcasted_iota(jnp.int32, (nrows, ncols), 1) & seg_mask
    shared = (jnp.minimum(n * CMP_STRIDE + CMP_LEN, (j + 1) * SEL_BLOCK)
              - jnp.maximum(n * CMP_STRIDE, j * SEL_BLOCK))
    return jnp.maximum(shared, 0).astype(F32) / CMP_LEN


def _rank_select_t(score, rows):
    sc = score[:rows]
    blk = lax.broadcasted_iota(jnp.int32, sc.shape, 0)
    rank = jnp.zeros(sc.shape, F32)
    for jp in range(rows):
        other = sc[jp:jp + 1, :]
        ahead = (other > sc) | ((other == sc) & (blk > jp))
        rank = rank + jnp.where(ahead, 1.0, 0.0)
    picked = jnp.where((rank < SEL_TOP) & (sc > NEG_INF), 1.0, 0.0)
    return jnp.concatenate([picked, jnp.zeros((score.shape[0] - rows, score.shape[1]), F32)], axis=0)


def _flash_loop_t(q4, k_ref, vt_ref, n_tiles, tk, bias_fn):
    cols = q4.shape[0]
    reps = cols // LANES

    def body(t, carry):
        m, l, acc = carry
        k0 = pl.multiple_of(t * tk, tk)
        s = _dot_nt(k_ref[pl.ds(k0, tk), :], q4) * ATTN_SCALE
        s = s + jnp.concatenate([bias_fn(k0)] * reps, axis=1)
        m_new = jnp.maximum(m, jnp.max(s, axis=0, keepdims=True))
        m_safe = jnp.where(m_new == NEG_INF, 0.0, m_new)
        alpha = jnp.exp(m - m_safe)
        e = jnp.exp(s - m_safe)
        l = alpha * l + jnp.sum(e, axis=0, keepdims=True)
        acc = alpha * acc + _dot(vt_ref[t], e.astype(BF))
        return m_new, l, acc

    init = (jnp.full((1, cols), NEG_INF, F32), jnp.zeros((1, cols), F32), jnp.zeros((HEAD_DIM, cols), F32))
    _, l, acc = lax.fori_loop(0, n_tiles, body, init)
    return acc / l


def _stack_heads(ref):
    return jnp.concatenate([ref[:, g * HEAD_DIM:(g + 1) * HEAD_DIM] for g in range(C_GQA)], axis=0)


def _nsa_kernel(qn_ref, qr_ref, kcmp_ref, vcmp_ref, ks_ref, vs_ref, kw_ref, vw_ref, g_ref, z_ref, o_ref,
                ksb_ref, kwb_ref, vst_ref, vwt_ref, vct_ref, exp_ref, *, S, n_sel, sel_tk):
    qb = Q_BLOCK
    qi = pl.program_id(2)
    q0 = qi * qb
    nseg = kcmp_ref.shape[0]
    cols4 = C_GQA * qb
    win_keys = C_WIN + qb

    @pl.when(qi == 0)
    def _():
        ksb_ref[...] = ks_ref[...].astype(BF)
        kwb_ref[...] = kw_ref[...].astype(BF)
        for t in range(S // sel_tk):
            vst_ref[t] = vs_ref[t * sel_tk:(t + 1) * sel_tk, :].T.astype(BF)
        for t in range(S // qb):
            vwt_ref[t] = vw_ref[t * qb:(t + 1) * qb, :].T.astype(BF)
        vct_ref[...] = vcmp_ref[...].T.astype(BF)
        key_blk = lax.broadcasted_iota(jnp.int32, (S, LANES), 0) // SEL_BLOCK
        exp_ref[...] = jnp.where(key_blk == lax.broadcasted_iota(jnp.int32, (S, LANES), 1), 1.0, 0.0).astype(BF)

    p_q = q0 + lax.broadcasted_iota(jnp.int32, (1, qb), 1)
    p_4 = q0 + (lax.broadcasted_iota(jnp.int32, (1, cols4), 1) & (qb - 1))

    qn4 = _stack_heads(qn_ref).astype(BF)
    seg = lax.broadcasted_iota(jnp.int32, (nseg, cols4), 0)
    vis = seg * CMP_STRIDE + (CMP_LEN - 1) <= p_4
    s = jnp.where(vis, _dot_nt(kcmp_ref[...].astype(BF), qn4) * ATTN_SCALE, NEG_INF)
    m = jnp.max(s, axis=0, keepdims=True)
    m = jnp.where(m == NEG_INF, 0.0, m)
    e = jnp.exp(s - m)
    prob_b = (e / jnp.maximum(jnp.sum(e, axis=0, keepdims=True), TINY)).astype(BF)
    o_cmp = _dot(vct_ref[...], prob_b)

    prob_stack = jnp.concatenate([prob_b[:, g * qb:(g + 1) * qb] for g in range(C_GQA)], axis=0)
    imp = _dot(_cmp_to_sel_t(LANES, C_GQA * nseg, nseg - 1).astype(BF), prob_stack)
    blk = lax.broadcasted_iota(jnp.int32, (LANES, qb), 0)
    cur = p_q // SEL_BLOCK
    forced = (blk == 0) | (blk == cur) | (blk == cur - 1)
    score = jnp.where(forced, imp + FORCE_BONUS, imp)
    score = jnp.where(blk <= cur, score, NEG_INF)
    sel = _rank_select_t(score, -(-n_sel // SUBLANES) * SUBLANES).astype(BF)

    qr4 = _stack_heads(qr_ref).astype(BF)

    def sel_bias(k0):
        chosen = _dot(exp_ref[pl.ds(k0, sel_tk), :], sel)
        kpos = k0 + lax.broadcasted_iota(jnp.int32, (sel_tk, 1), 0)
        return jnp.where((chosen > 0.5) & (kpos <= p_q), 0.0, NEG_INF)

    o_sel = _flash_loop_t(qr4, ksb_ref, vst_ref, (q0 + qb + sel_tk - 1) // sel_tk, sel_tk, sel_bias)

    w0 = pl.multiple_of(jnp.maximum(q0 - C_WIN, 0), qb)
    s = _dot_nt(kwb_ref[pl.ds(w0, win_keys), :], qr4) * ATTN_SCALE
    kpos = w0 + lax.broadcasted_iota(jnp.int32, (win_keys, 1), 0)
    bias = jnp.where((kpos <= p_q) & (kpos >= p_q - C_WIN), 0.0, NEG_INF)
    s = s + jnp.concatenate([bias] * C_GQA, axis=1)
    e = jnp.exp(s - jnp.max(s, axis=0, keepdims=True))
    den = jnp.sum(e, axis=0, keepdims=True)
    e = e.astype(BF)
    t0 = w0 // qb
    acc = _dot(vwt_ref[t0], e[:qb])
    for i in range(1, win_keys // qb):
        acc = acc + _dot(vwt_ref[t0 + i], e[i * qb:(i + 1) * qb])
    o_win = acc / den

    gates_t = _sigmoid(g_ref[...].T)
    for g in range(C_GQA):
        cs = slice(g * HEAD_DIM, (g + 1) * HEAD_DIM)
        o = (gates_t[g:g + 1] * o_cmp[:, cs] + gates_t[C_GQA + g:C_GQA + g + 1] * o_sel[:, cs]
             + gates_t[2 * C_GQA + g:2 * C_GQA + g + 1] * o_win[:, cs])
        o_ref[:, cs] = (o.T * _silu(z_ref[:, cs])).astype(o_ref.dtype)


def _nsa_prompt(qn, qr, kcmp, vcmp, ks_r, kw_r, proj, gates, z, B, S):
    nq = S // Q_BLOCK
    nseg = kcmp.shape[2]
    gw = C_GQA * HEAD_DIM
    sel_tk = 512
    assert S >= C_WIN + Q_BLOCK and S % sel_tk == 0
    hb = lambda off: off // HEAD_DIM
    qspec = pl.BlockSpec((Q_BLOCK, gw), lambda b, h, i: (b * nq + i, h))
    cspec = pl.BlockSpec((None, None, nseg, HEAD_DIM), lambda b, h, i: (b, h, 0, 0))

    def kv_spec(base):
        return pl.BlockSpec((S, HEAD_DIM), lambda b, h, i: (b, base + h))

    return pl.pallas_call(
        functools.partial(_nsa_kernel, S=S, n_sel=-(-S // SEL_BLOCK), sel_tk=sel_tk),
        grid=(B, C_KV_HEADS, nq),
        in_specs=[qspec, qspec, cspec, cspec, kv_spec(0), kv_spec(hb(C_VS)), kv_spec(0), kv_spec(hb(C_VW)),
                  pl.BlockSpec((Q_BLOCK, LANES), lambda b, h, i: (b * nq + i, h)), qspec],
        out_specs=qspec, out_shape=jax.ShapeDtypeStruct((B * S, C_WIDTH), BF),
        scratch_shapes=[pltpu.VMEM((S, HEAD_DIM), BF), pltpu.VMEM((S, HEAD_DIM), BF),
                        pltpu.VMEM((S // sel_tk, HEAD_DIM, sel_tk), BF),
                        pltpu.VMEM((S // Q_BLOCK, HEAD_DIM, Q_BLOCK), BF),
                        pltpu.VMEM((HEAD_DIM, nseg), BF), pltpu.VMEM((S, LANES), BF)],
        compiler_params=_params("arbitrary", "arbitrary", "arbitrary"), name="nsa_prompt",
    )(qn, qr, kcmp, vcmp, ks_r, proj, kw_r, proj, gates, z)


def _gate_weights(w_in_c, li):
    wg = w_in_c[li, :, C_G:C_Z].reshape(D_MODEL, 3, C_KV_HEADS, C_GQA)
    wg = jnp.transpose(wg, (0, 2, 1, 3)).reshape(D_MODEL, C_KV_HEADS, 3 * C_GQA)
    wg = jnp.pad(wg, ((0, 0), (0, 0), (0, LANES - 3 * C_GQA)))
    return wg.reshape(D_MODEL, C_KV_HEADS * LANES)


def _ab_layer_prompt(h, hb, li, layer, p, prm, tables, B, S):
    xn = _rms_rows(h, prm["ln_ab"], li, BF)
    proj = _matmul(xn, prm["w_in_ab"], li, 0, prm["w_in_ab"].shape[2])
    q_r = _head_norm(proj, AB_Q, A_GROUPS * A_WIDTH, prm["qn_a"][li][None], tables)
    k_r = _head_norm(proj, AB_K, A_WIDTH, prm["kn_a"][li][None], tables)
    ga = _dilated_prompt(q_r, k_r, proj, B, S)
    vn, gb = _gmlp_prompt(proj, prm["ln_sgu"], prm["ws_b"], prm["bs_b"], li)
    h1, h1b = _proj_residual([ga, gb], prm["w_out_ab"], li, h, True)
    h2 = _ple(h1, h1b, p, prm["w_ple_gate"], prm["w_ple"], layer)
    kv = jnp.stack([k_r.reshape(B, S, A_HEADS, HEAD_DIM),
                    proj[:, AB_V:AB_V + A_WIDTH].reshape(B, S, A_HEADS, HEAD_DIM)], axis=1)
    chunk_start = ((S - 1) // B_CHUNK) * B_CHUNK
    return h2, kv, vn.reshape(B, S, B_WIDTH)[:, chunk_start:]


def _c_layer_prompt(h, li, layer, p, prm, tables, B, S):
    w = prm["w_in_c"]
    xn = _rms_rows(h, prm["ln_c"], li, BF)
    proj = _matmul(xn, w, li, 0, C_QKV_COLS)
    z = _matmul(xn, w[li, :, C_Z:], None, 0, C_WIDTH)
    gates = _matmul(xn, _gate_weights(w, li), None, 0, C_KV_HEADS * LANES)
    qn, qr = _head_norm(proj, C_Q, C_WIDTH, prm["qn_c"][li][None], tables, plain=True)
    ks_r = _head_norm(proj, C_KS, C_KV_WIDTH, prm["kn_c"][li, 1][None], tables)
    kw_r = _head_norm(proj, C_KW, C_KV_WIDTH, prm["kn_c"][li, 2][None], tables)
    kcmp, vcmp = _compress_prompt(proj, prm["cw_k"], prm["cw_v"], prm["kn_c"][li, 0][None], li, B, S)
    lhs = _nsa_prompt(qn, qr, kcmp, vcmp, ks_r, kw_r, proj, gates, z, B, S)
    h1, h1b = _proj_residual([lhs], prm["w_out_c"], li, h, True)
    h2 = _ple(h1, h1b, p, prm["w_ple_gate"], prm["w_ple"], layer)
    kvs = (B, S, C_KV_HEADS, HEAD_DIM)
    col = lambda off: proj[:, off:off + C_KV_WIDTH].reshape(kvs)
    keep = min(C_WIN, S)
    cmp_rows = jnp.stack([col(C_KC), col(C_VC)], axis=1)
    sel_rows = jnp.stack([ks_r.reshape(kvs), col(C_VS)], axis=1)
    win_rows = jnp.stack([kw_r.reshape(kvs)[:, S - keep:], col(C_VW)[:, S - keep:]], axis=1)
    return h2, cmp_rows, sel_rows, win_rows


def _prompt_trunk(x, p, prm):
    B, S, _ = x.shape
    tables = _rope_tables(S, 0, 1)
    h = x.reshape(B * S, D_MODEL)
    a_rows, b_rows, cmp_rows, sel_rows, win_rows = [], [], [], [], []
    depth = p.shape[0]
    for i in range(depth):
        li = i // 2
        pi = p[i].reshape(B * S, PLE_DIM)
        if i % 2 == 0:
            h, kv, vb = _ab_layer_prompt(h, None, li, i, pi, prm, tables, B, S)
            a_rows.append(kv)
            b_rows.append(vb)
        else:
            h, c, s, w = _c_layer_prompt(h, li, i, pi, prm, tables, B, S)
            cmp_rows.append(c)
            sel_rows.append(s)
            win_rows.append(w)
    st = lambda xs: jnp.stack(xs, axis=1)
    return h.reshape(B, S, D_MODEL), st(a_rows), st(b_rows), st(cmp_rows), st(sel_rows), st(win_rows)


def _pad_rows(x, rows):
    return jnp.concatenate([x, jnp.zeros((rows - x.shape[0], x.shape[1]), x.dtype)], axis=0)


def _bf_round(x):
    return x.astype(BF).astype(F32)


def _dilated_sample_kernel(q_ref, kn_ref, pj_ref, kp_ref, vp_ref, o_ref, *, P):
    h = pl.program_id(1)
    qs = [q_ref[pl.ds(g * A_HEADS + h, 1), :] for g in range(A_GROUPS)]
    qmat = _pad_rows(jnp.concatenate(qs, axis=0), SUBLANES)
    k_new = kn_ref[pl.ds(h, 1), :]
    v_new = pj_ref[pl.ds(AB_V // HEAD_DIM + h, 1), :]
    z = pj_ref[pl.ds(AB_ZA // HEAD_DIM + h, 1), :]
    grp = lax.broadcasted_iota(jnp.int32, (SUBLANES, P), 0)
    delta = P - lax.broadcasted_iota(jnp.int32, (SUBLANES, P), 1)
    dil = jnp.where(grp == 0, A_PATTERNS[0][1], jnp.where(grp == 1, A_PATTERNS[1][1], A_PATTERNS[2][1]))
    win = jnp.where(grp == 0, A_PATTERNS[0][0], jnp.where(grp == 1, A_PATTERNS[1][0], A_PATTERNS[2][0]))
    valid = ((delta & (dil - 1)) == 0) & (delta <= win) & (grp < A_GROUPS)
    head_rows = pl.ds(h, P, stride=A_HEADS)
    s = jnp.where(valid, _dot_nt(qmat.astype(BF), kp_ref[head_rows, :].astype(BF)) * ATTN_SCALE, NEG_INF)
    s_new = jnp.sum(_bf_round(qmat) * _bf_round(k_new), axis=-1, keepdims=True) * ATTN_SCALE
    m = jnp.maximum(jnp.max(s, axis=-1, keepdims=True), s_new)
    e = jnp.exp(s - m)
    e_new = jnp.exp(s_new - m)
    den = jnp.sum(e, axis=-1, keepdims=True) + e_new
    num = _dot(e.astype(BF), vp_ref[head_rows, :].astype(BF)) + _bf_round(e_new) * _bf_round(v_new)
    live = lax.broadcasted_iota(jnp.int32, (SUBLANES, 1), 0) < A_GROUPS
    m_all = jnp.max(jnp.where(live, m, NEG_INF), axis=0, keepdims=True)
    w = jnp.where(live, jnp.exp(m - m_all), 0.0)
    num_t = jnp.sum(w * num, axis=0, keepdims=True)
    den_t = jnp.sum(w * den, axis=0, keepdims=True)
    o_ref[pl.ds(h, 1), :] = (num_t / den_t) * _silu(z)


def _dilated_sample(q_r, k_r, proj, cache, li):
    DB, P = cache.shape[0], cache.shape[3]
    cache = cache.reshape(cache.shape[:3] + (P * A_HEADS, HEAD_DIM))
    row3 = lambda x: x.reshape(DB, x.shape[1] // HEAD_DIM, HEAD_DIM)
    full = lambda x: pl.BlockSpec((None,) + x.shape[1:], lambda b, h: (b, 0, 0))
    q3, k3, p3 = row3(q_r), row3(k_r), row3(proj)
    cspec = lambda kv: pl.BlockSpec((None, None, None, P * A_HEADS, HEAD_DIM), lambda b, h: (b, li, kv, 0, 0))
    out = pl.pallas_call(
        functools.partial(_dilated_sample_kernel, P=P), grid=(DB, A_HEADS),
        in_specs=[full(q3), full(k3), full(p3), cspec(0), cspec(1)],
        out_specs=pl.BlockSpec((None, A_HEADS, HEAD_DIM), lambda b, h: (b, 0, 0)),
        out_shape=jax.ShapeDtypeStruct((DB, A_HEADS, HEAD_DIM), F32),
        compiler_params=_params("arbitrary", "arbitrary"), name="dilated_sample",
    )(q3, k3, p3, cache, cache)
    return out.reshape(DB, A_WIDTH)


def _gmlp_sample_kernel(u_ref, v_ref, z_ref, g_ref, w0_ref, b0_ref, vn_ref, o_ref):
    vn = _layer_norm(v_ref[...], g_ref[...])
    vn_ref[...] = vn
    mixed = _bf_round(w0_ref[...]) * _bf_round(vn) + b0_ref[...]
    o_ref[...] = (u_ref[...] * mixed) * _silu(z_ref[...])


def _gmlp_sample(proj, ln_sgu, ws_b, bs_b, li):
    M, wb = proj.shape[0], B_WIDTH
    blk = lambda off: pl.BlockSpec((M, wb), lambda i: (0, off // wb))
    vec = pl.BlockSpec((1, wb), lambda i: (0, 0))
    w0 = jnp.repeat(ws_b[li, :, 0, 0], wb // B_GROUPS)[None]
    b0 = jnp.repeat(bs_b[li, :, 0], wb // B_GROUPS)[None]
    shp = jax.ShapeDtypeStruct((M, wb), F32)
    return pl.pallas_call(
        _gmlp_sample_kernel, in_specs=[blk(AB_UB), blk(AB_VB), blk(AB_ZB), vec, vec, vec],
        out_specs=[pl.BlockSpec((M, wb), lambda i: (0, 0))] * 2, out_shape=[shp, shp], grid=(1,),
        compiler_params=_params("arbitrary"), name="gmlp_sample",
    )(proj, proj, proj, ln_sgu[li][None], w0, b0)


def _page_compress_kernel(pt_ref, cwk_ref, cwv_ref, *refs, pages, li):
    o_ref = refs[pages]
    per_page = o_ref.shape[1] // pages
    wbase = li * CMP_LEN * C_KV_HEADS
    seg_stride = CMP_STRIDE * C_KV_HEADS
    for j in range(pages):
        page = refs[j]
        for kv, w_ref in enumerate((cwk_ref, cwv_ref)):
            for h in range(C_KV_HEADS):
                first = jnp.zeros((per_page, HEAD_DIM), F32)
                second = jnp.zeros((per_page, HEAD_DIM), F32)
                for r in range(CMP_STRIDE):
                    rows = page[kv, pl.ds(r * C_KV_HEADS + h, per_page, stride=seg_stride), :]
                    first = first + rows * w_ref[wbase + r * C_KV_HEADS + h]
                    second = second + rows * w_ref[wbase + (CMP_STRIDE + r) * C_KV_HEADS + h]
                out_rows = slice(j * per_page, (j + 1) * per_page)
                out_cols = slice(h * HEAD_DIM, (h + 1) * HEAD_DIM)
                o_ref[2 * kv, out_rows, out_cols] = first
                o_ref[2 * kv + 1, out_rows, out_cols] = second


def _page_compress(pool, page_table, cw_k, cw_v, li, pages=4):
    DB, n_pages = page_table.shape
    page = pool.shape[3]
    pool = pool.reshape(pool.shape[:3] + (page * C_KV_HEADS, HEAD_DIM))
    per_page = page // CMP_STRIDE
    smem = pl.BlockSpec(memory_space=pltpu.SMEM)

    def pspec(j):
        return pl.BlockSpec((None, None, 2, page * C_KV_HEADS, HEAD_DIM),
                            lambda b, i, pt: (pt[b * n_pages + i * pages + j], li, 0, 0, 0))

    return pl.pallas_call(
        functools.partial(_page_compress_kernel, pages=pages, li=li),
        grid_spec=pltpu.PrefetchScalarGridSpec(
            num_scalar_prefetch=1, grid=(DB, n_pages // pages),
            in_specs=[smem, smem] + [pspec(j) for j in range(pages)],
            out_specs=pl.BlockSpec((None, 4, per_page * pages, C_KV_WIDTH), lambda b, i, pt: (b, 0, i, 0))),
        out_shape=jax.ShapeDtypeStruct((DB, 4, n_pages * per_page, C_KV_WIDTH), F32),
        compiler_params=_params("arbitrary", "arbitrary"), name="page_compress",
    )(page_table.reshape(-1), cw_k.reshape(-1), cw_v.reshape(-1), *([pool] * pages))


def _cmp_sample_kernel(fs_ref, q_ref, g_ref, o_ref, ix_ref, *, pos, n_sel, nj):
    h = pl.program_id(1)
    nseg = fs_ref.shape[1]
    seg_col = lax.broadcasted_iota(jnp.int32, (nseg, HEAD_DIM), 0)
    complete = seg_col < nseg - 1
    kc = jnp.where(complete, fs_ref[0] + pltpu.roll(fs_ref[1], nseg - 1, 0), 0.0)
    vc = jnp.where(complete, fs_ref[2] + pltpu.roll(fs_ref[3], nseg - 1, 0), 0.0)
    kc = kc * lax.rsqrt(jnp.mean(kc * kc, axis=-1, keepdims=True) + NORM_EPS) * g_ref[...]
    q8 = _pad_rows(q_ref[pl.ds(h * C_GQA, C_GQA), :], SUBLANES)
    seg = lax.broadcasted_iota(jnp.int32, (SUBLANES, nseg), 1)
    vis = seg * CMP_STRIDE + (CMP_LEN - 1) <= pos
    s = jnp.where(vis, _dot_nt(q8.astype(BF), kc.astype(BF)) * ATTN_SCALE, NEG_INF)
    m = jnp.max(s, axis=-1, keepdims=True)
    m = jnp.where(m == NEG_INF, 0.0, m)
    e = jnp.exp(s - m)
    prob = e / jnp.maximum(jnp.sum(e, axis=-1, keepdims=True), TINY)
    live = lax.broadcasted_iota(jnp.int32, (SUBLANES, 1), 0) < C_GQA
    prob_b = jnp.where(live, prob, 0.0).astype(BF)
    o_cmp = _dot(prob_b, vc.astype(BF))
    o_ref[pl.ds(h * C_GQA, C_GQA), :] = o_cmp[:C_GQA]

    imp = jnp.sum(_dot(prob_b, _cmp_to_sel(nseg, nj, -1).astype(BF)), axis=0, keepdims=True)
    lane = lax.broadcasted_iota(jnp.int32, (1, nj), 1)
    cur = pos // SEL_BLOCK
    forced = (lane == 0) | (lane == cur) | (lane == cur - 1)
    score = jnp.where(forced, imp + FORCE_BONUS, imp)
    score = jnp.where((lane <= cur) & (lane < n_sel), score, NEG_INF)

    ii = lax.broadcasted_iota(jnp.int32, (nj, nj), 0)
    jj = lax.broadcasted_iota(jnp.int32, (nj, nj), 1)
    s_row = jnp.broadcast_to(score, (nj, nj))
    s_col = jnp.sum(jnp.where(ii == jj, s_row, 0.0), axis=-1, keepdims=True)
    ahead = (s_row > s_col) | ((s_row == s_col) & (jj < ii))
    rank = jnp.sum(jnp.where(ahead, 1.0, 0.0), axis=-1, keepdims=True)
    slot = lax.broadcasted_iota(jnp.int32, (nj, LANES), 1).astype(F32)
    blk_id = lax.broadcasted_iota(jnp.int32, (nj, LANES), 0)
    hit = rank == slot
    idx = jnp.sum(jnp.where(hit, blk_id, 0), axis=0, keepdims=True)
    ok = jnp.sum(jnp.where(hit & (s_col > NEG_INF), 1, 0), axis=0, keepdims=True)
    ix_ref[...] = jnp.concatenate([idx, ok, jnp.zeros((SUBLANES - 2, LANES), jnp.int32)], axis=0)


def _cmp_sample(fs, qn, gain, pos):
    DB, _, nseg, _ = fs.shape
    total = pos + 1
    n_sel = -(-total // SEL_BLOCK)
    nj = -(-n_sel // LANES) * LANES
    q3 = qn.reshape(DB, C_HEADS, HEAD_DIM)
    return pl.pallas_call(
        functools.partial(_cmp_sample_kernel, pos=pos, n_sel=n_sel, nj=nj), grid=(DB, C_KV_HEADS),
        in_specs=[pl.BlockSpec((None, 4, nseg, HEAD_DIM), lambda b, h: (b, 0, 0, h)),
                  pl.BlockSpec((None, C_HEADS, HEAD_DIM), lambda b, h: (b, 0, 0)),
                  pl.BlockSpec((1, HEAD_DIM), lambda b, h: (0, 0))],
        out_specs=[pl.BlockSpec((None, C_HEADS, HEAD_DIM), lambda b, h: (b, 0, 0)),
                   pl.BlockSpec((None, None, SUBLANES, LANES), lambda b, h: (b, h, 0, 0))],
        out_shape=[jax.ShapeDtypeStruct((DB, C_HEADS, HEAD_DIM), F32),
                   jax.ShapeDtypeStruct((DB, C_KV_HEADS, SUBLANES, LANES), jnp.int32)],
        compiler_params=_params("arbitrary", "arbitrary"), name="cmp_sample",
    )(fs, q3, gain)


def _sel_sample_kernel(pt_ref, ix_ref, ok_ref, q_ref, kn_ref, pj_ref, *refs, nb, pos, n_past):
    k_refs, v_refs = refs[:nb], refs[nb:2 * nb]
    o_ref, m_ref, l_ref, acc_ref = refs[2 * nb:]
    b, h, i = pl.program_id(0), pl.program_id(1), pl.program_id(2)

    @pl.when(i == 0)
    def _():
        m_ref[...] = jnp.full(m_ref.shape, NEG_INF, F32)
        l_ref[...] = jnp.zeros(l_ref.shape, F32)
        acc_ref[...] = jnp.zeros(acc_ref.shape, F32)

    q8 = _pad_rows(q_ref[pl.ds(h * C_GQA, C_GQA), :], SUBLANES).astype(BF)
    k_new = kn_ref[pl.ds(h, 1), :]
    v_new = pj_ref[pl.ds(C_VS // HEAD_DIM + h, 1), :]
    first_row = lax.broadcasted_iota(jnp.int32, (SEL_BLOCK, HEAD_DIM), 0) == 0
    r = lax.broadcasted_iota(jnp.int32, (1, SEL_BLOCK), 1)
    head_rows = pl.ds(h, SEL_BLOCK, stride=C_KV_HEADS)
    for j in range(nb):
        slot = (b * C_KV_HEADS + h) * SEL_TOP + i * nb + j
        blk = ix_ref[slot]
        is_past = blk < n_past
        kt = jnp.where(is_past, k_refs[j][head_rows, :], jnp.where(first_row, k_new, 0.0))
        vt = jnp.where(is_past, v_refs[j][head_rows, :], jnp.where(first_row, v_new, 0.0))
        ok = (blk * SEL_BLOCK + r <= pos) & (ok_ref[slot] > 0)
        s = jnp.where(ok, _dot_nt(q8, kt.astype(BF)) * ATTN_SCALE, NEG_INF)
        m_old = m_ref[...]
        m_new = jnp.maximum(m_old, jnp.max(s, axis=-1, keepdims=True))
        m_safe = jnp.where(m_new == NEG_INF, 0.0, m_new)
        alpha = jnp.exp(m_old - m_safe)
        e = jnp.exp(s - m_safe)
        l_ref[...] = alpha * l_ref[...] + jnp.sum(e, axis=-1, keepdims=True)
        acc_ref[...] = alpha * acc_ref[...] + _dot(e.astype(BF), vt.astype(BF))
        m_ref[...] = m_new

    @pl.when(i == pl.num_programs(2) - 1)
    def _():
        o_ref[pl.ds(h * C_GQA, C_GQA), :] = (acc_ref[...] / l_ref[...])[:C_GQA]


def _sel_sample(pool, page_table, idx, ok, qr, ks_r, proj, li, pos, nb=4):
    DB, n_pages = page_table.shape
    page = pool.shape[3]
    bpp = page // SEL_BLOCK
    n_past = n_pages * bpp
    pool = pool.reshape(pool.shape[:3] + (page * C_KV_HEADS, HEAD_DIM))
    row3 = lambda x: x.reshape(DB, x.shape[1] // HEAD_DIM, HEAD_DIM)
    q3, k3, p3 = row3(qr), row3(ks_r), row3(proj)
    full = lambda x: pl.BlockSpec((None,) + x.shape[1:], lambda b, h, i, pt, ix, okf: (b, 0, 0))

    def bspec(j, kv):
        def imap(b, h, i, pt, ix, okf):
            blk = jnp.clip(ix[(b * C_KV_HEADS + h) * SEL_TOP + i * nb + j], 0, n_past - 1)
            return (pt[b * n_pages + blk // bpp], li, kv, blk % bpp, 0)
        return pl.BlockSpec((None, None, None, SEL_BLOCK * C_KV_HEADS, HEAD_DIM), imap)

    out = pl.pallas_call(
        functools.partial(_sel_sample_kernel, nb=nb, pos=pos, n_past=n_past),
        grid_spec=pltpu.PrefetchScalarGridSpec(
            num_scalar_prefetch=3, grid=(DB, C_KV_HEADS, SEL_TOP // nb),
            in_specs=[full(q3), full(k3), full(p3)] + [bspec(j, 0) for j in range(nb)]
            + [bspec(j, 1) for j in range(nb)],
            out_specs=pl.BlockSpec((None, C_HEADS, HEAD_DIM), lambda b, h, i, pt, ix, okf: (b, 0, 0)),
            scratch_shapes=[pltpu.VMEM((SUBLANES, 1), F32), pltpu.VMEM((SUBLANES, 1), F32),
                            pltpu.VMEM((SUBLANES, HEAD_DIM), F32)]),
        out_shape=jax.ShapeDtypeStruct((DB, C_HEADS, HEAD_DIM), F32),
        compiler_params=_params("arbitrary", "arbitrary", "arbitrary"), name="sel_sample",
    )(page_table.reshape(-1), idx, ok, q3, k3, p3, *([pool] * (2 * nb)))
    return out


def _win_sample_kernel(q_ref, kn_ref, pj_ref, kp_ref, vp_ref, oc_ref, os_ref, g_ref, z_ref, o_ref):
    h = pl.program_id(1)
    heads = pl.ds(h * C_GQA, C_GQA)
    q8 = _pad_rows(q_ref[heads, :], SUBLANES)
    k_new = kn_ref[pl.ds(h, 1), :]
    v_new = pj_ref[pl.ds(C_VW // HEAD_DIM + h, 1), :]
    head_rows = pl.ds(h, kp_ref.shape[0] // C_KV_HEADS, stride=C_KV_HEADS)
    s = _dot_nt(q8.astype(BF), kp_ref[head_rows, :].astype(BF)) * ATTN_SCALE
    s_new = jnp.sum(_bf_round(q8) * _bf_round(k_new), axis=-1, keepdims=True) * ATTN_SCALE
    m = jnp.maximum(jnp.max(s, axis=-1, keepdims=True), s_new)
    e = jnp.exp(s - m)
    e_new = jnp.exp(s_new - m)
    den = jnp.sum(e, axis=-1, keepdims=True) + e_new
    p_past = (e / den).astype(BF)
    o_win = _dot(p_past, vp_ref[head_rows, :].astype(BF)) + _bf_round(e_new / den) * _bf_round(v_new)
    g = g_ref[pl.ds(h, 1), :]
    for a in range(C_GQA):
        gate = [_sigmoid(g[:, br * C_GQA + a:br * C_GQA + a + 1]) for br in range(3)]
        row = pl.ds(h * C_GQA + a, 1)
        o = gate[0] * oc_ref[row, :] + gate[1] * os_ref[row, :] + gate[2] * o_win[a:a + 1]
        o_ref[row, :] = o * _silu(z_ref[row, :])


def _win_sample(cache, qr, kw_r, proj, o_cmp, o_sel, gates, z, li):
    DB, W = cache.shape[0], cache.shape[3]
    assert W <= C_WIN
    cache = cache.reshape(cache.shape[:3] + (W * C_KV_HEADS, HEAD_DIM))
    row3 = lambda x: x.reshape(DB, x.shape[1] // HEAD_DIM, HEAD_DIM)
    full = lambda x: pl.BlockSpec((None,) + x.shape[1:], lambda b, h: (b, 0, 0))
    ins = [row3(qr), row3(kw_r), row3(proj)]
    tail = [o_cmp, o_sel, row3(gates), row3(z)]
    cspec = lambda kv: pl.BlockSpec((None, None, None, W * C_KV_HEADS, HEAD_DIM), lambda b, h: (b, li, kv, 0, 0))
    out = pl.pallas_call(
        _win_sample_kernel, grid=(DB, C_KV_HEADS),
        in_specs=[full(x) for x in ins] + [cspec(0), cspec(1)] + [full(x) for x in tail],
        out_specs=pl.BlockSpec((None, C_HEADS, HEAD_DIM), lambda b, h: (b, 0, 0)),
        out_shape=jax.ShapeDtypeStruct((DB, C_HEADS, HEAD_DIM), F32),
        compiler_params=_params("arbitrary", "arbitrary"), name="win_sample",
    )(*ins, cache, cache, *tail)
    return out.reshape(DB, C_WIDTH)


def _ab_layer_sample(h, li, layer, p, prm, tables, cache_a):
    DB = h.shape[0]
    xn = _rms_rows(h, prm["ln_ab"], li, F32)
    proj = _matmul(xn, prm["w_in_ab"], li, 0, prm["w_in_ab"].shape[2])
    q_r = _head_norm(proj, AB_Q, A_GROUPS * A_WIDTH, prm["qn_a"][li][None], tables)
    k_r = _head_norm(proj, AB_K, A_WIDTH, prm["kn_a"][li][None], tables)
    ga = _dilated_sample(q_r, k_r, proj, cache_a, li)
    vn, gb = _gmlp_sample(proj, prm["ln_sgu"], prm["ws_b"], prm["bs_b"], li)
    h1, _ = _proj_residual([ga, gb], prm["w_out_ab"], li, h, False)
    h2 = _ple(h1, h1, p, prm["w_ple_gate"], prm["w_ple"], layer)
    kv = jnp.stack([k_r.reshape(DB, 1, A_HEADS, HEAD_DIM),
                    proj[:, AB_V:AB_V + A_WIDTH].reshape(DB, 1, A_HEADS, HEAD_DIM)], axis=1)
    return h2, kv, vn.reshape(DB, 1, B_WIDTH)


def _c_layer_sample(h, li, layer, p, prm, tables, past, pos):
    DB = h.shape[0]
    w = prm["w_in_c"]
    xn = _rms_rows(h, prm["ln_c"], li, F32)
    proj = _matmul(xn, w, li, 0, C_QKV_COLS)
    z = _matmul(xn, w[li, :, C_Z:], None, 0, C_WIDTH)
    gates = _matmul(xn, _gate_weights(w, li), None, 0, C_KV_HEADS * LANES)
    qn, qr = _head_norm(proj, C_Q, C_WIDTH, prm["qn_c"][li][None], tables, plain=True)
    ks_r = _head_norm(proj, C_KS, C_KV_WIDTH, prm["kn_c"][li, 1][None], tables)
    kw_r = _head_norm(proj, C_KW, C_KV_WIDTH, prm["kn_c"][li, 2][None], tables)
    fs = _page_compress(past["cmp"], past["page_table"], prm["cw_k"], prm["cw_v"], li)
    o_cmp, picks = _cmp_sample(fs, qn, prm["kn_c"][li, 0][None], pos)
    idx = picks[:, :, 0, :SEL_TOP].reshape(-1)
    ok = picks[:, :, 1, :SEL_TOP].reshape(-1)
    o_sel = _sel_sample(past["sel"], past["page_table"], idx, ok, qr, ks_r, proj, li, pos)
    lhs = _win_sample(past["win"], qr, kw_r, proj, o_cmp, o_sel, gates, z, li)
    h1, _ = _proj_residual([lhs], prm["w_out_c"], li, h, False)
    h2 = _ple(h1, h1, p, prm["w_ple_gate"], prm["w_ple"], layer)
    kvs = (DB, 1, C_KV_HEADS, HEAD_DIM)
    col = lambda off: proj[:, off:off + C_KV_WIDTH].reshape(kvs)
    cmp_rows = jnp.stack([col(C_KC), col(C_VC)], axis=1)
    sel_rows = jnp.stack([ks_r.reshape(kvs), col(C_VS)], axis=1)
    win_rows = jnp.stack([kw_r.reshape(kvs), col(C_VW)], axis=1)
    return h2, cmp_rows, sel_rows, win_rows


def _sample_trunk(x, p, past, prm):
    DB, S, _ = x.shape
    assert S == 1
    pos = past["page_table"].shape[1] * past["cmp"].shape[3]
    assert pos % CMP_STRIDE == 0 and past["cmp"].shape[3] % SEL_BLOCK == 0
    tables = _rope_tables(DB, pos, 0)
    h = x.reshape(DB, D_MODEL)
    a_rows, b_rows, cmp_rows, sel_rows, win_rows = [], [], [], [], []
    for i in range(p.shape[0]):
        li = i // 2
        pi = p[i].reshape(DB, PLE_DIM)
        if i % 2 == 0:
            h, kv, vb = _ab_layer_sample(h, li, i, pi, prm, tables, past["a"])
            a_rows.append(kv)
            b_rows.append(vb)
        else:
            h, c, s, w = _c_layer_sample(h, li, i, pi, prm, tables, past, pos)
            cmp_rows.append(c)
            sel_rows.append(s)
            win_rows.append(w)
    st = lambda xs: jnp.stack(xs, axis=1)
    return h.reshape(DB, S, D_MODEL), st(a_rows), st(b_rows), st(cmp_rows), st(sel_rows), st(win_rows)


def kernel(x_prompt, x_sample, cache_a_kv, cache_c_cmp_kv, cache_c_sel_kv, cache_c_win_kv, page_table,
           p_prompt, p_sample, ln_ab, w_in_ab, qn_a, kn_a, ln_sgu, ws_b, bs_b, w_out_ab,
           ln_c, w_in_c, qn_c, kn_c, cw_k, cw_v, w_out_c, w_ple, w_ple_gate):
    prm = dict(ln_ab=ln_ab, w_in_ab=w_in_ab, qn_a=qn_a, kn_a=kn_a, ln_sgu=ln_sgu, ws_b=ws_b, bs_b=bs_b,
               w_out_ab=w_out_ab, ln_c=ln_c, w_in_c=w_in_c, qn_c=qn_c, kn_c=kn_c, cw_k=cw_k, cw_v=cw_v,
               w_out_c=w_out_c, w_ple=w_ple, w_ple_gate=w_ple_gate)
    y_p, a_p, b_p, cmp_p, sel_p, win_p = _prompt_trunk(x_prompt, p_prompt, prm)
    past = dict(a=cache_a_kv, cmp=cache_c_cmp_kv, sel=cache_c_sel_kv, win=cache_c_win_kv, page_table=page_table)
    y_s, a_s, b_s, cmp_s, sel_s, win_s = _sample_trunk(x_sample, p_sample, past, prm)
    return (y_p, y_s, a_p, a_s, b_p, b_s, cmp_p, cmp_s, sel_p, sel_s, win_p, win_s)
```

```python
import functools
import math

import jax
import jax.numpy as jnp
from jax import lax
from jax.experimental import pallas as pl
from jax.experimental.pallas import tpu as pltpu

F32 = jnp.float32
BF = jnp.bfloat16

D_MODEL = 2048
HEAD_DIM = 128
ROPE_DIM = HEAD_DIM // 4
ROPE_HALF = ROPE_DIM // 2
ROPE_THETA = 500000.0
NORM_EPS = 1e-6
Q_BLOCK = 128
PLE_DIM = 256
ATTN_SCALE = HEAD_DIM ** -0.5
TINY = 1e-30
A_HEADS = D_MODEL // (2 * HEAD_DIM)
A_PATTERNS = ((128, 1), (512, 4), (2048, 16))
A_GROUPS = len(A_PATTERNS)
A_WIDTH = A_HEADS * HEAD_DIM
B_CHUNK = 128
B_WIDTH = D_MODEL - A_WIDTH
B_GROUPS = 8
C_HEADS = D_MODEL // HEAD_DIM
C_KV_HEADS = 4
C_GQA = C_HEADS // C_KV_HEADS
C_WIDTH = C_HEADS * HEAD_DIM
C_KV_WIDTH = C_KV_HEADS * HEAD_DIM
CMP_LEN = 32
CMP_STRIDE = 16
SEL_BLOCK = 64
SEL_TOP = 16
C_WIN = 512
FORCE_BONUS = 1000.0
NEG_INF = float("-inf")

LANES = 128
SUBLANES = 8
MM_TILE_M = 1024
MM_TILE_N = 512
MM_TILE_N_WIDE = 1024
ROW_TILE = 512
EMIT_TILE = 256
SAMPLE_ROWS = 16
VMEM_LIMIT = 56 * 1024 * 1024
DILATED_UNROLL = 4

AB_Q, AB_K, AB_V, AB_ZA, AB_UB, AB_VB, AB_ZB = 0, 3072, 4096, 5120, 6144, 7168, 8192
C_Q, C_KC, C_VC, C_KS, C_VS, C_KW, C_VW, C_G, C_Z = 0, 2048, 2560, 3072, 3584, 4096, 4608, 5120, 5168
C_QKV_COLS = 5120


def _params(*sem):
    return pltpu.CompilerParams(dimension_semantics=sem, vmem_limit_bytes=VMEM_LIMIT)


def _dot(a, b):
    return jnp.dot(a, b, preferred_element_type=F32)


def _dot_nt(a, b):
    return lax.dot_general(a, b, (((1,), (1,)), ((), ())), preferred_element_type=F32)


def _sigmoid(x):
    return jax.nn.sigmoid(x)


def _silu(x):
    return x * _sigmoid(x)


def _rope_table_kernel(cos_ref, sa_ref, sb_ref, *, start, step):
    rows = cos_ref.shape[0]
    r = lax.broadcasted_iota(jnp.int32, (rows, HEAD_DIM), 0)
    lane = lax.broadcasted_iota(jnp.int32, (rows, HEAD_DIM), 1)
    pos = (start + r * step).astype(F32)
    j = (lane & (ROPE_HALF - 1)).astype(F32)
    inv = jnp.exp(-math.log(ROPE_THETA) * j / ROPE_HALF)
    ang = pos * inv
    c, s = jnp.cos(ang), jnp.sin(ang)
    in_rope = lane < ROPE_DIM
    cos_ref[...] = jnp.where(in_rope, c, 1.0)
    sa_ref[...] = jnp.where(in_rope & (lane >= ROPE_HALF), s, 0.0)
    sb_ref[...] = jnp.where(lane < ROPE_HALF, -s, 0.0)


def _rope_tables(rows, start, step):
    shp = jax.ShapeDtypeStruct((rows, HEAD_DIM), F32)
    return pl.pallas_call(
        functools.partial(_rope_table_kernel, start=start, step=step),
        out_shape=(shp, shp, shp), name="rope_tables")()


def _rms_kernel(x_ref, g_ref, o_ref):
    x = x_ref[...]
    y = x * lax.rsqrt(jnp.mean(x * x, axis=-1, keepdims=True) + NORM_EPS)
    o_ref[...] = (y * g_ref[...]).astype(o_ref.dtype)


def _rms_rows(x, gains, li, out_dtype):
    M, D = x.shape
    tm = min(M, ROW_TILE)
    return pl.pallas_call(
        _rms_kernel, grid=(M // tm,),
        in_specs=[pl.BlockSpec((tm, D), lambda i: (i, 0)),
                  pl.BlockSpec((None, 1, D), lambda i: (li, 0, 0))],
        out_specs=pl.BlockSpec((tm, D), lambda i: (i, 0)),
        out_shape=jax.ShapeDtypeStruct((M, D), out_dtype),
        compiler_params=_params("arbitrary"), name="rms_rows",
    )(x, gains.reshape(gains.shape[0], 1, D))


def _head_norm_kernel(x_ref, g_ref, cos_ref, sa_ref, sb_ref, *out_refs, heads, rope, plain):
    g = g_ref[...]
    for j in range(heads):
        cols = slice(j * HEAD_DIM, (j + 1) * HEAD_DIM)
        x = x_ref[:, cols]
        y = x * lax.rsqrt(jnp.mean(x * x, axis=-1, keepdims=True) + NORM_EPS) * g
        if plain:
            out_refs[0][:, cols] = y
        if rope:
            up = pltpu.roll(y, ROPE_HALF, 1)
            down = pltpu.roll(y, HEAD_DIM - ROPE_HALF, 1)
            out_refs[-1][:, cols] = y * cos_ref[...] + up * sa_ref[...] + down * sb_ref[...]


def _head_norm(x, col0, ncols, gain, tables, *, rope=True, plain=False):
    M = x.shape[0]
    cos, sa, sb = tables
    trows = cos.shape[0]
    tm = min(M, ROW_TILE, trows)
    tc = 512
    heads = tc // HEAD_DIM
    nt = trows // tm
    n_out = int(rope) + int(plain)
    tspec = pl.BlockSpec((tm, HEAD_DIM), lambda i, j: (i % nt, 0))
    ospec = pl.BlockSpec((tm, tc), lambda i, j: (i, j))
    outs = pl.pallas_call(
        functools.partial(_head_norm_kernel, heads=heads, rope=rope, plain=plain),
        grid=(M // tm, ncols // tc),
        in_specs=[pl.BlockSpec((tm, tc), lambda i, j: (i, j + col0 // tc)),
                  pl.BlockSpec((1, HEAD_DIM), lambda i, j: (0, 0)), tspec, tspec, tspec],
        out_specs=[ospec] * n_out,
        out_shape=[jax.ShapeDtypeStruct((M, ncols), F32)] * n_out,
        compiler_params=_params("arbitrary", "arbitrary"), name="head_norm",
    )(x, gain, cos, sa, sb)
    return outs if n_out > 1 else outs[0]


def _mm_kernel(x_ref, w_ref, o_ref, wb_ref):
    @pl.when(pl.program_id(1) == 0)
    def _():
        wb_ref[...] = w_ref[...].astype(BF)

    o_ref[...] = _dot(x_ref[...].astype(BF), wb_ref[...])


def _matmul(x, w, li, col0, ncols):
    M, K = x.shape
    tm = min(M, MM_TILE_M)
    tn = MM_TILE_N_WIDE if ncols % MM_TILE_N_WIDE == 0 and col0 % MM_TILE_N_WIDE == 0 else MM_TILE_N
    if li is None:
        wspec = pl.BlockSpec((K, tn), lambda n, m: (0, n + col0 // tn))
    else:
        wspec = pl.BlockSpec((None, K, tn), lambda n, m: (li, 0, n + col0 // tn))
    return pl.pallas_call(
        _mm_kernel, grid=(ncols // tn, M // tm),
        in_specs=[pl.BlockSpec((tm, K), lambda n, m: (m, 0)), wspec],
        out_specs=pl.BlockSpec((tm, tn), lambda n, m: (m, n)),
        out_shape=jax.ShapeDtypeStruct((M, ncols), F32),
        scratch_shapes=[pltpu.VMEM((K, tn), BF)],
        compiler_params=_params("arbitrary", "arbitrary"), name="matmul",
    )(x, w)


def _proj_res_kernel(*refs, ks, with_bf):
    n = len(ks)
    lhs = refs[:n]
    w_ref, res_ref, o_ref = refs[n:n + 3]
    wb_ref = refs[-1]

    @pl.when(pl.program_id(1) == 0)
    def _():
        wb_ref[...] = w_ref[...].astype(BF)

    acc = res_ref[...]
    off = 0
    for r, k in zip(lhs, ks):
        acc = acc + _dot(r[...].astype(BF), wb_ref[off:off + k, :])
        off += k
    o_ref[...] = acc
    if with_bf:
        refs[n + 3][...] = acc.astype(BF)


def _proj_residual(lhs_list, w, li, res, with_bf):
    M, N = res.shape
    ks = tuple(a.shape[1] for a in lhs_list)
    K = sum(ks)
    tm, tn = min(M, MM_TILE_M), MM_TILE_N
    ospec = pl.BlockSpec((tm, tn), lambda n, m: (m, n))
    out_shape = [jax.ShapeDtypeStruct((M, N), F32)]
    if with_bf:
        out_shape.append(jax.ShapeDtypeStruct((M, N), BF))
    outs = pl.pallas_call(
        functools.partial(_proj_res_kernel, ks=ks, with_bf=with_bf),
        grid=(N // tn, M // tm),
        in_specs=[pl.BlockSpec((tm, k), lambda n, m: (m, 0)) for k in ks]
        + [pl.BlockSpec((None, K, tn), lambda n, m: (li, 0, n)), ospec],
        out_specs=[ospec] * len(out_shape), out_shape=out_shape,
        scratch_shapes=[pltpu.VMEM((K, tn), BF)],
        compiler_params=_params("arbitrary", "arbitrary"), name="proj_residual",
    )(*lhs_list, w, res)
    return outs if with_bf else (outs[0], outs[0])


def _ple_kernel(hl_ref, wg_ref, p_ref, wp_ref, h_ref, o_ref, wgb_ref, wpb_ref):
    @pl.when(pl.program_id(1) == 0)
    def _():
        wgb_ref[...] = wg_ref[...].astype(BF)
        wpb_ref[...] = wp_ref[...].astype(BF)

    gate = _sigmoid(_dot(hl_ref[...].astype(BF), wgb_ref[...]))
    pp = _dot(p_ref[...].astype(BF), wpb_ref[...])
    o_ref[...] = h_ref[...] + gate * pp


def _ple(h, h_lhs, p, w_gate, w_ple, layer):
    M, N = h.shape
    K, KP = h_lhs.shape[1], p.shape[2]
    tm, tn = min(M, MM_TILE_M), MM_TILE_N
    ospec = pl.BlockSpec((tm, tn), lambda n, m: (m, n))
    return pl.pallas_call(
        _ple_kernel, grid=(N // tn, M // tm),
        in_specs=[pl.BlockSpec((tm, K), lambda n, m: (m, 0)),
                  pl.BlockSpec((None, K, tn), lambda n, m: (layer, 0, n)),
                  pl.BlockSpec((None, tm, KP), lambda n, m: (layer, m, 0)),
                  pl.BlockSpec((None, KP, tn), lambda n, m: (layer, 0, n)), ospec],
        out_specs=ospec, out_shape=jax.ShapeDtypeStruct((M, N), F32),
        scratch_shapes=[pltpu.VMEM((K, tn), BF), pltpu.VMEM((KP, tn), BF)],
        compiler_params=_params("arbitrary", "arbitrary"), name="ple",
    )(h_lhs, w_gate, p, w_ple, h)


def _dilated_kernel(q0_ref, q1_ref, q2_ref, k_ref, v_ref, z_ref, o_ref, num_ref, m_ref, l_ref, *, S):
    blk = Q_BLOCK
    row = lax.broadcasted_iota(jnp.int32, (blk, blk), 0)
    col = lax.broadcasted_iota(jnp.int32, (blk, blk), 1)
    q_refs = (q0_ref, q1_ref, q2_ref)
    for g, (window, d) in enumerate(A_PATTERNS):
        assert window // d == blk
        nblk = S // d // blk
        q_ref = q_refs[g]

        def unit(u, carry, g=g, d=d, nblk=nblk, q_ref=q_ref):
            c = u // nblk
            i = u - c * nblk
            start = c + d * blk * i
            rows = pl.ds(start, blk, stride=d) if d > 1 else pl.ds(pl.multiple_of(start, blk), blk)
            q = q_ref[rows, :].astype(BF)
            kc = k_ref[rows, :].astype(BF)
            vc = v_ref[rows, :].astype(BF)
            s_c = jnp.where(col <= row, _dot_nt(q, kc) * ATTN_SCALE, NEG_INF)
            m = jnp.max(s_c, axis=-1, keepdims=True)
            if nblk > 1:
                pstart = jnp.maximum(start - d * blk, c)
                prows = pl.ds(pstart, blk, stride=d) if d > 1 else pl.ds(pl.multiple_of(pstart, blk), blk)
                kp = k_ref[prows, :].astype(BF)
                vp = v_ref[prows, :].astype(BF)
                s_p = jnp.where((col >= row) & (i > 0), _dot_nt(q, kp) * ATTN_SCALE, NEG_INF)
                m = jnp.maximum(m, jnp.max(s_p, axis=-1, keepdims=True))
            e_c = jnp.exp(s_c - m)
            l = jnp.sum(e_c, axis=-1, keepdims=True)
            num = _dot(e_c.astype(BF), vc)
            if nblk > 1:
                e_p = jnp.exp(s_p - m)
                l = l + jnp.sum(e_p, axis=-1, keepdims=True)
                num = num + _dot(e_p.astype(BF), vp)
            num_ref[g, rows, :] = num
            m_ref[g, rows, :] = jnp.broadcast_to(m, (blk, HEAD_DIM))
            l_ref[g, rows, :] = jnp.broadcast_to(l, (blk, HEAD_DIM))
            return carry

        lax.fori_loop(0, d * nblk, unit, 0, unroll=DILATED_UNROLL)

    def merge(i, carry):
        rows = pl.ds(pl.multiple_of(i * blk, blk), blk)
        ms = [m_ref[g, rows, :] for g in range(A_GROUPS)]
        m_all = jnp.maximum(jnp.maximum(ms[0], ms[1]), ms[2])
        ws = [jnp.exp(m - m_all) for m in ms]
        num = ws[0] * num_ref[0, rows, :] + ws[1] * num_ref[1, rows, :] + ws[2] * num_ref[2, rows, :]
        den = ws[0] * l_ref[0, rows, :] + ws[1] * l_ref[1, rows, :] + ws[2] * l_ref[2, rows, :]
        o_ref[rows, :] = ((num / den) * _silu(z_ref[rows, :])).astype(o_ref.dtype)
        return carry

    lax.fori_loop(0, S // blk, merge, 0, unroll=2)


def _dilated_prompt(q_r, k_r, proj, B, S):
    H = A_HEADS
    hb = lambda off: off // HEAD_DIM

    def col_spec(base):
        return pl.BlockSpec((S, HEAD_DIM), lambda b, h: (b, base + h))

    return pl.pallas_call(
        functools.partial(_dilated_kernel, S=S), grid=(B, H),
        in_specs=[col_spec(0), col_spec(H), col_spec(2 * H), col_spec(0),
                  col_spec(hb(AB_V)), col_spec(hb(AB_ZA))],
        out_specs=col_spec(0),
        out_shape=jax.ShapeDtypeStruct((B * S, A_WIDTH), BF),
        scratch_shapes=[pltpu.VMEM((A_GROUPS, S, HEAD_DIM), F32)] * 3,
        compiler_params=_params("arbitrary", "arbitrary"), name="dilated_prompt",
    )(q_r, q_r, q_r, k_r, proj, proj)


def _layer_norm(v, g):
    xc = v - jnp.mean(v, axis=-1, keepdims=True)
    return xc * lax.rsqrt(jnp.mean(xc * xc, axis=-1, keepdims=True) + NORM_EPS) * g


def _gmlp_kernel(u_ref, v_ref, z_ref, g_ref, ws_ref, bst_ref, vn_ref, o_ref):
    vn = _layer_norm(v_ref[...], g_ref[...])
    vn_ref[...] = vn
    row = lax.broadcasted_iota(jnp.int32, (B_CHUNK, B_CHUNK), 0)
    col = lax.broadcasted_iota(jnp.int32, (B_CHUNK, B_CHUNK), 1)
    for g in range(B_GROUPS):
        cols = slice(g * LANES, (g + 1) * LANES)
        w = jnp.where(row >= col, ws_ref[g], 0.0).astype(BF)
        mixed = _dot(w, vn[:, cols].astype(BF)) + bst_ref[:, g:g + 1]
        o_ref[:, cols] = ((u_ref[:, cols] * mixed) * _silu(z_ref[:, cols])).astype(o_ref.dtype)


def _gmlp_prompt(proj, ln_sgu, ws_b, bs_b, li):
    M = proj.shape[0]
    wb = B_WIDTH
    blk = lambda off: pl.BlockSpec((B_CHUNK, wb), lambda i: (i, off // wb))
    bst = jnp.swapaxes(bs_b, 1, 2)
    return pl.pallas_call(
        _gmlp_kernel, grid=(M // B_CHUNK,),
        in_specs=[blk(AB_UB), blk(AB_VB), blk(AB_ZB),
                  pl.BlockSpec((None, 1, wb), lambda i: (li, 0, 0)),
                  pl.BlockSpec((None, B_GROUPS, B_CHUNK, B_CHUNK), lambda i: (li, 0, 0, 0)),
                  pl.BlockSpec((None, B_CHUNK, B_GROUPS), lambda i: (li, 0, 0))],
        out_specs=[pl.BlockSpec((B_CHUNK, wb), lambda i: (i, 0))] * 2,
        out_shape=[jax.ShapeDtypeStruct((M, wb), F32), jax.ShapeDtypeStruct((M, wb), BF)],
        compiler_params=_params("arbitrary"), name="gmlp_prompt",
    )(proj, proj, proj, ln_sgu.reshape(-1, 1, wb), ws_b, bst)


def _compress_rows(x_ref, w_ref, wbase, h, nseg):
    first = jnp.zeros((nseg, HEAD_DIM), F32)
    second = jnp.zeros((nseg, HEAD_DIM), F32)
    for r in range(CMP_STRIDE):
        rows = x_ref[pl.ds(r, nseg, stride=CMP_STRIDE), :]
        first = first + rows * w_ref[wbase + r * C_KV_HEADS + h]
        second = second + rows * w_ref[wbase + (CMP_STRIDE + r) * C_KV_HEADS + h]
    c = first + pltpu.roll(second, nseg - 1, 0)
    seg = lax.broadcasted_iota(jnp.int32, (nseg, HEAD_DIM), 0)
    return jnp.where(seg < nseg - 1, c, 0.0)


def _compress_kernel(cwk_ref, cwv_ref, kc_ref, vc_ref, g_ref, ko_ref, vo_ref, *, li, nseg):
    h = pl.program_id(1)
    wbase = li * CMP_LEN * C_KV_HEADS
    kc = _compress_rows(kc_ref, cwk_ref, wbase, h, nseg)
    ko_ref[...] = kc * lax.rsqrt(jnp.mean(kc * kc, axis=-1, keepdims=True) + NORM_EPS) * g_ref[...]
    vo_ref[...] = _compress_rows(vc_ref, cwv_ref, wbase, h, nseg)


def _compress_prompt(proj, cw_k, cw_v, gain, li, B, S):
    nseg = S // CMP_STRIDE
    hb = lambda off: off // HEAD_DIM
    smem = pl.BlockSpec(memory_space=pltpu.SMEM)
    ospec = pl.BlockSpec((None, None, nseg, HEAD_DIM), lambda b, h: (b, h, 0, 0))
    oshape = jax.ShapeDtypeStruct((B, C_KV_HEADS, nseg, HEAD_DIM), F32)
    return pl.pallas_call(
        functools.partial(_compress_kernel, li=li, nseg=nseg), grid=(B, C_KV_HEADS),
        in_specs=[smem, smem,
                  pl.BlockSpec((S, HEAD_DIM), lambda b, h: (b, hb(C_KC) + h)),
                  pl.BlockSpec((S, HEAD_DIM), lambda b, h: (b, hb(C_VC) + h)),
                  pl.BlockSpec((1, HEAD_DIM), lambda b, h: (0, 0))],
        out_specs=[ospec, ospec], out_shape=[oshape, oshape],
        compiler_params=_params("arbitrary", "arbitrary"), name="compress_prompt",
    )(cw_k.reshape(-1), cw_v.reshape(-1), proj, proj, gain)


def _cmp_to_sel(nrows, ncols, rows_per_seg):
    n = lax.broadcasted_iota(jnp.int32, (nrows, ncols), 0) // rows_per_seg
    j = lax.broadcasted_iota(jnp.int32, (nrows, ncols), 1)
    shared = (jnp.minimum(n * CMP_STRIDE + CMP_LEN, (j + 1) * SEL_BLOCK)
              - jnp.maximum(n * CMP_STRIDE, j * SEL_BLOCK))
    return jnp.maximum(shared, 0).astype(F32) / CMP_LEN


def _cmp_to_sel_t(nrows, ncols, seg_mask):
    j = lax.broadcasted_iota(jnp.int32, (nrows, ncols), 0)
    n = lax.broadcasted_iota(jnp.int32, (nrows, ncols), 1) & seg_mask
    shared = (jnp.minimum(n * CMP_STRIDE + CMP_LEN, (j + 1) * SEL_BLOCK)
              - jnp.maximum(n * CMP_STRIDE, j * SEL_BLOCK))
    return jnp.maximum(shared, 0).astype(F32) / CMP_LEN


def _rank_select_t(score, rows):
    sc = score[:rows]
    blk = lax.broadcasted_iota(jnp.int32, sc.shape, 0)
    rank = jnp.zeros(sc.shape, F32)
    for jp in range(rows):
        other = sc[jp:jp + 1, :]
        ahead = (other > sc) | ((other == sc) & (blk > jp))
        rank = rank + jnp.where(ahead, 1.0, 0.0)
    picked = jnp.where((rank < SEL_TOP) & (sc > NEG_INF), 1.0, 0.0)
    return jnp.concatenate([picked, jnp.zeros((score.shape[0] - rows, score.shape[1]), F32)], axis=0)


def _flash_loop_t(q4, k_ref, vt_ref, n_tiles, tk, bias_fn):
    cols = q4.shape[0]
    reps = cols // LANES

    def body(t, carry):
        m, l, acc = carry
        k0 = pl.multiple_of(t * tk, tk)
        s = _dot_nt(k_ref[pl.ds(k0, tk), :], q4) * ATTN_SCALE
        s = s + jnp.concatenate([bias_fn(k0)] * reps, axis=1)
        m_new = jnp.maximum(m, jnp.max(s, axis=0, keepdims=True))
        m_safe = jnp.where(m_new == NEG_INF, 0.0, m_new)
        alpha = jnp.exp(m - m_safe)
        e = jnp.exp(s - m_safe)
        l = alpha * l + jnp.sum(e, axis=0, keepdims=True)
        acc = alpha * acc + _dot(vt_ref[t], e.astype(BF))
        return m_new, l, acc

    init = (jnp.full((1, cols), NEG_INF, F32), jnp.zeros((1, cols), F32), jnp.zeros((HEAD_DIM, cols), F32))
    _, l, acc = lax.fori_loop(0, n_tiles, body, init)
    return acc / l


def _stack_heads(ref):
    return jnp.concatenate([ref[:, g * HEAD_DIM:(g + 1) * HEAD_DIM] for g in range(C_GQA)], axis=0)


def _nsa_kernel(qn_ref, qr_ref, kcmp_ref, vcmp_ref, ks_ref, vs_ref, kw_ref, vw_ref, g_ref, z_ref, o_ref,
                ksb_ref, kwb_ref, vst_ref, vwt_ref, vct_ref, exp_ref, *, S, n_sel, sel_tk):
    qb = Q_BLOCK
    qi = pl.program_id(2)
    q0 = qi * qb
    nseg = kcmp_ref.shape[0]
    cols4 = C_GQA * qb
    win_keys = C_WIN + qb

    @pl.when(qi == 0)
    def _():
        ksb_ref[...] = ks_ref[...].astype(BF)
        kwb_ref[...] = kw_ref[...].astype(BF)
        for t in range(S // sel_tk):
            vst_ref[t] = vs_ref[t * sel_tk:(t + 1) * sel_tk, :].T.astype(BF)
        for t in range(S // qb):
            vwt_ref[t] = vw_ref[t * qb:(t + 1) * qb, :].T.astype(BF)
        vct_ref[...] = vcmp_ref[...].T.astype(BF)
        key_blk = lax.broadcasted_iota(jnp.int32, (S, LANES), 0) // SEL_BLOCK
        exp_ref[...] = jnp.where(key_blk == lax.broadcasted_iota(jnp.int32, (S, LANES), 1), 1.0, 0.0).astype(BF)

    p_q = q0 + lax.broadcasted_iota(jnp.int32, (1, qb), 1)
    p_4 = q0 + (lax.broadcasted_iota(jnp.int32, (1, cols4), 1) & (qb - 1))

    qn4 = _stack_heads(qn_ref).astype(BF)
    seg = lax.broadcasted_iota(jnp.int32, (nseg, cols4), 0)
    vis = seg * CMP_STRIDE + (CMP_LEN - 1) <= p_4
    s = jnp.where(vis, _dot_nt(kcmp_ref[...].astype(BF), qn4) * ATTN_SCALE, NEG_INF)
    m = jnp.max(s, axis=0, keepdims=True)
    m = jnp.where(m == NEG_INF, 0.0, m)
    e = jnp.exp(s - m)
    prob_b = (e / jnp.maximum(jnp.sum(e, axis=0, keepdims=True), TINY)).astype(BF)
    o_cmp = _dot(vct_ref[...], prob_b)

    prob_stack = jnp.concatenate([prob_b[:, g * qb:(g + 1) * qb] for g in range(C_GQA)], axis=0)
    imp = _dot(_cmp_to_sel_t(LANES, C_GQA * nseg, nseg - 1).astype(BF), prob_stack)
    blk = lax.broadcasted_iota(jnp.int32, (LANES, qb), 0)
    cur = p_q // SEL_BLOCK
    forced = (blk == 0) | (blk == cur) | (blk == cur - 1)
    score = jnp.where(forced, imp + FORCE_BONUS, imp)
    score = jnp.where(blk <= cur, score, NEG_INF)
    sel = _rank_select_t(score, -(-n_sel // SUBLANES) * SUBLANES).astype(BF)

    qr4 = _stack_heads(qr_ref).astype(BF)

    def sel_bias(k0):
        chosen = _dot(exp_ref[pl.ds(k0, sel_tk), :], sel)
        kpos = k0 + lax.broadcasted_iota(jnp.int32, (sel_tk, 1), 0)
        return jnp.where((chosen > 0.5) & (kpos <= p_q), 0.0, NEG_INF)

    o_sel = _flash_loop_t(qr4, ksb_ref, vst_ref, (q0 + qb + sel_tk - 1) // sel_tk, sel_tk, sel_bias)

    w0 = pl.multiple_of(jnp.maximum(q0 - C_WIN, 0), qb)
    s = _dot_nt(kwb_ref[pl.ds(w0, win_keys), :], qr4) * ATTN_SCALE
    kpos = w0 + lax.broadcasted_iota(jnp.int32, (win_keys, 1), 0)
    bias = jnp.where((kpos <= p_q) & (kpos >= p_q - C_WIN), 0.0, NEG_INF)
    s = s + jnp.concatenate([bias] * C_GQA, axis=1)
    e = jnp.exp(s - jnp.max(s, axis=0, keepdims=True))
    den = jnp.sum(e, axis=0, keepdims=True)
    e = e.astype(BF)
    t0 = w0 // qb
    acc = _dot(vwt_ref[t0], e[:qb])
    for i in range(1, win_keys // qb):
        acc = acc + _dot(vwt_ref[t0 + i], e[i * qb:(i + 1) * qb])
    o_win = acc / den

    gates_t = _sigmoid(g_ref[...].T)
    for g in range(C_GQA):
        cs = slice(g * HEAD_DIM, (g + 1) * HEAD_DIM)
        o = (gates_t[g:g + 1] * o_cmp[:, cs] + gates_t[C_GQA + g:C_GQA + g + 1] * o_sel[:, cs]
             + gates_t[2 * C_GQA + g:2 * C_GQA + g + 1] * o_win[:, cs])
        o_ref[:, cs] = (o.T * _silu(z_ref[:, cs])).astype(o_ref.dtype)


def _nsa_prompt(qn, qr, kcmp, vcmp, ks_r, kw_r, proj, gates, z, B, S):
    nq = S // Q_BLOCK
    nseg = kcmp.shape[2]
    gw = C_GQA * HEAD_DIM
    sel_tk = 512
    assert S >= C_WIN + Q_BLOCK and S % sel_tk == 0
    hb = lambda off: off // HEAD_DIM
    qspec = pl.BlockSpec((Q_BLOCK, gw), lambda b, h, i: (b * nq + i, h))
    cspec = pl.BlockSpec((None, None, nseg, HEAD_DIM), lambda b, h, i: (b, h, 0, 0))

    def kv_spec(base):
        return pl.BlockSpec((S, HEAD_DIM), lambda b, h, i: (b, base + h))

    return pl.pallas_call(
        functools.partial(_nsa_kernel, S=S, n_sel=-(-S // SEL_BLOCK), sel_tk=sel_tk),
        grid=(B, C_KV_HEADS, nq),
        in_specs=[qspec, qspec, cspec, cspec, kv_spec(0), kv_spec(hb(C_VS)), kv_spec(0), kv_spec(hb(C_VW)),
                  pl.BlockSpec((Q_BLOCK, LANES), lambda b, h, i: (b * nq + i, h)), qspec],
        out_specs=qspec, out_shape=jax.ShapeDtypeStruct((B * S, C_WIDTH), BF),
        scratch_shapes=[pltpu.VMEM((S, HEAD_DIM), BF), pltpu.VMEM((S, HEAD_DIM), BF),
                        pltpu.VMEM((S // sel_tk, HEAD_DIM, sel_tk), BF),
                        pltpu.VMEM((S // Q_BLOCK, HEAD_DIM, Q_BLOCK), BF),
                        pltpu.VMEM((HEAD_DIM, nseg), BF), pltpu.VMEM((S, LANES), BF)],
        compiler_params=_params("arbitrary", "arbitrary", "arbitrary"), name="nsa_prompt",
    )(qn, qr, kcmp, vcmp, ks_r, proj, kw_r, proj, gates, z)


def _gate_weights(w_in_c, li):
    wg = w_in_c[li, :, C_G:C_Z].reshape(D_MODEL, 3, C_KV_HEADS, C_GQA)
    wg = jnp.transpose(wg, (0, 2, 1, 3)).reshape(D_MODEL, C_KV_HEADS, 3 * C_GQA)
    wg = jnp.pad(wg, ((0, 0), (0, 0), (0, LANES - 3 * C_GQA)))
    return wg.reshape(D_MODEL, C_KV_HEADS * LANES)


def _emit_kernel(*refs, heads, n_src, skip, tm):
    srcs = refs[skip:skip + n_src]
    o_ref = refs[skip + n_src]
    for a in range(n_src):
        for h in range(heads):
            o_ref[a, pl.ds(h, tm, stride=heads), :] = srcs[a][:, h * HEAD_DIM:(h + 1) * HEAD_DIM]


def _emit_rows(srcs, heads, B, S, keep, li, n_layers, prev):
    width = heads * HEAD_DIM
    tm = EMIT_TILE
    spb, row0 = S // tm, (S - keep) // tm
    n_src = len(srcs)

    def src_spec(off):
        return pl.BlockSpec((tm, width), lambda b, i: (b * spb + row0 + i, off // width))

    in_specs = [src_spec(off) for _, off in srcs]
    args = [a for a, _ in srcs]
    aliases = {}
    if prev is not None:
        in_specs = [pl.BlockSpec(memory_space=pl.ANY)] + in_specs
        args = [prev] + args
        aliases = {0: 0}
    return pl.pallas_call(
        functools.partial(_emit_kernel, heads=heads, n_src=n_src, skip=len(aliases), tm=tm),
        grid=(B, keep // tm), in_specs=in_specs,
        out_specs=pl.BlockSpec((None, None, n_src, tm * heads, HEAD_DIM), lambda b, i: (b, li, 0, i, 0)),
        out_shape=jax.ShapeDtypeStruct((B, n_layers, n_src, keep * heads, HEAD_DIM), F32),
        input_output_aliases=aliases,
        compiler_params=_params("arbitrary", "arbitrary"), name="emit_rows",
    )(*args)


def _ab_layer_prompt(h, li, layer, p, prm, tables, B, S, outs, n_layers):
    xn = _rms_rows(h, prm["ln_ab"], li, BF)
    proj = _matmul(xn, prm["w_in_ab"], li, 0, prm["w_in_ab"].shape[2])
    q_r = _head_norm(proj, AB_Q, A_GROUPS * A_WIDTH, prm["qn_a"][li][None], tables)
    k_r = _head_norm(proj, AB_K, A_WIDTH, prm["kn_a"][li][None], tables)
    ga = _dilated_prompt(q_r, k_r, proj, B, S)
    vn, gb = _gmlp_prompt(proj, prm["ln_sgu"], prm["ws_b"], prm["bs_b"], li)
    h1, h1b = _proj_residual([ga, gb], prm["w_out_ab"], li, h, True)
    h2 = _ple(h1, h1b, p, prm["w_ple_gate"], prm["w_ple"], layer)
    outs["a"] = _emit_rows([(k_r, 0), (proj, AB_V)], A_HEADS, B, S, S, li, n_layers, outs.get("a"))
    chunk_start = ((S - 1) // B_CHUNK) * B_CHUNK
    return h2, vn.reshape(B, S, B_WIDTH)[:, chunk_start:]


def _c_layer_prompt(h, li, layer, p, prm, tables, B, S, outs, n_layers):
    w = prm["w_in_c"]
    xn = _rms_rows(h, prm["ln_c"], li, BF)
    proj = _matmul(xn, w, li, 0, C_QKV_COLS)
    z = _matmul(xn, w[li, :, C_Z:], None, 0, C_WIDTH)
    gates = _matmul(xn, _gate_weights(w, li), None, 0, C_KV_HEADS * LANES)
    qn, qr = _head_norm(proj, C_Q, C_WIDTH, prm["qn_c"][li][None], tables, plain=True)
    ks_r = _head_norm(proj, C_KS, C_KV_WIDTH, prm["kn_c"][li, 1][None], tables)
    kw_r = _head_norm(proj, C_KW, C_KV_WIDTH, prm["kn_c"][li, 2][None], tables)
    kcmp, vcmp = _compress_prompt(proj, prm["cw_k"], prm["cw_v"], prm["kn_c"][li, 0][None], li, B, S)
    lhs = _nsa_prompt(qn, qr, kcmp, vcmp, ks_r, kw_r, proj, gates, z, B, S)
    h1, h1b = _proj_residual([lhs], prm["w_out_c"], li, h, True)
    h2 = _ple(h1, h1b, p, prm["w_ple_gate"], prm["w_ple"], layer)
    keep = min(C_WIN, S)
    emit = lambda key, srcs, rows: _emit_rows(srcs, C_KV_HEADS, B, S, rows, li, n_layers, outs.get(key))
    outs["cmp"] = emit("cmp", [(proj, C_KC), (proj, C_VC)], S)
    outs["sel"] = emit("sel", [(ks_r, 0), (proj, C_VS)], S)
    outs["win"] = emit("win", [(kw_r, 0), (proj, C_VW)], keep)
    return h2


def _prompt_trunk(x, p, prm):
    B, S, _ = x.shape
    tables = _rope_tables(S, 0, 1)
    h = x.reshape(B * S, D_MODEL)
    depth = p.shape[0]
    n_ab, n_c = (depth + 1) // 2, depth // 2
    p = p.reshape(depth, B * S, PLE_DIM)
    outs, b_rows = {}, []
    for i in range(depth):
        li = i // 2
        if i % 2 == 0:
            h, vb = _ab_layer_prompt(h, li, i, p, prm, tables, B, S, outs, n_ab)
            b_rows.append(vb)
        else:
            h = _c_layer_prompt(h, li, i, p, prm, tables, B, S, outs, n_c)

    def rows(key, heads):
        a = outs[key]
        return a.reshape(a.shape[:3] + (a.shape[3] // heads, heads, HEAD_DIM))

    return (h.reshape(B, S, D_MODEL), rows("a", A_HEADS), jnp.stack(b_rows, axis=1),
            rows("cmp", C_KV_HEADS), rows("sel", C_KV_HEADS), rows("win", C_KV_HEADS))


def _pad_rows(x, rows):
    return jnp.concatenate([x, jnp.zeros((rows - x.shape[0], x.shape[1]), x.dtype)], axis=0)


def _bf_round(x):
    return x.astype(BF).astype(F32)


def _dilated_sample_kernel(q_ref, kn_ref, pj_ref, kp_ref, vp_ref, o_ref, *, P):
    h = pl.program_id(1)
    qs = [q_ref[pl.ds(g * A_HEADS + h, 1), :] for g in range(A_GROUPS)]
    qmat = _pad_rows(jnp.concatenate(qs, axis=0), SUBLANES)
    k_new = kn_ref[pl.ds(h, 1), :]
    v_new = pj_ref[pl.ds(AB_V // HEAD_DIM + h, 1), :]
    z = pj_ref[pl.ds(AB_ZA // HEAD_DIM + h, 1), :]
    grp = lax.broadcasted_iota(jnp.int32, (SUBLANES, P), 0)
    delta = P - lax.broadcasted_iota(jnp.int32, (SUBLANES, P), 1)
    dil = jnp.where(grp == 0, A_PATTERNS[0][1], jnp.where(grp == 1, A_PATTERNS[1][1], A_PATTERNS[2][1]))
    win = jnp.where(grp == 0, A_PATTERNS[0][0], jnp.where(grp == 1, A_PATTERNS[1][0], A_PATTERNS[2][0]))
    valid = ((delta & (dil - 1)) == 0) & (delta <= win) & (grp < A_GROUPS)
    head_rows = pl.ds(h, P, stride=A_HEADS)
    s = jnp.where(valid, _dot_nt(qmat.astype(BF), kp_ref[head_rows, :].astype(BF)) * ATTN_SCALE, NEG_INF)
    s_new = jnp.sum(_bf_round(qmat) * _bf_round(k_new), axis=-1, keepdims=True) * ATTN_SCALE
    m = jnp.maximum(jnp.max(s, axis=-1, keepdims=True), s_new)
    e = jnp.exp(s - m)
    e_new = jnp.exp(s_new - m)
    den = jnp.sum(e, axis=-1, keepdims=True) + e_new
    num = _dot(e.astype(BF), vp_ref[head_rows, :].astype(BF)) + _bf_round(e_new) * _bf_round(v_new)
    live = lax.broadcasted_iota(jnp.int32, (SUBLANES, 1), 0) < A_GROUPS
    m_all = jnp.max(jnp.where(live, m, NEG_INF), axis=0, keepdims=True)
    w = jnp.where(live, jnp.exp(m - m_all), 0.0)
    num_t = jnp.sum(w * num, axis=0, keepdims=True)
    den_t = jnp.sum(w * den, axis=0, keepdims=True)
    o_ref[pl.ds(h, 1), :] = (num_t / den_t) * _silu(z)


def _dilated_sample(q_r, k_r, proj, cache, li):
    DB, P = cache.shape[0], cache.shape[3]
    cache = cache.reshape(cache.shape[:3] + (P * A_HEADS, HEAD_DIM))
    row3 = lambda x: x.reshape(DB, x.shape[1] // HEAD_DIM, HEAD_DIM)
    full = lambda x: pl.BlockSpec((None,) + x.shape[1:], lambda b, h: (b, 0, 0))
    q3, k3, p3 = row3(q_r), row3(k_r), row3(proj)
    cspec = lambda kv: pl.BlockSpec((None, None, None, P * A_HEADS, HEAD_DIM), lambda b, h: (b, li, kv, 0, 0))
    out = pl.pallas_call(
        functools.partial(_dilated_sample_kernel, P=P), grid=(DB, A_HEADS),
        in_specs=[full(q3), full(k3), full(p3), cspec(0), cspec(1)],
        out_specs=pl.BlockSpec((None, A_HEADS, HEAD_DIM), lambda b, h: (b, 0, 0)),
        out_shape=jax.ShapeDtypeStruct((DB, A_HEADS, HEAD_DIM), F32),
        compiler_params=_params("arbitrary", "arbitrary"), name="dilated_sample",
    )(q3, k3, p3, cache, cache)
    return out.reshape(DB, A_WIDTH)


def _gmlp_sample_kernel(u_ref, v_ref, z_ref, g_ref, w0_ref, b0_ref, vn_ref, o_ref):
    vn = _layer_norm(v_ref[...], g_ref[...])
    vn_ref[...] = vn
    mixed = _bf_round(w0_ref[...]) * _bf_round(vn) + b0_ref[...]
    o_ref[...] = (u_ref[...] * mixed) * _silu(z_ref[...])


def _gmlp_sample(proj, ln_sgu, ws_b, bs_b, li):
    M, wb = proj.shape[0], B_WIDTH
    blk = lambda off: pl.BlockSpec((M, wb), lambda i: (0, off // wb))
    vec = pl.BlockSpec((1, wb), lambda i: (0, 0))
    w0 = jnp.repeat(ws_b[li, :, 0, 0], wb // B_GROUPS)[None]
    b0 = jnp.repeat(bs_b[li, :, 0], wb // B_GROUPS)[None]
    shp = jax.ShapeDtypeStruct((M, wb), F32)
    return pl.pallas_call(
        _gmlp_sample_kernel, in_specs=[blk(AB_UB), blk(AB_VB), blk(AB_ZB), vec, vec, vec],
        out_specs=[pl.BlockSpec((M, wb), lambda i: (0, 0))] * 2, out_shape=[shp, shp], grid=(1,),
        compiler_params=_params("arbitrary"), name="gmlp_sample",
    )(proj, proj, proj, ln_sgu[li][None], w0, b0)


def _page_compress_kernel(pt_ref, w_ref, *refs, pages, segs):
    o_ref = refs[pages]
    tiles = CMP_STRIDE * C_KV_HEADS // SUBLANES
    low = lax.broadcasted_iota(jnp.int32, (SUBLANES, HEAD_DIM), 0) < C_KV_HEADS
    for j in range(pages):
        page = refs[j]
        for kv in range(2):
            folded = []
            for n in range(segs):
                base = n * tiles * SUBLANES
                first = second = None
                for i in range(tiles):
                    rows = page[kv, base + i * SUBLANES:base + (i + 1) * SUBLANES, :]
                    t1 = rows * w_ref[2 * kv, i * SUBLANES:(i + 1) * SUBLANES, :]
                    t2 = rows * w_ref[2 * kv + 1, i * SUBLANES:(i + 1) * SUBLANES, :]
                    first = t1 if first is None else first + t1
                    second = t2 if second is None else second + t2
                folded.append((first + pltpu.roll(first, C_KV_HEADS, 0), second + pltpu.roll(second, C_KV_HEADS, 0)))
            for i in range(segs // 2):
                out_rows = slice((j * segs // 2 + i) * SUBLANES, (j * segs // 2 + i + 1) * SUBLANES)
                o_ref[2 * kv, out_rows, :] = jnp.where(low, folded[2 * i][0], folded[2 * i + 1][0])
                o_ref[2 * kv + 1, out_rows, :] = jnp.where(low, folded[2 * i][1], folded[2 * i + 1][1])


def _page_compress(pool, page_table, cw_k, cw_v, li, pages=4):
    DB, n_pages = page_table.shape
    page = pool.shape[3]
    assert 2 * C_KV_HEADS == SUBLANES and page % (2 * CMP_STRIDE) == 0
    pool = pool.reshape(pool.shape[:3] + (page * C_KV_HEADS, HEAD_DIM))
    segs = page // CMP_STRIDE
    rows_out = segs * C_KV_HEADS
    lanes = lambda w: jnp.broadcast_to(w.reshape(CMP_STRIDE * C_KV_HEADS, 1), (CMP_STRIDE * C_KV_HEADS, HEAD_DIM))
    w = jnp.stack([lanes(cw_k[li, :CMP_STRIDE]), lanes(cw_k[li, CMP_STRIDE:]),
                   lanes(cw_v[li, :CMP_STRIDE]), lanes(cw_v[li, CMP_STRIDE:])])

    def pspec(j):
        return pl.BlockSpec((None, None, 2, page * C_KV_HEADS, HEAD_DIM),
                            lambda b, i, pt: (pt[b * n_pages + i * pages + j], li, 0, 0, 0))

    return pl.pallas_call(
        functools.partial(_page_compress_kernel, pages=pages, segs=segs),
        grid_spec=pltpu.PrefetchScalarGridSpec(
            num_scalar_prefetch=1, grid=(DB, n_pages // pages),
            in_specs=[pl.BlockSpec(w.shape, lambda b, i, pt: (0, 0, 0))] + [pspec(j) for j in range(pages)],
            out_specs=pl.BlockSpec((None, 4, rows_out * pages, HEAD_DIM), lambda b, i, pt: (b, 0, i, 0))),
        out_shape=jax.ShapeDtypeStruct((DB, 4, n_pages * rows_out, HEAD_DIM), F32),
        compiler_params=_params("arbitrary", "arbitrary"), name="page_compress",
    )(page_table.reshape(-1), w, *([pool] * pages))


def _cmp_sample_kernel(fs_ref, q_ref, g_ref, o_ref, ix_ref, *, pos, n_sel, nj):
    rows = fs_ref.shape[1]
    row_i = lax.broadcasted_iota(jnp.int32, (rows, HEAD_DIM), 0)
    complete = row_i < rows - C_KV_HEADS
    kc = jnp.where(complete, fs_ref[0] + pltpu.roll(fs_ref[1], rows - C_KV_HEADS, 0), 0.0)
    vc = jnp.where(complete, fs_ref[2] + pltpu.roll(fs_ref[3], rows - C_KV_HEADS, 0), 0.0)
    kc = kc * lax.rsqrt(jnp.mean(kc * kc, axis=-1, keepdims=True) + NORM_EPS) * g_ref[...]
    col = lax.broadcasted_iota(jnp.int32, (C_HEADS, rows), 1)
    q_head = lax.broadcasted_iota(jnp.int32, (C_HEADS, rows), 0)
    vis = (((col // C_KV_HEADS) * CMP_STRIDE + (CMP_LEN - 1) <= pos)
           & ((col & (C_KV_HEADS - 1)) == q_head // C_GQA))
    s = jnp.where(vis, _dot_nt(q_ref[...].astype(BF), kc.astype(BF)) * ATTN_SCALE, NEG_INF)
    m = jnp.max(s, axis=-1, keepdims=True)
    m = jnp.where(m == NEG_INF, 0.0, m)
    e = jnp.exp(s - m)
    prob_b = (e / jnp.maximum(jnp.sum(e, axis=-1, keepdims=True), TINY)).astype(BF)
    o_ref[...] = _dot(prob_b, vc.astype(BF))

    imp_heads = _dot(prob_b, _cmp_to_sel(rows, nj, C_KV_HEADS).astype(BF))
    lane = lax.broadcasted_iota(jnp.int32, (1, nj), 1)
    cur = pos // SEL_BLOCK
    forced = (lane == 0) | (lane == cur) | (lane == cur - 1)
    ii = lax.broadcasted_iota(jnp.int32, (nj, nj), 0)
    jj = lax.broadcasted_iota(jnp.int32, (nj, nj), 1)
    slot = lax.broadcasted_iota(jnp.int32, (nj, LANES), 1).astype(F32)
    blk_id = lax.broadcasted_iota(jnp.int32, (nj, LANES), 0)
    for h in range(C_KV_HEADS):
        imp = jnp.sum(imp_heads[h * C_GQA:(h + 1) * C_GQA], axis=0, keepdims=True)
        score = jnp.where(forced, imp + FORCE_BONUS, imp)
        score = jnp.where((lane <= cur) & (lane < n_sel), score, NEG_INF)
        s_row = jnp.broadcast_to(score, (nj, nj))
        s_col = jnp.sum(jnp.where(ii == jj, s_row, 0.0), axis=-1, keepdims=True)
        ahead = (s_row > s_col) | ((s_row == s_col) & (jj < ii))
        rank = jnp.sum(jnp.where(ahead, 1.0, 0.0), axis=-1, keepdims=True)
        hit = rank == slot
        idx = jnp.sum(jnp.where(hit, blk_id, 0), axis=0, keepdims=True)
        ok = jnp.sum(jnp.where(hit & (s_col > NEG_INF), 1, 0), axis=0, keepdims=True)
        ix_ref[h] = jnp.concatenate([idx, ok, jnp.zeros((SUBLANES - 2, LANES), jnp.int32)], axis=0)


def _cmp_sample(fs, qn, gain, pos):
    DB, _, rows, _ = fs.shape
    total = pos + 1
    n_sel = -(-total // SEL_BLOCK)
    nj = -(-n_sel // LANES) * LANES
    q3 = qn.reshape(DB, C_HEADS, HEAD_DIM)
    return pl.pallas_call(
        functools.partial(_cmp_sample_kernel, pos=pos, n_sel=n_sel, nj=nj), grid=(DB,),
        in_specs=[pl.BlockSpec((None, 4, rows, HEAD_DIM), lambda b: (b, 0, 0, 0)),
                  pl.BlockSpec((None, C_HEADS, HEAD_DIM), lambda b: (b, 0, 0)),
                  pl.BlockSpec((1, HEAD_DIM), lambda b: (0, 0))],
        out_specs=[pl.BlockSpec((None, C_HEADS, HEAD_DIM), lambda b: (b, 0, 0)),
                   pl.BlockSpec((None, C_KV_HEADS, SUBLANES, LANES), lambda b: (b, 0, 0, 0))],
        out_shape=[jax.ShapeDtypeStruct((DB, C_HEADS, HEAD_DIM), F32),
                   jax.ShapeDtypeStruct((DB, C_KV_HEADS, SUBLANES, LANES), jnp.int32)],
        compiler_params=_params("arbitrary"), name="cmp_sample",
    )(fs, q3, gain)


def _sel_sample_kernel(pt_ref, ix_ref, ok_ref, q_ref, kn_ref, pj_ref, *refs, nb, pos, n_past):
    k_refs, v_refs = refs[:nb], refs[nb:2 * nb]
    o_ref, m_ref, l_ref, acc_ref = refs[2 * nb:]
    b, h, i = pl.program_id(0), pl.program_id(1), pl.program_id(2)

    @pl.when(i == 0)
    def _():
        m_ref[...] = jnp.full(m_ref.shape, NEG_INF, F32)
        l_ref[...] = jnp.zeros(l_ref.shape, F32)
        acc_ref[...] = jnp.zeros(acc_ref.shape, F32)

    q8 = _pad_rows(q_ref[pl.ds(h * C_GQA, C_GQA), :], SUBLANES).astype(BF)
    k_new = kn_ref[pl.ds(h, 1), :]
    v_new = pj_ref[pl.ds(C_VS // HEAD_DIM + h, 1), :]
    first_row = lax.broadcasted_iota(jnp.int32, (SEL_BLOCK, HEAD_DIM), 0) == 0
    r = lax.broadcasted_iota(jnp.int32, (1, SEL_BLOCK), 1)
    head_rows = pl.ds(h, SEL_BLOCK, stride=C_KV_HEADS)
    for j in range(nb):
        slot = (b * C_KV_HEADS + h) * SEL_TOP + i * nb + j
        blk = ix_ref[slot]
        is_past = blk < n_past
        kt = jnp.where(is_past, k_refs[j][head_rows, :], jnp.where(first_row, k_new, 0.0))
        vt = jnp.where(is_past, v_refs[j][head_rows, :], jnp.where(first_row, v_new, 0.0))
        ok = (blk * SEL_BLOCK + r <= pos) & (ok_ref[slot] > 0)
        s = jnp.where(ok, _dot_nt(q8, kt.astype(BF)) * ATTN_SCALE, NEG_INF)
        m_old = m_ref[...]
        m_new = jnp.maximum(m_old, jnp.max(s, axis=-1, keepdims=True))
        m_safe = jnp.where(m_new == NEG_INF, 0.0, m_new)
        alpha = jnp.exp(m_old - m_safe)
        e = jnp.exp(s - m_safe)
        l_ref[...] = alpha * l_ref[...] + jnp.sum(e, axis=-1, keepdims=True)
        acc_ref[...] = alpha * acc_ref[...] + _dot(e.astype(BF), vt.astype(BF))
        m_ref[...] = m_new

    @pl.when(i == pl.num_programs(2) - 1)
    def _():
        o_ref[pl.ds(h * C_GQA, C_GQA), :] = (acc_ref[...] / l_ref[...])[:C_GQA]


def _sel_sample(pool, page_table, idx, ok, qr, ks_r, proj, li, pos, nb=SEL_TOP):
    DB, n_pages = page_table.shape
    page = pool.shape[3]
    bpp = page // SEL_BLOCK
    n_past = n_pages * bpp
    pool = pool.reshape(pool.shape[:3] + (page * C_KV_HEADS, HEAD_DIM))
    row3 = lambda x: x.reshape(DB, x.shape[1] // HEAD_DIM, HEAD_DIM)
    q3, k3, p3 = row3(qr), row3(ks_r), row3(proj)
    full = lambda x: pl.BlockSpec((None,) + x.shape[1:], lambda b, h, i, pt, ix, okf: (b, 0, 0))

    def bspec(j, kv):
        def imap(b, h, i, pt, ix, okf):
            blk = jnp.clip(ix[(b * C_KV_HEADS + h) * SEL_TOP + i * nb + j], 0, n_past - 1)
            return (pt[b * n_pages + blk // bpp], li, kv, blk % bpp, 0)
        return pl.BlockSpec((None, None, None, SEL_BLOCK * C_KV_HEADS, HEAD_DIM), imap)

    out = pl.pallas_call(
        functools.partial(_sel_sample_kernel, nb=nb, pos=pos, n_past=n_past),
        grid_spec=pltpu.PrefetchScalarGridSpec(
            num_scalar_prefetch=3, grid=(DB, C_KV_HEADS, SEL_TOP // nb),
            in_specs=[full(q3), full(k3), full(p3)] + [bspec(j, 0) for j in range(nb)]
            + [bspec(j, 1) for j in range(nb)],
            out_specs=pl.BlockSpec((None, C_HEADS, HEAD_DIM), lambda b, h, i, pt, ix, okf: (b, 0, 0)),
            scratch_shapes=[pltpu.VMEM((SUBLANES, 1), F32), pltpu.VMEM((SUBLANES, 1), F32),
                            pltpu.VMEM((SUBLANES, HEAD_DIM), F32)]),
        out_shape=jax.ShapeDtypeStruct((DB, C_HEADS, HEAD_DIM), F32),
        compiler_params=_params("arbitrary", "arbitrary", "arbitrary"), name="sel_sample",
    )(page_table.reshape(-1), idx, ok, q3, k3, p3, *([pool] * (2 * nb)))
    return out


def _win_sample_kernel(q_ref, kn_ref, pj_ref, kp_ref, vp_ref, oc_ref, os_ref, g_ref, z_ref, o_ref):
    h = pl.program_id(1)
    heads = pl.ds(h * C_GQA, C_GQA)
    q8 = _pad_rows(q_ref[heads, :], SUBLANES)
    k_new = kn_ref[pl.ds(h, 1), :]
    v_new = pj_ref[pl.ds(C_VW // HEAD_DIM + h, 1), :]
    head_rows = pl.ds(h, kp_ref.shape[0] // C_KV_HEADS, stride=C_KV_HEADS)
    s = _dot_nt(q8.astype(BF), kp_ref[head_rows, :].astype(BF)) * ATTN_SCALE
    s_new = jnp.sum(_bf_round(q8) * _bf_round(k_new), axis=-1, keepdims=True) * ATTN_SCALE
    m = jnp.maximum(jnp.max(s, axis=-1, keepdims=True), s_new)
    e = jnp.exp(s - m)
    e_new = jnp.exp(s_new - m)
    den = jnp.sum(e, axis=-1, keepdims=True) + e_new
    p_past = (e / den).astype(BF)
    o_win = _dot(p_past, vp_ref[head_rows, :].astype(BF)) + _bf_round(e_new / den) * _bf_round(v_new)
    g = g_ref[pl.ds(h, 1), :]
    for a in range(C_GQA):
        gate = [_sigmoid(g[:, br * C_GQA + a:br * C_GQA + a + 1]) for br in range(3)]
        row = pl.ds(h * C_GQA + a, 1)
        o = gate[0] * oc_ref[row, :] + gate[1] * os_ref[row, :] + gate[2] * o_win[a:a + 1]
        o_ref[row, :] = o * _silu(z_ref[row, :])


def _win_sample(cache, qr, kw_r, proj, o_cmp, o_sel, gates, z, li):
    DB, W = cache.shape[0], cache.shape[3]
    assert W <= C_WIN
    cache = cache.reshape(cache.shape[:3] + (W * C_KV_HEADS, HEAD_DIM))
    row3 = lambda x: x.reshape(DB, x.shape[1] // HEAD_DIM, HEAD_DIM)
    full = lambda x: pl.BlockSpec((None,) + x.shape[1:], lambda b, h: (b, 0, 0))
    ins = [row3(qr), row3(kw_r), row3(proj)]
    tail = [o_cmp, o_sel, row3(gates), row3(z)]
    cspec = lambda kv: pl.BlockSpec((None, None, None, W * C_KV_HEADS, HEAD_DIM), lambda b, h: (b, li, kv, 0, 0))
    out = pl.pallas_call(
        _win_sample_kernel, grid=(DB, C_KV_HEADS),
        in_specs=[full(x) for x in ins] + [cspec(0), cspec(1)] + [full(x) for x in tail],
        out_specs=pl.BlockSpec((None, C_HEADS, HEAD_DIM), lambda b, h: (b, 0, 0)),
        out_shape=jax.ShapeDtypeStruct((DB, C_HEADS, HEAD_DIM), F32),
        compiler_params=_params("arbitrary", "arbitrary"), name="win_sample",
    )(*ins, cache, cache, *tail)
    return out.reshape(DB, C_WIDTH)


def _ab_layer_sample(h, li, layer, p, prm, tables, cache_a):
    DB = h.shape[0]
    xn = _rms_rows(h, prm["ln_ab"], li, F32)
    proj = _matmul(xn, prm["w_in_ab"], li, 0, prm["w_in_ab"].shape[2])
    q_r = _head_norm(proj, AB_Q, A_GROUPS * A_WIDTH, prm["qn_a"][li][None], tables)
    k_r = _head_norm(proj, AB_K, A_WIDTH, prm["kn_a"][li][None], tables)
    ga = _dilated_sample(q_r, k_r, proj, cache_a, li)
    vn, gb = _gmlp_sample(proj, prm["ln_sgu"], prm["ws_b"], prm["bs_b"], li)
    h1, _ = _proj_residual([ga, gb], prm["w_out_ab"], li, h, False)
    h2 = _ple(h1, h1, p, prm["w_ple_gate"], prm["w_ple"], layer)
    kv = jnp.stack([k_r.reshape(DB, 1, A_HEADS, HEAD_DIM),
                    proj[:, AB_V:AB_V + A_WIDTH].reshape(DB, 1, A_HEADS, HEAD_DIM)], axis=1)
    return h2, kv, vn.reshape(DB, 1, B_WIDTH)


def _c_layer_sample(h, li, layer, p, prm, tables, past, pos):
    DB = h.shape[0]
    w = prm["w_in_c"]
    xn = _rms_rows(h, prm["ln_c"], li, F32)
    proj = _matmul(xn, w, li, 0, C_QKV_COLS)
    z = _matmul(xn, w[li, :, C_Z:], None, 0, C_WIDTH)
    gates = _matmul(xn, _gate_weights(w, li), None, 0, C_KV_HEADS * LANES)
    qn, qr = _head_norm(proj, C_Q, C_WIDTH, prm["qn_c"][li][None], tables, plain=True)
    ks_r = _head_norm(proj, C_KS, C_KV_WIDTH, prm["kn_c"][li, 1][None], tables)
    kw_r = _head_norm(proj, C_KW, C_KV_WIDTH, prm["kn_c"][li, 2][None], tables)
    fs = _page_compress(past["cmp"], past["page_table"], prm["cw_k"], prm["cw_v"], li)
    o_cmp, picks = _cmp_sample(fs, qn, prm["kn_c"][li, 0][None], pos)
    idx = picks[:, :, 0, :SEL_TOP].reshape(-1)
    ok = picks[:, :, 1, :SEL_TOP].reshape(-1)
    o_sel = _sel_sample(past["sel"], past["page_table"], idx, ok, qr, ks_r, proj, li, pos)
    lhs = _win_sample(past["win"], qr, kw_r, proj, o_cmp, o_sel, gates, z, li)
    h1, _ = _proj_residual([lhs], prm["w_out_c"], li, h, False)
    h2 = _ple(h1, h1, p, prm["w_ple_gate"], prm["w_ple"], layer)
    kvs = (DB, 1, C_KV_HEADS, HEAD_DIM)
    col = lambda off: proj[:, off:off + C_KV_WIDTH].reshape(kvs)
    cmp_rows = jnp.stack([col(C_KC), col(C_VC)], axis=1)
    sel_rows = jnp.stack([ks_r.reshape(kvs), col(C_VS)], axis=1)
    win_rows = jnp.stack([kw_r.reshape(kvs), col(C_VW)], axis=1)
    return h2, cmp_rows, sel_rows, win_rows


def _sample_trunk(x, p, past, prm):
    DB, S, _ = x.shape
    assert S == 1
    pos = past["page_table"].shape[1] * past["cmp"].shape[3]
    assert pos % CMP_STRIDE == 0 and past["cmp"].shape[3] % SEL_BLOCK == 0
    tables = _rope_tables(DB, pos, 0)
    h = x.reshape(DB, D_MODEL)
    a_rows, b_rows, cmp_rows, sel_rows, win_rows = [], [], [], [], []
    pi = p.reshape(p.shape[0], DB, PLE_DIM)
    for i in range(p.shape[0]):
        li = i // 2
        if i % 2 == 0:
            h, kv, vb = _ab_layer_sample(h, li, i, pi, prm, tables, past["a"])
            a_rows.append(kv)
            b_rows.append(vb)
        else:
            h, c, s, w = _c_layer_sample(h, li, i, pi, prm, tables, past, pos)
            cmp_rows.append(c)
            sel_rows.append(s)
            win_rows.append(w)
    st = lambda xs: jnp.stack(xs, axis=1)
    return h.reshape(DB, S, D_MODEL), st(a_rows), st(b_rows), st(cmp_rows), st(sel_rows), st(win_rows)


def kernel(x_prompt, x_sample, cache_a_kv, cache_c_cmp_kv, cache_c_sel_kv, cache_c_win_kv, page_table,
           p_prompt, p_sample, ln_ab, w_in_ab, qn_a, kn_a, ln_sgu, ws_b, bs_b, w_out_ab,
           ln_c, w_in_c, qn_c, kn_c, cw_k, cw_v, w_out_c, w_ple, w_ple_gate):
    prm = dict(ln_ab=ln_ab, w_in_ab=w_in_ab, qn_a=qn_a, kn_a=kn_a, ln_sgu=ln_sgu, ws_b=ws_b, bs_b=bs_b,
               w_out_ab=w_out_ab, ln_c=ln_c, w_in_c=w_in_c, qn_c=qn_c, kn_c=kn_c, cw_k=cw_k, cw_v=cw_v,
               w_out_c=w_out_c, w_ple=w_ple, w_ple_gate=w_ple_gate)
    y_p, a_p, b_p, cmp_p, sel_p, win_p = _prompt_trunk(x_prompt, p_prompt, prm)
    past = dict(a=cache_a_kv, cmp=cache_c_cmp_kv, sel=cache_c_sel_kv, win=cache_c_win_kv, page_table=page_table)
    y_s, a_s, b_s, cmp_s, sel_s, win_s = _sample_trunk(x_sample, p_sample, past, prm)
    return (y_p, y_s, a_p, a_s, b_p, b_s, cmp_p, cmp_s, sel_p, sel_s, win_p, win_s)
```

```python
import functools
import math

import jax
import jax.numpy as jnp
from jax import lax
from jax.experimental import pallas as pl
from jax.experimental.pallas import tpu as pltpu

F32 = jnp.float32
BF = jnp.bfloat16

D_MODEL = 2048
HEAD_DIM = 128
ROPE_DIM = HEAD_DIM // 4
ROPE_HALF = ROPE_DIM // 2
ROPE_THETA = 500000.0
NORM_EPS = 1e-6
Q_BLOCK = 128
PLE_DIM = 256
ATTN_SCALE = HEAD_DIM ** -0.5
SCALE_LOG2E = ATTN_SCALE * math.log2(math.e)
TINY = 1e-30
A_HEADS = D_MODEL // (2 * HEAD_DIM)
A_PATTERNS = ((128, 1), (512, 4), (2048, 16))
A_GROUPS = len(A_PATTERNS)
A_WIDTH = A_HEADS * HEAD_DIM
B_CHUNK = 128
B_WIDTH = D_MODEL - A_WIDTH
B_GROUPS = 8
C_HEADS = D_MODEL // HEAD_DIM
C_KV_HEADS = 4
C_GQA = C_HEADS // C_KV_HEADS
C_WIDTH = C_HEADS * HEAD_DIM
C_KV_WIDTH = C_KV_HEADS * HEAD_DIM
CMP_LEN = 32
CMP_STRIDE = 16
SEL_BLOCK = 64
SEL_TOP = 16
C_WIN = 512
FORCE_BONUS = 1000.0
NEG_INF = float("-inf")

LANES = 128
SUBLANES = 8
MM_TILE_M = 1024
MM_TILE_N = 512
MM_TILE_N_WIDE = 1024
ROW_TILE = 512
EMIT_TILE = 256
SAMPLE_ROWS = 16
VMEM_LIMIT = 56 * 1024 * 1024
DILATED_UNROLL = 4

AB_Q, AB_K, AB_V, AB_ZA, AB_UB, AB_VB, AB_ZB = 0, 3072, 4096, 5120, 6144, 7168, 8192
C_W_Q, C_W_KV, C_G, C_Z = 0, 2048, 5120, 5168
C_KC, C_VC, C_KS, C_VS, C_KW, C_VW = 0, 512, 1024, 1536, 2048, 2560


def _params(*sem):
    return pltpu.CompilerParams(dimension_semantics=sem, vmem_limit_bytes=VMEM_LIMIT)


def _dot(a, b):
    return jnp.dot(a, b, preferred_element_type=F32)


def _dot_nt(a, b):
    return lax.dot_general(a, b, (((1,), (1,)), ((), ())), preferred_element_type=F32)


def _sigmoid(x):
    return jax.nn.sigmoid(x)


def _silu(x):
    return x * _sigmoid(x)


def _rope_table_kernel(cos_ref, sa_ref, sb_ref, *, start, step):
    rows = cos_ref.shape[0]
    r = lax.broadcasted_iota(jnp.int32, (rows, HEAD_DIM), 0)
    lane = lax.broadcasted_iota(jnp.int32, (rows, HEAD_DIM), 1)
    pos = (start + r * step).astype(F32)
    j = (lane & (ROPE_HALF - 1)).astype(F32)
    inv = jnp.exp(-math.log(ROPE_THETA) * j / ROPE_HALF)
    ang = pos * inv
    c, s = jnp.cos(ang), jnp.sin(ang)
    in_rope = lane < ROPE_DIM
    cos_ref[...] = jnp.where(in_rope, c, 1.0)
    sa_ref[...] = jnp.where(in_rope & (lane >= ROPE_HALF), s, 0.0)
    sb_ref[...] = jnp.where(lane < ROPE_HALF, -s, 0.0)


def _rope_tables(rows, start, step):
    shp = jax.ShapeDtypeStruct((rows, HEAD_DIM), F32)
    return pl.pallas_call(
        functools.partial(_rope_table_kernel, start=start, step=step),
        out_shape=(shp, shp, shp), name="rope_tables")()


def _rms_kernel(x_ref, g_ref, o_ref):
    x = x_ref[...]
    y = x * lax.rsqrt(jnp.mean(x * x, axis=-1, keepdims=True) + NORM_EPS)
    o_ref[...] = (y * g_ref[...]).astype(o_ref.dtype)


def _rms_rows(x, gains, li, out_dtype):
    M, D = x.shape
    tm = min(M, ROW_TILE)
    return pl.pallas_call(
        _rms_kernel, grid=(M // tm,),
        in_specs=[pl.BlockSpec((tm, D), lambda i: (i, 0)),
                  pl.BlockSpec((None, 1, D), lambda i: (li, 0, 0))],
        out_specs=pl.BlockSpec((tm, D), lambda i: (i, 0)),
        out_shape=jax.ShapeDtypeStruct((M, D), out_dtype),
        compiler_params=_params("arbitrary"), name="rms_rows",
    )(x, gains.reshape(gains.shape[0], 1, D))


TILE_RAW, TILE_NORM, TILE_HALF_NORM = 0, 1, 2


def _mm_norm_kernel(x_ref, w_ref, g_ref, cos_ref, sa_ref, sb_ref, *rest, kinds, two_out):
    o_ref, wb_ref = rest[0], rest[-1]
    n = pl.program_id(0)

    @pl.when(pl.program_id(1) == 0)
    def _():
        wb_ref[...] = w_ref[...].astype(BF)

    acc = _dot(x_ref[...].astype(BF), wb_ref[...])
    heads = acc.shape[1] // HEAD_DIM

    def store(kind):
        normed = {TILE_RAW: 0, TILE_NORM: heads, TILE_HALF_NORM: heads // 2}[kind]
        for j in range(heads):
            cols = slice(j * HEAD_DIM, (j + 1) * HEAD_DIM)
            a = acc[:, cols]
            if j >= normed:
                o_ref[:, cols] = a
                continue
            y = a * lax.rsqrt(jnp.mean(a * a, axis=-1, keepdims=True) + NORM_EPS) * g_ref[...]
            if two_out:
                rest[1][:, cols] = y
            up = pltpu.roll(y, ROPE_HALF, 1)
            down = pltpu.roll(y, HEAD_DIM - ROPE_HALF, 1)
            o_ref[:, cols] = y * cos_ref[...] + up * sa_ref[...] + down * sb_ref[...]

    distinct = sorted(set(kinds))
    if len(distinct) == 1:
        store(distinct[0])
    else:
        for kind in distinct:
            hit = functools.reduce(jnp.logical_or, [n == t for t, k in enumerate(kinds) if k == kind])
            pl.when(hit)(functools.partial(store, kind))


def _matmul_norm(x, w, li, col0, kinds, gains, gain_of_tile, tables, two_out=False):
    M, K = x.shape
    tm, tn = min(M, MM_TILE_M), MM_TILE_N_WIDE
    assert col0 % tn == 0 and (not two_out or set(kinds) == {TILE_NORM})
    ncols = tn * len(kinds)
    cos, sa, sb = tables
    nt = cos.shape[0] // tm

    def gain_index(n, m):
        idx = 0
        for t, g in enumerate(gain_of_tile):
            idx = jnp.where(n == t, g, idx)
        return (idx, 0, 0)

    tspec = pl.BlockSpec((tm, HEAD_DIM), lambda n, m: (m % nt, 0))
    ospec = pl.BlockSpec((tm, tn), lambda n, m: (m, n))
    n_out = 2 if two_out else 1
    outs = pl.pallas_call(
        functools.partial(_mm_norm_kernel, kinds=tuple(kinds), two_out=two_out),
        grid=(len(kinds), M // tm),
        in_specs=[pl.BlockSpec((tm, K), lambda n, m: (m, 0)),
                  pl.BlockSpec((None, K, tn), lambda n, m: (li, 0, n + col0 // tn)),
                  pl.BlockSpec((None, 1, HEAD_DIM), gain_index), tspec, tspec, tspec],
        out_specs=[ospec] * n_out, out_shape=[jax.ShapeDtypeStruct((M, ncols), F32)] * n_out,
        scratch_shapes=[pltpu.VMEM((K, tn), BF)],
        compiler_params=_params("arbitrary", "arbitrary"), name="matmul_norm",
    )(x, w, gains, cos, sa, sb)
    return (outs[1], outs[0]) if two_out else outs[0]


def _mm_kernel(x_ref, w_ref, o_ref, wb_ref):
    @pl.when(pl.program_id(1) == 0)
    def _():
        wb_ref[...] = w_ref[...].astype(BF)

    o_ref[...] = _dot(x_ref[...].astype(BF), wb_ref[...])


def _matmul(x, w, li, col0, ncols):
    M, K = x.shape
    tm = min(M, MM_TILE_M)
    tn = MM_TILE_N_WIDE if ncols % MM_TILE_N_WIDE == 0 and col0 % MM_TILE_N_WIDE == 0 else MM_TILE_N
    if li is None:
        wspec = pl.BlockSpec((K, tn), lambda n, m: (0, n + col0 // tn))
    else:
        wspec = pl.BlockSpec((None, K, tn), lambda n, m: (li, 0, n + col0 // tn))
    return pl.pallas_call(
        _mm_kernel, grid=(ncols // tn, M // tm),
        in_specs=[pl.BlockSpec((tm, K), lambda n, m: (m, 0)), wspec],
        out_specs=pl.BlockSpec((tm, tn), lambda n, m: (m, n)),
        out_shape=jax.ShapeDtypeStruct((M, ncols), F32),
        scratch_shapes=[pltpu.VMEM((K, tn), BF)],
        compiler_params=_params("arbitrary", "arbitrary"), name="matmul",
    )(x, w)


def _proj_res_kernel(*refs, ks, with_bf):
    n = len(ks)
    lhs = refs[:n]
    w_ref, res_ref, o_ref = refs[n:n + 3]
    wb_ref = refs[-1]

    @pl.when(pl.program_id(1) == 0)
    def _():
        wb_ref[...] = w_ref[...].astype(BF)

    acc = res_ref[...]
    off = 0
    for r, k in zip(lhs, ks):
        acc = acc + _dot(r[...].astype(BF), wb_ref[off:off + k, :])
        off += k
    o_ref[...] = acc
    if with_bf:
        refs[n + 3][...] = acc.astype(BF)


def _proj_residual(lhs_list, w, li, res, with_bf):
    M, N = res.shape
    ks = tuple(a.shape[1] for a in lhs_list)
    K = sum(ks)
    tm, tn = min(M, MM_TILE_M), MM_TILE_N_WIDE
    ospec = pl.BlockSpec((tm, tn), lambda n, m: (m, n))
    out_shape = [jax.ShapeDtypeStruct((M, N), F32)]
    if with_bf:
        out_shape.append(jax.ShapeDtypeStruct((M, N), BF))
    outs = pl.pallas_call(
        functools.partial(_proj_res_kernel, ks=ks, with_bf=with_bf),
        grid=(N // tn, M // tm),
        in_specs=[pl.BlockSpec((tm, k), lambda n, m: (m, 0)) for k in ks]
        + [pl.BlockSpec((None, K, tn), lambda n, m: (li, 0, n)), ospec],
        out_specs=[ospec] * len(out_shape), out_shape=out_shape,
        scratch_shapes=[pltpu.VMEM((K, tn), BF)],
        compiler_params=_params("arbitrary", "arbitrary"), name="proj_residual",
    )(*lhs_list, w, res)
    return outs if with_bf else (outs[0], outs[0])


def _ple_kernel(hl_ref, wg_ref, p_ref, wp_ref, h_ref, o_ref, wgb_ref, wpb_ref):
    @pl.when(pl.program_id(1) == 0)
    def _():
        wgb_ref[...] = wg_ref[...].astype(BF)
        wpb_ref[...] = wp_ref[...].astype(BF)

    gate = _sigmoid(_dot(hl_ref[...].astype(BF), wgb_ref[...]))
    pp = _dot(p_ref[...].astype(BF), wpb_ref[...])
    o_ref[...] = h_ref[...] + gate * pp


def _ple(h, h_lhs, p, w_gate, w_ple, layer):
    M, N = h.shape
    K, KP = h_lhs.shape[1], p.shape[2]
    tm, tn = min(M, MM_TILE_M), MM_TILE_N_WIDE
    ospec = pl.BlockSpec((tm, tn), lambda n, m: (m, n))
    return pl.pallas_call(
        _ple_kernel, grid=(N // tn, M // tm),
        in_specs=[pl.BlockSpec((tm, K), lambda n, m: (m, 0)),
                  pl.BlockSpec((None, K, tn), lambda n, m: (layer, 0, n)),
                  pl.BlockSpec((None, tm, KP), lambda n, m: (layer, m, 0)),
                  pl.BlockSpec((None, KP, tn), lambda n, m: (layer, 0, n)), ospec],
        out_specs=ospec, out_shape=jax.ShapeDtypeStruct((M, N), F32),
        scratch_shapes=[pltpu.VMEM((K, tn), BF), pltpu.VMEM((KP, tn), BF)],
        compiler_params=_params("arbitrary", "arbitrary"), name="ple",
    )(h_lhs, w_gate, p, w_ple, h)


def _dilated_kernel(q0_ref, q1_ref, q2_ref, k_ref, v_ref, z_ref, o_ref, num_ref, m_ref, l_ref, *, S):
    blk = Q_BLOCK
    row = lax.broadcasted_iota(jnp.int32, (blk, blk), 0)
    col = lax.broadcasted_iota(jnp.int32, (blk, blk), 1)
    q_refs = (q0_ref, q1_ref, q2_ref)
    for g, (window, d) in enumerate(A_PATTERNS):
        assert window // d == blk
        nblk = S // d // blk
        q_ref = q_refs[g]

        def unit(u, carry, g=g, d=d, nblk=nblk, q_ref=q_ref):
            c = u // nblk
            i = u - c * nblk
            start = c + d * blk * i
            rows = pl.ds(start, blk, stride=d) if d > 1 else pl.ds(pl.multiple_of(start, blk), blk)
            q = q_ref[rows, :].astype(BF)
            kc = k_ref[rows, :].astype(BF)
            vc = v_ref[rows, :].astype(BF)
            s_c = jnp.where(col <= row, _dot_nt(q, kc) * SCALE_LOG2E, NEG_INF)
            if nblk > 1:
                pstart = jnp.maximum(start - d * blk, c)
                prows = pl.ds(pstart, blk, stride=d) if d > 1 else pl.ds(pl.multiple_of(pstart, blk), blk)
                kp = k_ref[prows, :].astype(BF)
                vp = v_ref[prows, :].astype(BF)
                s_p = jnp.where((col >= row) & (i > 0), _dot_nt(q, kp) * SCALE_LOG2E, NEG_INF)
                m = jnp.max(jnp.maximum(s_c, s_p), axis=-1, keepdims=True)
                e_c = jnp.exp2(s_c - m)
                e_p = jnp.exp2(s_p - m)
                l = jnp.sum(e_c + e_p, axis=-1, keepdims=True)
                num = _dot(e_c.astype(BF), vc) + _dot(e_p.astype(BF), vp)
            else:
                m = jnp.max(s_c, axis=-1, keepdims=True)
                e_c = jnp.exp2(s_c - m)
                l = jnp.sum(e_c, axis=-1, keepdims=True)
                num = _dot(e_c.astype(BF), vc)
            num_ref[g, rows, :] = num
            m_ref[g, rows, :] = jnp.broadcast_to(m, (blk, HEAD_DIM))
            l_ref[g, rows, :] = jnp.broadcast_to(l, (blk, HEAD_DIM))
            return carry

        lax.fori_loop(0, d * nblk, unit, 0, unroll=DILATED_UNROLL)

    def merge(i, carry):
        rows = pl.ds(pl.multiple_of(i * blk, blk), blk)
        ms = [m_ref[g, rows, :] for g in range(A_GROUPS)]
        m_all = jnp.maximum(jnp.maximum(ms[0], ms[1]), ms[2])
        ws = [jnp.exp2(m - m_all) for m in ms]
        num = ws[0] * num_ref[0, rows, :] + ws[1] * num_ref[1, rows, :] + ws[2] * num_ref[2, rows, :]
        den = ws[0] * l_ref[0, rows, :] + ws[1] * l_ref[1, rows, :] + ws[2] * l_ref[2, rows, :]
        o_ref[rows, :] = ((num / den) * _silu(z_ref[rows, :])).astype(o_ref.dtype)
        return carry

    lax.fori_loop(0, S // blk, merge, 0, unroll=2)


def _dilated_prompt(proj, B, S):
    H = A_HEADS
    hb = lambda off: off // HEAD_DIM

    def col_spec(base):
        return pl.BlockSpec((S, HEAD_DIM), lambda b, h: (b, base + h))

    return pl.pallas_call(
        functools.partial(_dilated_kernel, S=S), grid=(B, H),
        in_specs=[col_spec(hb(AB_Q)), col_spec(hb(AB_Q) + H), col_spec(hb(AB_Q) + 2 * H), col_spec(hb(AB_K)),
                  col_spec(hb(AB_V)), col_spec(hb(AB_ZA))],
        out_specs=col_spec(0),
        out_shape=jax.ShapeDtypeStruct((B * S, A_WIDTH), BF),
        scratch_shapes=[pltpu.VMEM((A_GROUPS, S, HEAD_DIM), F32)] * 3,
        compiler_params=_params("arbitrary", "arbitrary"), name="dilated_prompt",
    )(*([proj] * 6))


def _layer_norm(v, g):
    xc = v - jnp.mean(v, axis=-1, keepdims=True)
    return xc * lax.rsqrt(jnp.mean(xc * xc, axis=-1, keepdims=True) + NORM_EPS) * g


def _gmlp_kernel(u_ref, v_ref, z_ref, g_ref, ws_ref, bst_ref, vn_ref, o_ref):
    vn = _layer_norm(v_ref[...], g_ref[...])
    vn_ref[...] = vn
    row = lax.broadcasted_iota(jnp.int32, (B_CHUNK, B_CHUNK), 0)
    col = lax.broadcasted_iota(jnp.int32, (B_CHUNK, B_CHUNK), 1)
    for g in range(B_GROUPS):
        cols = slice(g * LANES, (g + 1) * LANES)
        w = jnp.where(row >= col, ws_ref[g], 0.0).astype(BF)
        mixed = _dot(w, vn[:, cols].astype(BF)) + bst_ref[:, g:g + 1]
        o_ref[:, cols] = ((u_ref[:, cols] * mixed) * _silu(z_ref[:, cols])).astype(o_ref.dtype)


def _gmlp_prompt(proj, ln_sgu, ws_b, bs_b, li):
    M = proj.shape[0]
    wb = B_WIDTH
    blk = lambda off: pl.BlockSpec((B_CHUNK, wb), lambda i: (i, off // wb))
    bst = jnp.swapaxes(bs_b, 1, 2)
    return pl.pallas_call(
        _gmlp_kernel, grid=(M // B_CHUNK,),
        in_specs=[blk(AB_UB), blk(AB_VB), blk(AB_ZB),
                  pl.BlockSpec((None, 1, wb), lambda i: (li, 0, 0)),
                  pl.BlockSpec((None, B_GROUPS, B_CHUNK, B_CHUNK), lambda i: (li, 0, 0, 0)),
                  pl.BlockSpec((None, B_CHUNK, B_GROUPS), lambda i: (li, 0, 0))],
        out_specs=[pl.BlockSpec((B_CHUNK, wb), lambda i: (i, 0))] * 2,
        out_shape=[jax.ShapeDtypeStruct((M, wb), F32), jax.ShapeDtypeStruct((M, wb), BF)],
        compiler_params=_params("arbitrary"), name="gmlp_prompt",
    )(proj, proj, proj, ln_sgu.reshape(-1, 1, wb), ws_b, bst)


def _compress_rows(x_ref, w_ref, wbase, h, nseg):
    first = jnp.zeros((nseg, HEAD_DIM), F32)
    second = jnp.zeros((nseg, HEAD_DIM), F32)
    for r in range(CMP_STRIDE):
        rows = x_ref[pl.ds(r, nseg, stride=CMP_STRIDE), :]
        first = first + rows * w_ref[wbase + r * C_KV_HEADS + h]
        second = second + rows * w_ref[wbase + (CMP_STRIDE + r) * C_KV_HEADS + h]
    c = first + pltpu.roll(second, nseg - 1, 0)
    seg = lax.broadcasted_iota(jnp.int32, (nseg, HEAD_DIM), 0)
    return jnp.where(seg < nseg - 1, c, 0.0)


def _compress_kernel(cwk_ref, cwv_ref, kc_ref, vc_ref, g_ref, ko_ref, vo_ref, *, li, nseg):
    h = pl.program_id(1)
    wbase = li * CMP_LEN * C_KV_HEADS
    kc = _compress_rows(kc_ref, cwk_ref, wbase, h, nseg)
    ko_ref[...] = kc * lax.rsqrt(jnp.mean(kc * kc, axis=-1, keepdims=True) + NORM_EPS) * g_ref[...]
    vo_ref[...] = _compress_rows(vc_ref, cwv_ref, wbase, h, nseg)


def _compress_prompt(proj, cw_k, cw_v, gain, li, B, S):
    nseg = S // CMP_STRIDE
    hb = lambda off: off // HEAD_DIM
    smem = pl.BlockSpec(memory_space=pltpu.SMEM)
    ospec = pl.BlockSpec((None, None, nseg, HEAD_DIM), lambda b, h: (b, h, 0, 0))
    oshape = jax.ShapeDtypeStruct((B, C_KV_HEADS, nseg, HEAD_DIM), F32)
    return pl.pallas_call(
        functools.partial(_compress_kernel, li=li, nseg=nseg), grid=(B, C_KV_HEADS),
        in_specs=[smem, smem,
                  pl.BlockSpec((S, HEAD_DIM), lambda b, h: (b, hb(C_KC) + h)),
                  pl.BlockSpec((S, HEAD_DIM), lambda b, h: (b, hb(C_VC) + h)),
                  pl.BlockSpec((1, HEAD_DIM), lambda b, h: (0, 0))],
        out_specs=[ospec, ospec], out_shape=[oshape, oshape],
        compiler_params=_params("arbitrary", "arbitrary"), name="compress_prompt",
    )(cw_k.reshape(-1), cw_v.reshape(-1), proj, proj, gain)


def _cmp_to_sel(nrows, ncols, rows_per_seg):
    n = lax.broadcasted_iota(jnp.int32, (nrows, ncols), 0) // rows_per_seg
    j = lax.broadcasted_iota(jnp.int32, (nrows, ncols), 1)
    shared = (jnp.minimum(n * CMP_STRIDE + CMP_LEN, (j + 1) * SEL_BLOCK)
              - jnp.maximum(n * CMP_STRIDE, j * SEL_BLOCK))
    return jnp.maximum(shared, 0).astype(F32) / CMP_LEN


def _cmp_to_sel_t(nrows, ncols, seg_mask):
    j = lax.broadcasted_iota(jnp.int32, (nrows, ncols), 0)
    n = lax.broadcasted_iota(jnp.int32, (nrows, ncols), 1) & seg_mask
    shared = (jnp.minimum(n * CMP_STRIDE + CMP_LEN, (j + 1) * SEL_BLOCK)
              - jnp.maximum(n * CMP_STRIDE, j * SEL_BLOCK))
    return jnp.maximum(shared, 0).astype(F32) / CMP_LEN


def _rank_select_t(score, rows):
    sc = score[:rows]
    blk = lax.broadcasted_iota(jnp.int32, sc.shape, 0)
    rank = jnp.zeros(sc.shape, F32)
    for jp in range(rows):
        other = sc[jp:jp + 1, :]
        ahead = (other > sc) | ((other == sc) & (blk > jp))
        rank = rank + jnp.where(ahead, 1.0, 0.0)
    picked = jnp.where((rank < SEL_TOP) & (sc > NEG_INF), 1.0, 0.0)
    return jnp.concatenate([picked, jnp.zeros((score.shape[0] - rows, score.shape[1]), F32)], axis=0)


def _flash_loop_t(q4, k_ref, vt_ref, n_tiles, tk, bias_fn):
    cols = q4.shape[0]
    reps = cols // LANES

    def body(t, carry):
        m, l, acc = carry
        k0 = pl.multiple_of(t * tk, tk)
        s = _dot_nt(k_ref[pl.ds(k0, tk), :], q4) * SCALE_LOG2E
        s = s + jnp.concatenate([bias_fn(k0)] * reps, axis=1)
        m_new = jnp.maximum(m, jnp.max(s, axis=0, keepdims=True))
        m_safe = jnp.where(m_new == NEG_INF, 0.0, m_new)
        alpha = jnp.exp2(m - m_safe)
        e = jnp.exp2(s - m_safe)
        l = alpha * l + jnp.sum(e, axis=0, keepdims=True)
        acc = alpha * acc + _dot(vt_ref[t], e.astype(BF))
        return m_new, l, acc

    init = (jnp.full((1, cols), NEG_INF, F32), jnp.zeros((1, cols), F32), jnp.zeros((HEAD_DIM, cols), F32))
    _, l, acc = lax.fori_loop(0, n_tiles, body, init)
    return acc / l


def _stack_heads(ref):
    return jnp.concatenate([ref[:, g * HEAD_DIM:(g + 1) * HEAD_DIM] for g in range(C_GQA)], axis=0)


def _nsa_kernel(qn_ref, qr_ref, kcmp_ref, vcmp_ref, ks_ref, vs_ref, kw_ref, vw_ref, g_ref, z_ref, o_ref,
                ksb_ref, kwb_ref, vst_ref, vwt_ref, vct_ref, exp_ref, c2s_ref, *, S, n_sel, sel_tk):
    qb = Q_BLOCK
    qi = pl.program_id(2)
    q0 = qi * qb
    nseg = kcmp_ref.shape[0]
    cols4 = C_GQA * qb
    win_keys = C_WIN + qb

    @pl.when(qi == 0)
    def _():
        ksb_ref[...] = ks_ref[...].astype(BF)
        kwb_ref[...] = kw_ref[...].astype(BF)
        for t in range(S // sel_tk):
            vst_ref[t] = vs_ref[t * sel_tk:(t + 1) * sel_tk, :].T.astype(BF)
        for t in range(S // qb):
            vwt_ref[t] = vw_ref[t * qb:(t + 1) * qb, :].T.astype(BF)
        vct_ref[...] = vcmp_ref[...].T.astype(BF)
        key_blk = lax.broadcasted_iota(jnp.int32, (S, LANES), 0) // SEL_BLOCK
        exp_ref[...] = jnp.where(key_blk == lax.broadcasted_iota(jnp.int32, (S, LANES), 1), 1.0, 0.0).astype(BF)
        c2s_ref[...] = _cmp_to_sel_t(LANES, C_GQA * nseg, nseg - 1).astype(BF)

    p_q = q0 + lax.broadcasted_iota(jnp.int32, (1, qb), 1)
    p_4 = q0 + (lax.broadcasted_iota(jnp.int32, (1, cols4), 1) & (qb - 1))

    qn4 = _stack_heads(qn_ref).astype(BF)
    seg = lax.broadcasted_iota(jnp.int32, (nseg, cols4), 0)
    vis = seg * CMP_STRIDE + (CMP_LEN - 1) <= p_4
    s = jnp.where(vis, _dot_nt(kcmp_ref[...].astype(BF), qn4) * SCALE_LOG2E, NEG_INF)
    m = jnp.max(s, axis=0, keepdims=True)
    m = jnp.where(m == NEG_INF, 0.0, m)
    e = jnp.exp2(s - m)
    prob_b = (e / jnp.maximum(jnp.sum(e, axis=0, keepdims=True), TINY)).astype(BF)
    o_cmp = _dot(vct_ref[...], prob_b)

    prob_stack = jnp.concatenate([prob_b[:, g * qb:(g + 1) * qb] for g in range(C_GQA)], axis=0)
    imp = _dot(c2s_ref[...], prob_stack)
    blk = lax.broadcasted_iota(jnp.int32, (LANES, qb), 0)
    cur = p_q // SEL_BLOCK
    forced = (blk == 0) | (blk == cur) | (blk == cur - 1)
    score = jnp.where(forced, imp + FORCE_BONUS, imp)
    score = jnp.where(blk <= cur, score, NEG_INF)
    sel = _rank_select_t(score, -(-n_sel // SUBLANES) * SUBLANES).astype(BF)

    qr4 = _stack_heads(qr_ref).astype(BF)

    def sel_bias(k0):
        chosen = _dot(exp_ref[pl.ds(k0, sel_tk), :], sel)
        kpos = k0 + lax.broadcasted_iota(jnp.int32, (sel_tk, 1), 0)
        return jnp.where((chosen > 0.5) & (kpos <= p_q), 0.0, NEG_INF)

    o_sel = _flash_loop_t(qr4, ksb_ref, vst_ref, (q0 + qb + sel_tk - 1) // sel_tk, sel_tk, sel_bias)

    w0 = pl.multiple_of(jnp.maximum(q0 - C_WIN, 0), qb)
    s = _dot_nt(kwb_ref[pl.ds(w0, win_keys), :], qr4) * SCALE_LOG2E
    kpos = w0 + lax.broadcasted_iota(jnp.int32, (win_keys, 1), 0)
    bias = jnp.where((kpos <= p_q) & (kpos >= p_q - C_WIN), 0.0, NEG_INF)
    s = s + jnp.concatenate([bias] * C_GQA, axis=1)
    e = jnp.exp2(s - jnp.max(s, axis=0, keepdims=True))
    den = jnp.sum(e, axis=0, keepdims=True)
    e = e.astype(BF)
    t0 = w0 // qb
    acc = _dot(vwt_ref[t0], e[:qb])
    for i in range(1, win_keys // qb):
        acc = acc + _dot(vwt_ref[t0 + i], e[i * qb:(i + 1) * qb])
    o_win = acc / den

    gates_t = _sigmoid(g_ref[...].T)
    for g in range(C_GQA):
        cs = slice(g * HEAD_DIM, (g + 1) * HEAD_DIM)
        o = (gates_t[g:g + 1] * o_cmp[:, cs] + gates_t[C_GQA + g:C_GQA + g + 1] * o_sel[:, cs]
             + gates_t[2 * C_GQA + g:2 * C_GQA + g + 1] * o_win[:, cs])
        o_ref[:, cs] = (o.T * _silu(z_ref[:, cs])).astype(o_ref.dtype)


def _nsa_prompt(qn, qr, kcmp, vcmp, kv, gates, z, B, S):
    nq = S // Q_BLOCK
    nseg = kcmp.shape[2]
    gw = C_GQA * HEAD_DIM
    sel_tk = 256
    assert S >= C_WIN + Q_BLOCK and S % sel_tk == 0
    hb = lambda off: off // HEAD_DIM
    qspec = pl.BlockSpec((Q_BLOCK, gw), lambda b, h, i: (b * nq + i, h))
    cspec = pl.BlockSpec((None, None, nseg, HEAD_DIM), lambda b, h, i: (b, h, 0, 0))

    def kv_spec(base):
        return pl.BlockSpec((S, HEAD_DIM), lambda b, h, i: (b, base + h))

    return pl.pallas_call(
        functools.partial(_nsa_kernel, S=S, n_sel=-(-S // SEL_BLOCK), sel_tk=sel_tk),
        grid=(B, C_KV_HEADS, nq),
        in_specs=[qspec, qspec, cspec, cspec, kv_spec(hb(C_KS)), kv_spec(hb(C_VS)), kv_spec(hb(C_KW)),
                  kv_spec(hb(C_VW)),
                  pl.BlockSpec((Q_BLOCK, LANES), lambda b, h, i: (b * nq + i, h)), qspec],
        out_specs=qspec, out_shape=jax.ShapeDtypeStruct((B * S, C_WIDTH), BF),
        scratch_shapes=[pltpu.VMEM((S, HEAD_DIM), BF), pltpu.VMEM((S, HEAD_DIM), BF),
                        pltpu.VMEM((S // sel_tk, HEAD_DIM, sel_tk), BF),
                        pltpu.VMEM((S // Q_BLOCK, HEAD_DIM, Q_BLOCK), BF),
                        pltpu.VMEM((HEAD_DIM, nseg), BF), pltpu.VMEM((S, LANES), BF),
                        pltpu.VMEM((LANES, C_GQA * nseg), BF)],
        compiler_params=_params("arbitrary", "arbitrary", "arbitrary"), name="nsa_prompt",
    )(qn, qr, kcmp, vcmp, kv, kv, kv, kv, gates, z)


def _gate_weights(w_in_c, li):
    wg = w_in_c[li, :, C_G:C_Z].reshape(D_MODEL, 3, C_KV_HEADS, C_GQA)
    wg = jnp.transpose(wg, (0, 2, 1, 3)).reshape(D_MODEL, C_KV_HEADS, 3 * C_GQA)
    wg = jnp.pad(wg, ((0, 0), (0, 0), (0, LANES - 3 * C_GQA)))
    return wg.reshape(D_MODEL, C_KV_HEADS * LANES)


def _emit_kernel(*refs, heads, n_src, skip, tm):
    srcs = refs[skip:skip + n_src]
    o_ref = refs[skip + n_src]
    for a in range(n_src):
        for h in range(heads):
            o_ref[a, pl.ds(h, tm, stride=heads), :] = srcs[a][:, h * HEAD_DIM:(h + 1) * HEAD_DIM]


def _emit_rows(srcs, heads, B, S, keep, li, n_layers, prev):
    width = heads * HEAD_DIM
    tm = EMIT_TILE
    spb, row0 = S // tm, (S - keep) // tm
    n_src = len(srcs)

    def src_spec(off):
        return pl.BlockSpec((tm, width), lambda b, i: (b * spb + row0 + i, off // width))

    in_specs = [src_spec(off) for _, off in srcs]
    args = [a for a, _ in srcs]
    aliases = {}
    if prev is not None:
        in_specs = [pl.BlockSpec(memory_space=pl.ANY)] + in_specs
        args = [prev] + args
        aliases = {0: 0}
    return pl.pallas_call(
        functools.partial(_emit_kernel, heads=heads, n_src=n_src, skip=len(aliases), tm=tm),
        grid=(B, keep // tm), in_specs=in_specs,
        out_specs=pl.BlockSpec((None, None, n_src, tm * heads, HEAD_DIM), lambda b, i: (b, li, 0, i, 0)),
        out_shape=jax.ShapeDtypeStruct((B, n_layers, n_src, keep * heads, HEAD_DIM), F32),
        input_output_aliases=aliases,
        compiler_params=_params("arbitrary", "arbitrary"), name="emit_rows",
    )(*args)


def _ab_in_proj(xn, li, prm, tables):
    w = prm["w_in_ab"]
    tiles = w.shape[2] // MM_TILE_N_WIDE
    q_tiles, k_tiles = A_GROUPS * A_WIDTH // MM_TILE_N_WIDE, A_WIDTH // MM_TILE_N_WIDE
    kinds = [TILE_NORM] * (q_tiles + k_tiles) + [TILE_RAW] * (tiles - q_tiles - k_tiles)
    gain_of_tile = [0] * q_tiles + [1] * k_tiles + [0] * (tiles - q_tiles - k_tiles)
    gains = jnp.stack([prm["qn_a"][li], prm["kn_a"][li]])[:, None]
    return _matmul_norm(xn, w, li, 0, kinds, gains, gain_of_tile, tables)


def _c_in_proj(xn, li, prm, tables):
    w = prm["w_in_c"]
    q_tiles = C_WIDTH // MM_TILE_N_WIDE
    qn, qr = _matmul_norm(xn, w, li, C_W_Q, [TILE_NORM] * q_tiles, prm["qn_c"][li][None, None],
                          [0] * q_tiles, tables, two_out=True)
    kv = _matmul_norm(xn, w, li, C_W_KV, [TILE_RAW, TILE_HALF_NORM, TILE_HALF_NORM],
                      prm["kn_c"][li, 1:3][:, None], [0, 0, 1], tables)
    z = _matmul(xn, w[li, :, C_Z:], None, 0, C_WIDTH)
    gates = _matmul(xn, _gate_weights(w, li), None, 0, C_KV_HEADS * LANES)
    return qn, qr, kv, z, gates


def _ab_layer_prompt(h, li, layer, p, prm, tables, B, S, outs, n_layers):
    xn = _rms_rows(h, prm["ln_ab"], li, BF)
    proj = _ab_in_proj(xn, li, prm, tables)
    ga = _dilated_prompt(proj, B, S)
    vn, gb = _gmlp_prompt(proj, prm["ln_sgu"], prm["ws_b"], prm["bs_b"], li)
    h1, h1b = _proj_residual([ga, gb], prm["w_out_ab"], li, h, True)
    h2 = _ple(h1, h1b, p, prm["w_ple_gate"], prm["w_ple"], layer)
    outs["a"] = _emit_rows([(proj, AB_K), (proj, AB_V)], A_HEADS, B, S, S, li, n_layers, outs.get("a"))
    chunk_start = ((S - 1) // B_CHUNK) * B_CHUNK
    return h2, vn.reshape(B, S, B_WIDTH)[:, chunk_start:]


def _c_layer_prompt(h, li, layer, p, prm, tables, B, S, outs, n_layers):
    xn = _rms_rows(h, prm["ln_c"], li, BF)
    qn, qr, kv, z, gates = _c_in_proj(xn, li, prm, tables)
    kcmp, vcmp = _compress_prompt(kv, prm["cw_k"], prm["cw_v"], prm["kn_c"][li, 0][None], li, B, S)
    lhs = _nsa_prompt(qn, qr, kcmp, vcmp, kv, gates, z, B, S)
    h1, h1b = _proj_residual([lhs], prm["w_out_c"], li, h, True)
    h2 = _ple(h1, h1b, p, prm["w_ple_gate"], prm["w_ple"], layer)
    keep = min(C_WIN, S)
    emit = lambda key, k_off, v_off, rows: _emit_rows([(kv, k_off), (kv, v_off)], C_KV_HEADS, B, S, rows, li,
                                                      n_layers, outs.get(key))
    outs["cmp"] = emit("cmp", C_KC, C_VC, S)
    outs["sel"] = emit("sel", C_KS, C_VS, S)
    outs["win"] = emit("win", C_KW, C_VW, keep)
    return h2


def _prompt_trunk(x, p, prm):
    B, S, _ = x.shape
    tables = _rope_tables(S, 0, 1)
    h = x.reshape(B * S, D_MODEL)
    depth = p.shape[0]
    n_ab, n_c = (depth + 1) // 2, depth // 2
    p = p.reshape(depth, B * S, PLE_DIM)
    outs, b_rows = {}, []
    for i in range(depth):
        li = i // 2
        if i % 2 == 0:
            h, vb = _ab_layer_prompt(h, li, i, p, prm, tables, B, S, outs, n_ab)
            b_rows.append(vb)
        else:
            h = _c_layer_prompt(h, li, i, p, prm, tables, B, S, outs, n_c)

    def rows(key, heads):
        a = outs[key]
        return a.reshape(a.shape[:3] + (a.shape[3] // heads, heads, HEAD_DIM))

    return (h.reshape(B, S, D_MODEL), rows("a", A_HEADS), jnp.stack(b_rows, axis=1),
            rows("cmp", C_KV_HEADS), rows("sel", C_KV_HEADS), rows("win", C_KV_HEADS))


def _pad_rows(x, rows):
    return jnp.concatenate([x, jnp.zeros((rows - x.shape[0], x.shape[1]), x.dtype)], axis=0)


def _bf_round(x):
    return x.astype(BF).astype(F32)


def _dilated_sample_kernel(q_ref, kn_ref, pj_ref, kp_ref, vp_ref, o_ref, *, P):
    h = pl.program_id(1)
    qs = [q_ref[pl.ds(g * A_HEADS + h, 1), :] for g in range(A_GROUPS)]
    qmat = _pad_rows(jnp.concatenate(qs, axis=0), SUBLANES)
    k_new = kn_ref[pl.ds(h, 1), :]
    v_new = pj_ref[pl.ds(AB_V // HEAD_DIM + h, 1), :]
    z = pj_ref[pl.ds(AB_ZA // HEAD_DIM + h, 1), :]
    grp = lax.broadcasted_iota(jnp.int32, (SUBLANES, P), 0)
    delta = P - lax.broadcasted_iota(jnp.int32, (SUBLANES, P), 1)
    dil = jnp.where(grp == 0, A_PATTERNS[0][1], jnp.where(grp == 1, A_PATTERNS[1][1], A_PATTERNS[2][1]))
    win = jnp.where(grp == 0, A_PATTERNS[0][0], jnp.where(grp == 1, A_PATTERNS[1][0], A_PATTERNS[2][0]))
    valid = ((delta & (dil - 1)) == 0) & (delta <= win) & (grp < A_GROUPS)
    head_rows = pl.ds(h, P, stride=A_HEADS)
    s = jnp.where(valid, _dot_nt(qmat.astype(BF), kp_ref[head_rows, :].astype(BF)) * ATTN_SCALE, NEG_INF)
    s_new = jnp.sum(_bf_round(qmat) * _bf_round(k_new), axis=-1, keepdims=True) * ATTN_SCALE
    m = jnp.maximum(jnp.max(s, axis=-1, keepdims=True), s_new)
    e = jnp.exp(s - m)
    e_new = jnp.exp(s_new - m)
    den = jnp.sum(e, axis=-1, keepdims=True) + e_new
    num = _dot(e.astype(BF), vp_ref[head_rows, :].astype(BF)) + _bf_round(e_new) * _bf_round(v_new)
    live = lax.broadcasted_iota(jnp.int32, (SUBLANES, 1), 0) < A_GROUPS
    m_all = jnp.max(jnp.where(live, m, NEG_INF), axis=0, keepdims=True)
    w = jnp.where(live, jnp.exp(m - m_all), 0.0)
    num_t = jnp.sum(w * num, axis=0, keepdims=True)
    den_t = jnp.sum(w * den, axis=0, keepdims=True)
    o_ref[pl.ds(h, 1), :] = (num_t / den_t) * _silu(z)


def _dilated_sample(q_r, k_r, proj, cache, li):
    DB, P = cache.shape[0], cache.shape[3]
    cache = cache.reshape(cache.shape[:3] + (P * A_HEADS, HEAD_DIM))
    row3 = lambda x: x.reshape(DB, x.shape[1] // HEAD_DIM, HEAD_DIM)
    full = lambda x: pl.BlockSpec((None,) + x.shape[1:], lambda b, h: (b, 0, 0))
    q3, k3, p3 = row3(q_r), row3(k_r), row3(proj)
    cspec = lambda kv: pl.BlockSpec((None, None, None, P * A_HEADS, HEAD_DIM), lambda b, h: (b, li, kv, 0, 0))
    out = pl.pallas_call(
        functools.partial(_dilated_sample_kernel, P=P), grid=(DB, A_HEADS),
        in_specs=[full(q3), full(k3), full(p3), cspec(0), cspec(1)],
        out_specs=pl.BlockSpec((None, A_HEADS, HEAD_DIM), lambda b, h: (b, 0, 0)),
        out_shape=jax.ShapeDtypeStruct((DB, A_HEADS, HEAD_DIM), F32),
        compiler_params=_params("arbitrary", "arbitrary"), name="dilated_sample",
    )(q3, k3, p3, cache, cache)
    return out.reshape(DB, A_WIDTH)


def _gmlp_sample_kernel(u_ref, v_ref, z_ref, g_ref, w0_ref, b0_ref, vn_ref, o_ref):
    vn = _layer_norm(v_ref[...], g_ref[...])
    vn_ref[...] = vn
    mixed = _bf_round(w0_ref[...]) * _bf_round(vn) + b0_ref[...]
    o_ref[...] = (u_ref[...] * mixed) * _silu(z_ref[...])


def _gmlp_sample(proj, ln_sgu, ws_b, bs_b, li):
    M, wb = proj.shape[0], B_WIDTH
    blk = lambda off: pl.BlockSpec((M, wb), lambda i: (0, off // wb))
    vec = pl.BlockSpec((1, wb), lambda i: (0, 0))
    w0 = jnp.repeat(ws_b[li, :, 0, 0], wb // B_GROUPS)[None]
    b0 = jnp.repeat(bs_b[li, :, 0], wb // B_GROUPS)[None]
    shp = jax.ShapeDtypeStruct((M, wb), F32)
    return pl.pallas_call(
        _gmlp_sample_kernel, in_specs=[blk(AB_UB), blk(AB_VB), blk(AB_ZB), vec, vec, vec],
        out_specs=[pl.BlockSpec((M, wb), lambda i: (0, 0))] * 2, out_shape=[shp, shp], grid=(1,),
        compiler_params=_params("arbitrary"), name="gmlp_sample",
    )(proj, proj, proj, ln_sgu[li][None], w0, b0)


def _page_compress_kernel(pt_ref, w_ref, *refs, pages, segs):
    o_ref = refs[pages]
    tiles = CMP_STRIDE * C_KV_HEADS // SUBLANES
    low = lax.broadcasted_iota(jnp.int32, (SUBLANES, HEAD_DIM), 0) < C_KV_HEADS
    for j in range(pages):
        page = refs[j]
        for kv in range(2):
            folded = []
            for n in range(segs):
                base = n * tiles * SUBLANES
                first = second = None
                for i in range(tiles):
                    rows = page[kv, base + i * SUBLANES:base + (i + 1) * SUBLANES, :]
                    t1 = rows * w_ref[2 * kv, i * SUBLANES:(i + 1) * SUBLANES, :]
                    t2 = rows * w_ref[2 * kv + 1, i * SUBLANES:(i + 1) * SUBLANES, :]
                    first = t1 if first is None else first + t1
                    second = t2 if second is None else second + t2
                folded.append((first + pltpu.roll(first, C_KV_HEADS, 0), second + pltpu.roll(second, C_KV_HEADS, 0)))
            for i in range(segs // 2):
                out_rows = slice((j * segs // 2 + i) * SUBLANES, (j * segs // 2 + i + 1) * SUBLANES)
                o_ref[2 * kv, out_rows, :] = jnp.where(low, folded[2 * i][0], folded[2 * i + 1][0])
                o_ref[2 * kv + 1, out_rows, :] = jnp.where(low, folded[2 * i][1], folded[2 * i + 1][1])


def _page_compress(pool, page_table, cw_k, cw_v, li, pages=4):
    DB, n_pages = page_table.shape
    page = pool.shape[3]
    assert 2 * C_KV_HEADS == SUBLANES and page % (2 * CMP_STRIDE) == 0
    pool = pool.reshape(pool.shape[:3] + (page * C_KV_HEADS, HEAD_DIM))
    segs = page // CMP_STRIDE
    rows_out = segs * C_KV_HEADS
    lanes = lambda w: jnp.broadcast_to(w.reshape(CMP_STRIDE * C_KV_HEADS, 1), (CMP_STRIDE * C_KV_HEADS, HEAD_DIM))
    w = jnp.stack([lanes(cw_k[li, :CMP_STRIDE]), lanes(cw_k[li, CMP_STRIDE:]),
                   lanes(cw_v[li, :CMP_STRIDE]), lanes(cw_v[li, CMP_STRIDE:])])

    def pspec(j):
        return pl.BlockSpec((None, None, 2, page * C_KV_HEADS, HEAD_DIM),
                            lambda b, i, pt: (pt[b * n_pages + i * pages + j], li, 0, 0, 0))

    return pl.pallas_call(
        functools.partial(_page_compress_kernel, pages=pages, segs=segs),
        grid_spec=pltpu.PrefetchScalarGridSpec(
            num_scalar_prefetch=1, grid=(DB, n_pages // pages),
            in_specs=[pl.BlockSpec(w.shape, lambda b, i, pt: (0, 0, 0))] + [pspec(j) for j in range(pages)],
            out_specs=pl.BlockSpec((None, 4, rows_out * pages, HEAD_DIM), lambda b, i, pt: (b, 0, i, 0))),
        out_shape=jax.ShapeDtypeStruct((DB, 4, n_pages * rows_out, HEAD_DIM), F32),
        compiler_params=_params("arbitrary", "arbitrary"), name="page_compress",
    )(page_table.reshape(-1), w, *([pool] * pages))


def _cmp_sample_kernel(fs_ref, q_ref, g_ref, o_ref, ix_ref, *, pos, n_sel, nj):
    rows = fs_ref.shape[1]
    row_i = lax.broadcasted_iota(jnp.int32, (rows, HEAD_DIM), 0)
    complete = row_i < rows - C_KV_HEADS
    kc = jnp.where(complete, fs_ref[0] + pltpu.roll(fs_ref[1], rows - C_KV_HEADS, 0), 0.0)
    vc = jnp.where(complete, fs_ref[2] + pltpu.roll(fs_ref[3], rows - C_KV_HEADS, 0), 0.0)
    kc = kc * lax.rsqrt(jnp.mean(kc * kc, axis=-1, keepdims=True) + NORM_EPS) * g_ref[...]
    col = lax.broadcasted_iota(jnp.int32, (C_HEADS, rows), 1)
    q_head = lax.broadcasted_iota(jnp.int32, (C_HEADS, rows), 0)
    vis = (((col // C_KV_HEADS) * CMP_STRIDE + (CMP_LEN - 1) <= pos)
           & ((col & (C_KV_HEADS - 1)) == q_head // C_GQA))
    s = jnp.where(vis, _dot_nt(q_ref[...].astype(BF), kc.astype(BF)) * ATTN_SCALE, NEG_INF)
    m = jnp.max(s, axis=-1, keepdims=True)
    m = jnp.where(m == NEG_INF, 0.0, m)
    e = jnp.exp(s - m)
    prob_b = (e / jnp.maximum(jnp.sum(e, axis=-1, keepdims=True), TINY)).astype(BF)
    o_ref[...] = _dot(prob_b, vc.astype(BF))

    imp_heads = _dot(prob_b, _cmp_to_sel(rows, nj, C_KV_HEADS).astype(BF))
    lane = lax.broadcasted_iota(jnp.int32, (1, nj), 1)
    cur = pos // SEL_BLOCK
    forced = (lane == 0) | (lane == cur) | (lane == cur - 1)
    ii = lax.broadcasted_iota(jnp.int32, (nj, nj), 0)
    jj = lax.broadcasted_iota(jnp.int32, (nj, nj), 1)
    slot = lax.broadcasted_iota(jnp.int32, (nj, LANES), 1).astype(F32)
    blk_id = lax.broadcasted_iota(jnp.int32, (nj, LANES), 0)
    for h in range(C_KV_HEADS):
        imp = jnp.sum(imp_heads[h * C_GQA:(h + 1) * C_GQA], axis=0, keepdims=True)
        score = jnp.where(forced, imp + FORCE_BONUS, imp)
        score = jnp.where((lane <= cur) & (lane < n_sel), score, NEG_INF)
        s_row = jnp.broadcast_to(score, (nj, nj))
        s_col = jnp.sum(jnp.where(ii == jj, s_row, 0.0), axis=-1, keepdims=True)
        ahead = (s_row > s_col) | ((s_row == s_col) & (jj < ii))
        rank = jnp.sum(jnp.where(ahead, 1.0, 0.0), axis=-1, keepdims=True)
        hit = rank == slot
        idx = jnp.sum(jnp.where(hit, blk_id, 0), axis=0, keepdims=True)
        ok = jnp.sum(jnp.where(hit & (s_col > NEG_INF), 1, 0), axis=0, keepdims=True)
        ix_ref[h] = jnp.concatenate([idx, ok, jnp.zeros((SUBLANES - 2, LANES), jnp.int32)], axis=0)


def _cmp_sample(fs, qn, gain, pos):
    DB, _, rows, _ = fs.shape
    total = pos + 1
    n_sel = -(-total // SEL_BLOCK)
    nj = -(-n_sel // LANES) * LANES
    q3 = qn.reshape(DB, C_HEADS, HEAD_DIM)
    return pl.pallas_call(
        functools.partial(_cmp_sample_kernel, pos=pos, n_sel=n_sel, nj=nj), grid=(DB,),
        in_specs=[pl.BlockSpec((None, 4, rows, HEAD_DIM), lambda b: (b, 0, 0, 0)),
                  pl.BlockSpec((None, C_HEADS, HEAD_DIM), lambda b: (b, 0, 0)),
                  pl.BlockSpec((1, HEAD_DIM), lambda b: (0, 0))],
        out_specs=[pl.BlockSpec((None, C_HEADS, HEAD_DIM), lambda b: (b, 0, 0)),
                   pl.BlockSpec((None, C_KV_HEADS, SUBLANES, LANES), lambda b: (b, 0, 0, 0))],
        out_shape=[jax.ShapeDtypeStruct((DB, C_HEADS, HEAD_DIM), F32),
                   jax.ShapeDtypeStruct((DB, C_KV_HEADS, SUBLANES, LANES), jnp.int32)],
        compiler_params=_params("arbitrary"), name="cmp_sample",
    )(fs, q3, gain)


def _sel_sample_kernel(pt_ref, ix_ref, ok_ref, q_ref, kn_ref, pj_ref, *refs, nb, pos, n_past):
    kv_refs = refs[:nb]
    o_ref, m_ref, l_ref, acc_ref = refs[nb:]
    b, h, i = pl.program_id(0), pl.program_id(1), pl.program_id(2)

    @pl.when(i == 0)
    def _():
        m_ref[...] = jnp.full(m_ref.shape, NEG_INF, F32)
        l_ref[...] = jnp.zeros(l_ref.shape, F32)
        acc_ref[...] = jnp.zeros(acc_ref.shape, F32)

    q8 = _pad_rows(q_ref[pl.ds(h * C_GQA, C_GQA), :], SUBLANES).astype(BF)
    k_new = kn_ref[pl.ds(h, 1), :]
    v_new = pj_ref[pl.ds(C_VS // HEAD_DIM + h, 1), :]
    first_row = lax.broadcasted_iota(jnp.int32, (SEL_BLOCK, HEAD_DIM), 0) == 0
    r = lax.broadcasted_iota(jnp.int32, (1, SEL_BLOCK), 1)
    head_rows = pl.ds(h, SEL_BLOCK, stride=C_KV_HEADS)
    for j in range(nb):
        slot = (b * C_KV_HEADS + h) * SEL_TOP + i * nb + j
        blk = ix_ref[slot]
        is_past = blk < n_past
        kt = jnp.where(is_past, kv_refs[j][0, head_rows, :], jnp.where(first_row, k_new, 0.0))
        vt = jnp.where(is_past, kv_refs[j][1, head_rows, :], jnp.where(first_row, v_new, 0.0))
        ok = (blk * SEL_BLOCK + r <= pos) & (ok_ref[slot] > 0)
        s = jnp.where(ok, _dot_nt(q8, kt.astype(BF)) * ATTN_SCALE, NEG_INF)
        m_old = m_ref[...]
        m_new = jnp.maximum(m_old, jnp.max(s, axis=-1, keepdims=True))
        m_safe = jnp.where(m_new == NEG_INF, 0.0, m_new)
        alpha = jnp.exp(m_old - m_safe)
        e = jnp.exp(s - m_safe)
        l_ref[...] = alpha * l_ref[...] + jnp.sum(e, axis=-1, keepdims=True)
        acc_ref[...] = alpha * acc_ref[...] + _dot(e.astype(BF), vt.astype(BF))
        m_ref[...] = m_new

    @pl.when(i == pl.num_programs(2) - 1)
    def _():
        o_ref[pl.ds(h * C_GQA, C_GQA), :] = (acc_ref[...] / l_ref[...])[:C_GQA]


def _sel_sample(pool, page_table, idx, ok, qr, ks_r, proj, li, pos, nb=SEL_TOP):
    DB, n_pages = page_table.shape
    page = pool.shape[3]
    bpp = page // SEL_BLOCK
    n_past = n_pages * bpp
    pool = pool.reshape(pool.shape[:3] + (page * C_KV_HEADS, HEAD_DIM))
    row3 = lambda x: x.reshape(DB, x.shape[1] // HEAD_DIM, HEAD_DIM)
    q3, k3, p3 = row3(qr), row3(ks_r), row3(proj)
    full = lambda x: pl.BlockSpec((None,) + x.shape[1:], lambda b, h, i, pt, ix, okf: (b, 0, 0))

    def bspec(j):
        def imap(b, h, i, pt, ix, okf):
            blk = jnp.clip(ix[(b * C_KV_HEADS + h) * SEL_TOP + i * nb + j], 0, n_past - 1)
            return (pt[b * n_pages + blk // bpp], li, 0, blk % bpp, 0)
        return pl.BlockSpec((None, None, 2, SEL_BLOCK * C_KV_HEADS, HEAD_DIM), imap)

    out = pl.pallas_call(
        functools.partial(_sel_sample_kernel, nb=nb, pos=pos, n_past=n_past),
        grid_spec=pltpu.PrefetchScalarGridSpec(
            num_scalar_prefetch=3, grid=(DB, C_KV_HEADS, SEL_TOP // nb),
            in_specs=[full(q3), full(k3), full(p3)] + [bspec(j) for j in range(nb)],
            out_specs=pl.BlockSpec((None, C_HEADS, HEAD_DIM), lambda b, h, i, pt, ix, okf: (b, 0, 0)),
            scratch_shapes=[pltpu.VMEM((SUBLANES, 1), F32), pltpu.VMEM((SUBLANES, 1), F32),
                            pltpu.VMEM((SUBLANES, HEAD_DIM), F32)]),
        out_shape=jax.ShapeDtypeStruct((DB, C_HEADS, HEAD_DIM), F32),
        compiler_params=_params("arbitrary", "arbitrary", "arbitrary"), name="sel_sample",
    )(page_table.reshape(-1), idx, ok, q3, k3, p3, *([pool] * nb))
    return out


def _win_sample_kernel(q_ref, kn_ref, pj_ref, kp_ref, vp_ref, oc_ref, os_ref, g_ref, z_ref, o_ref):
    h = pl.program_id(1)
    heads = pl.ds(h * C_GQA, C_GQA)
    q8 = _pad_rows(q_ref[heads, :], SUBLANES)
    k_new = kn_ref[pl.ds(h, 1), :]
    v_new = pj_ref[pl.ds(C_VW // HEAD_DIM + h, 1), :]
    head_rows = pl.ds(h, kp_ref.shape[0] // C_KV_HEADS, stride=C_KV_HEADS)
    s = _dot_nt(q8.astype(BF), kp_ref[head_rows, :].astype(BF)) * ATTN_SCALE
    s_new = jnp.sum(_bf_round(q8) * _bf_round(k_new), axis=-1, keepdims=True) * ATTN_SCALE
    m = jnp.maximum(jnp.max(s, axis=-1, keepdims=True), s_new)
    e = jnp.exp(s - m)
    e_new = jnp.exp(s_new - m)
    den = jnp.sum(e, axis=-1, keepdims=True) + e_new
    p_past = (e / den).astype(BF)
    o_win = _dot(p_past, vp_ref[head_rows, :].astype(BF)) + _bf_round(e_new / den) * _bf_round(v_new)
    g = g_ref[pl.ds(h, 1), :]
    for a in range(C_GQA):
        gate = [_sigmoid(g[:, br * C_GQA + a:br * C_GQA + a + 1]) for br in range(3)]
        row = pl.ds(h * C_GQA + a, 1)
        o = gate[0] * oc_ref[row, :] + gate[1] * os_ref[row, :] + gate[2] * o_win[a:a + 1]
        o_ref[row, :] = o * _silu(z_ref[row, :])


def _win_sample(cache, qr, kw_r, proj, o_cmp, o_sel, gates, z, li):
    DB, W = cache.shape[0], cache.shape[3]
    assert W <= C_WIN
    cache = cache.reshape(cache.shape[:3] + (W * C_KV_HEADS, HEAD_DIM))
    row3 = lambda x: x.reshape(DB, x.shape[1] // HEAD_DIM, HEAD_DIM)
    full = lambda x: pl.BlockSpec((None,) + x.shape[1:], lambda b, h: (b, 0, 0))
    ins = [row3(qr), row3(kw_r), row3(proj)]
    tail = [o_cmp, o_sel, row3(gates), row3(z)]
    cspec = lambda kv: pl.BlockSpec((None, None, None, W * C_KV_HEADS, HEAD_DIM), lambda b, h: (b, li, kv, 0, 0))
    out = pl.pallas_call(
        _win_sample_kernel, grid=(DB, C_KV_HEADS),
        in_specs=[full(x) for x in ins] + [cspec(0), cspec(1)] + [full(x) for x in tail],
        out_specs=pl.BlockSpec((None, C_HEADS, HEAD_DIM), lambda b, h: (b, 0, 0)),
        out_shape=jax.ShapeDtypeStruct((DB, C_HEADS, HEAD_DIM), F32),
        compiler_params=_params("arbitrary", "arbitrary"), name="win_sample",
    )(*ins, cache, cache, *tail)
    return out.reshape(DB, C_WIDTH)


def _ab_layer_sample(h, li, layer, p, prm, tables, cache_a):
    DB = h.shape[0]
    xn = _rms_rows(h, prm["ln_ab"], li, F32)
    proj = _ab_in_proj(xn, li, prm, tables)
    q_r = proj[:, AB_Q:AB_Q + A_GROUPS * A_WIDTH]
    k_r = proj[:, AB_K:AB_K + A_WIDTH]
    ga = _dilated_sample(q_r, k_r, proj, cache_a, li)
    vn, gb = _gmlp_sample(proj, prm["ln_sgu"], prm["ws_b"], prm["bs_b"], li)
    h1, _ = _proj_residual([ga, gb], prm["w_out_ab"], li, h, False)
    h2 = _ple(h1, h1, p, prm["w_ple_gate"], prm["w_ple"], layer)
    kv = jnp.stack([k_r.reshape(DB, 1, A_HEADS, HEAD_DIM),
                    proj[:, AB_V:AB_V + A_WIDTH].reshape(DB, 1, A_HEADS, HEAD_DIM)], axis=1)
    return h2, kv, vn.reshape(DB, 1, B_WIDTH)


def _c_layer_sample(h, li, layer, p, prm, tables, past, pos):
    DB = h.shape[0]
    xn = _rms_rows(h, prm["ln_c"], li, F32)
    qn, qr, proj, z, gates = _c_in_proj(xn, li, prm, tables)
    ks_r = proj[:, C_KS:C_KS + C_KV_WIDTH]
    kw_r = proj[:, C_KW:C_KW + C_KV_WIDTH]
    fs = _page_compress(past["cmp"], past["page_table"], prm["cw_k"], prm["cw_v"], li)
    o_cmp, picks = _cmp_sample(fs, qn, prm["kn_c"][li, 0][None], pos)
    idx = picks[:, :, 0, :SEL_TOP].reshape(-1)
    ok = picks[:, :, 1, :SEL_TOP].reshape(-1)
    o_sel = _sel_sample(past["sel"], past["page_table"], idx, ok, qr, ks_r, proj, li, pos)
    lhs = _win_sample(past["win"], qr, kw_r, proj, o_cmp, o_sel, gates, z, li)
    h1, _ = _proj_residual([lhs], prm["w_out_c"], li, h, False)
    h2 = _ple(h1, h1, p, prm["w_ple_gate"], prm["w_ple"], layer)
    kvs = (DB, 1, C_KV_HEADS, HEAD_DIM)
    col = lambda off: proj[:, off:off + C_KV_WIDTH].reshape(kvs)
    cmp_rows = jnp.stack([col(C_KC), col(C_VC)], axis=1)
    sel_rows = jnp.stack([ks_r.reshape(kvs), col(C_VS)], axis=1)
    win_rows = jnp.stack([kw_r.reshape(kvs), col(C_VW)], axis=1)
    return h2, cmp_rows, sel_rows, win_rows


def _sample_trunk(x, p, past, prm):
    DB, S, _ = x.shape
    assert S == 1
    pos = past["page_table"].shape[1] * past["cmp"].shape[3]
    assert pos % CMP_STRIDE == 0 and past["cmp"].shape[3] % SEL_BLOCK == 0
    tables = _rope_tables(DB, pos, 0)
    h = x.reshape(DB, D_MODEL)
    a_rows, b_rows, cmp_rows, sel_rows, win_rows = [], [], [], [], []
    pi = p.reshape(p.shape[0], DB, PLE_DIM)
    for i in range(p.shape[0]):
        li = i // 2
        if i % 2 == 0:
            h, kv, vb = _ab_layer_sample(h, li, i, pi, prm, tables, past["a"])
            a_rows.append(kv)
            b_rows.append(vb)
        else:
            h, c, s, w = _c_layer_sample(h, li, i, pi, prm, tables, past, pos)
            cmp_rows.append(c)
            sel_rows.append(s)
            win_rows.append(w)
    st = lambda xs: jnp.stack(xs, axis=1)
    return h.reshape(DB, S, D_MODEL), st(a_rows), st(b_rows), st(cmp_rows), st(sel_rows), st(win_rows)


def kernel(x_prompt, x_sample, cache_a_kv, cache_c_cmp_kv, cache_c_sel_kv, cache_c_win_kv, page_table,
           p_prompt, p_sample, ln_ab, w_in_ab, qn_a, kn_a, ln_sgu, ws_b, bs_b, w_out_ab,
           ln_c, w_in_c, qn_c, kn_c, cw_k, cw_v, w_out_c, w_ple, w_ple_gate):
    prm = dict(ln_ab=ln_ab, w_in_ab=w_in_ab, qn_a=qn_a, kn_a=kn_a, ln_sgu=ln_sgu, ws_b=ws_b, bs_b=bs_b,
               w_out_ab=w_out_ab, ln_c=ln_c, w_in_c=w_in_c, qn_c=qn_c, kn_c=kn_c, cw_k=cw_k, cw_v=cw_v,
               w_out_c=w_out_c, w_ple=w_ple, w_ple_gate=w_ple_gate)
    y_p, a_p, b_p, cmp_p, sel_p, win_p = _prompt_trunk(x_prompt, p_prompt, prm)
    past = dict(a=cache_a_kv, cmp=cache_c_cmp_kv, sel=cache_c_sel_kv, win=cache_c_win_kv, page_table=page_table)
    y_s, a_s, b_s, cmp_s, sel_s, win_s = _sample_trunk(x_sample, p_sample, past, prm)
    return (y_p, y_s, a_p, a_s, b_p, b_s, cmp_p, cmp_s, sel_p, sel_s, win_p, win_s)
```

```python
import functools
import math

import jax
import jax.numpy as jnp
from jax import lax
from jax.experimental import pallas as pl
from jax.experimental.pallas import tpu as pltpu

F32 = jnp.float32
BF = jnp.bfloat16

D_MODEL = 2048
HEAD_DIM = 128
ROPE_DIM = HEAD_DIM // 4
ROPE_HALF = ROPE_DIM // 2
ROPE_THETA = 500000.0
NORM_EPS = 1e-6
Q_BLOCK = 128
PLE_DIM = 256
ATTN_SCALE = HEAD_DIM ** -0.5
SCALE_LOG2E = ATTN_SCALE * math.log2(math.e)
TINY = 1e-30
A_HEADS = D_MODEL // (2 * HEAD_DIM)
A_PATTERNS = ((128, 1), (512, 4), (2048, 16))
A_GROUPS = len(A_PATTERNS)
A_WIDTH = A_HEADS * HEAD_DIM
B_CHUNK = 128
B_WIDTH = D_MODEL - A_WIDTH
B_GROUPS = 8
C_HEADS = D_MODEL // HEAD_DIM
C_KV_HEADS = 4
C_GQA = C_HEADS // C_KV_HEADS
C_WIDTH = C_HEADS * HEAD_DIM
C_KV_WIDTH = C_KV_HEADS * HEAD_DIM
CMP_LEN = 32
CMP_STRIDE = 16
SEL_BLOCK = 64
SEL_TOP = 16
C_WIN = 512
FORCE_BONUS = 1000.0
NEG_INF = float("-inf")

LANES = 128
SUBLANES = 8
MM_TILE_M = 1024
MM_TILE_N = 512
MM_TILE_N_WIDE = 1024
ROW_TILE = 512
EMIT_TILE = 256
SAMPLE_ROWS = 16
VMEM_LIMIT = 56 * 1024 * 1024
DILATED_BATCH = 8

AB_Q, AB_K, AB_V, AB_ZA, AB_UB, AB_VB, AB_ZB = 0, 3072, 4096, 5120, 6144, 7168, 8192
C_W_Q, C_W_KV, C_G, C_Z = 0, 2048, 5120, 5168
C_KC, C_VC, C_KS, C_VS, C_KW, C_VW = 0, 512, 1024, 1536, 2048, 2560


def _params(*sem):
    return pltpu.CompilerParams(dimension_semantics=sem, vmem_limit_bytes=VMEM_LIMIT)


def _dot(a, b):
    return jnp.dot(a, b, preferred_element_type=F32)


def _dot_nt(a, b):
    return lax.dot_general(a, b, (((1,), (1,)), ((), ())), preferred_element_type=F32)


def _sigmoid(x):
    return jax.nn.sigmoid(x)


def _silu(x):
    return x * _sigmoid(x)


def _rope_table_kernel(cos_ref, sa_ref, sb_ref, *, start, step):
    rows = cos_ref.shape[0]
    r = lax.broadcasted_iota(jnp.int32, (rows, HEAD_DIM), 0)
    lane = lax.broadcasted_iota(jnp.int32, (rows, HEAD_DIM), 1)
    pos = (start + r * step).astype(F32)
    j = (lane & (ROPE_HALF - 1)).astype(F32)
    inv = jnp.exp(-math.log(ROPE_THETA) * j / ROPE_HALF)
    ang = pos * inv
    c, s = jnp.cos(ang), jnp.sin(ang)
    in_rope = lane < ROPE_DIM
    cos_ref[...] = jnp.where(in_rope, c, 1.0)
    sa_ref[...] = jnp.where(in_rope & (lane >= ROPE_HALF), s, 0.0)
    sb_ref[...] = jnp.where(lane < ROPE_HALF, -s, 0.0)


def _rope_tables(rows, start, step):
    shp = jax.ShapeDtypeStruct((rows, HEAD_DIM), F32)
    return pl.pallas_call(
        functools.partial(_rope_table_kernel, start=start, step=step),
        out_shape=(shp, shp, shp), name="rope_tables")()


def _rms_kernel(x_ref, g_ref, o_ref):
    x = x_ref[...]
    y = x * lax.rsqrt(jnp.mean(x * x, axis=-1, keepdims=True) + NORM_EPS)
    o_ref[...] = (y * g_ref[...]).astype(o_ref.dtype)


def _rms_rows(x, gains, li, out_dtype):
    M, D = x.shape
    tm = min(M, ROW_TILE)
    return pl.pallas_call(
        _rms_kernel, grid=(M // tm,),
        in_specs=[pl.BlockSpec((tm, D), lambda i: (i, 0)),
                  pl.BlockSpec((None, 1, D), lambda i: (li, 0, 0))],
        out_specs=pl.BlockSpec((tm, D), lambda i: (i, 0)),
        out_shape=jax.ShapeDtypeStruct((M, D), out_dtype),
        compiler_params=_params("arbitrary"), name="rms_rows",
    )(x, gains.reshape(gains.shape[0], 1, D))


TILE_RAW, TILE_NORM, TILE_HALF_NORM = 0, 1, 2


def _mm_norm_kernel(x_ref, w_ref, g_ref, cos_ref, sa_ref, sb_ref, *rest, kinds, two_out):
    o_ref, wb_ref = rest[0], rest[-1]
    n = pl.program_id(0)

    @pl.when(pl.program_id(1) == 0)
    def _():
        wb_ref[...] = w_ref[...].astype(BF)

    acc = _dot(x_ref[...].astype(BF), wb_ref[...])
    heads = acc.shape[1] // HEAD_DIM

    def store(kind):
        normed = {TILE_RAW: 0, TILE_NORM: heads, TILE_HALF_NORM: heads // 2}[kind]
        for j in range(heads):
            cols = slice(j * HEAD_DIM, (j + 1) * HEAD_DIM)
            a = acc[:, cols]
            if j >= normed:
                o_ref[:, cols] = a
                continue
            y = a * lax.rsqrt(jnp.mean(a * a, axis=-1, keepdims=True) + NORM_EPS) * g_ref[...]
            if two_out:
                rest[1][:, cols] = y
            up = pltpu.roll(y, ROPE_HALF, 1)
            down = pltpu.roll(y, HEAD_DIM - ROPE_HALF, 1)
            o_ref[:, cols] = y * cos_ref[...] + up * sa_ref[...] + down * sb_ref[...]

    distinct = sorted(set(kinds))
    if len(distinct) == 1:
        store(distinct[0])
    else:
        for kind in distinct:
            hit = functools.reduce(jnp.logical_or, [n == t for t, k in enumerate(kinds) if k == kind])
            pl.when(hit)(functools.partial(store, kind))


def _matmul_norm(x, w, li, col0, kinds, gains, gain_of_tile, tables, two_out=False):
    M, K = x.shape
    tm, tn = min(M, MM_TILE_M), MM_TILE_N_WIDE
    assert col0 % tn == 0 and (not two_out or set(kinds) == {TILE_NORM})
    ncols = tn * len(kinds)
    cos, sa, sb = tables
    nt = cos.shape[0] // tm

    def gain_index(n, m):
        idx = 0
        for t, g in enumerate(gain_of_tile):
            idx = jnp.where(n == t, g, idx)
        return (idx, 0, 0)

    tspec = pl.BlockSpec((tm, HEAD_DIM), lambda n, m: (m % nt, 0))
    ospec = pl.BlockSpec((tm, tn), lambda n, m: (m, n))
    n_out = 2 if two_out else 1
    outs = pl.pallas_call(
        functools.partial(_mm_norm_kernel, kinds=tuple(kinds), two_out=two_out),
        grid=(len(kinds), M // tm),
        in_specs=[pl.BlockSpec((tm, K), lambda n, m: (m, 0)),
                  pl.BlockSpec((None, K, tn), lambda n, m: (li, 0, n + col0 // tn)),
                  pl.BlockSpec((None, 1, HEAD_DIM), gain_index), tspec, tspec, tspec],
        out_specs=[ospec] * n_out, out_shape=[jax.ShapeDtypeStruct((M, ncols), F32)] * n_out,
        scratch_shapes=[pltpu.VMEM((K, tn), BF)],
        compiler_params=_params("arbitrary", "arbitrary"), name="matmul_norm",
    )(x, w, gains, cos, sa, sb)
    return (outs[1], outs[0]) if two_out else outs[0]


def _mm_kernel(x_ref, w_ref, o_ref, wb_ref):
    @pl.when(pl.program_id(1) == 0)
    def _():
        wb_ref[...] = w_ref[...].astype(BF)

    o_ref[...] = _dot(x_ref[...].astype(BF), wb_ref[...])


def _matmul(x, w, li, col0, ncols):
    M, K = x.shape
    tm = min(M, MM_TILE_M)
    tn = MM_TILE_N_WIDE if ncols % MM_TILE_N_WIDE == 0 and col0 % MM_TILE_N_WIDE == 0 else MM_TILE_N
    if li is None:
        wspec = pl.BlockSpec((K, tn), lambda n, m: (0, n + col0 // tn))
    else:
        wspec = pl.BlockSpec((None, K, tn), lambda n, m: (li, 0, n + col0 // tn))
    return pl.pallas_call(
        _mm_kernel, grid=(ncols // tn, M // tm),
        in_specs=[pl.BlockSpec((tm, K), lambda n, m: (m, 0)), wspec],
        out_specs=pl.BlockSpec((tm, tn), lambda n, m: (m, n)),
        out_shape=jax.ShapeDtypeStruct((M, ncols), F32),
        scratch_shapes=[pltpu.VMEM((K, tn), BF)],
        compiler_params=_params("arbitrary", "arbitrary"), name="matmul",
    )(x, w)


def _proj_res_kernel(*refs, ks, with_bf):
    n = len(ks)
    lhs = refs[:n]
    w_ref, res_ref, o_ref = refs[n:n + 3]
    wb_ref = refs[-1]

    @pl.when(pl.program_id(1) == 0)
    def _():
        wb_ref[...] = w_ref[...].astype(BF)

    acc = res_ref[...]
    off = 0
    for r, k in zip(lhs, ks):
        acc = acc + _dot(r[...].astype(BF), wb_ref[off:off + k, :])
        off += k
    o_ref[...] = acc
    if with_bf:
        refs[n + 3][...] = acc.astype(BF)


def _proj_residual(lhs_list, w, li, res, with_bf):
    M, N = res.shape
    ks = tuple(a.shape[1] for a in lhs_list)
    K = sum(ks)
    tm, tn = min(M, MM_TILE_M), MM_TILE_N_WIDE
    ospec = pl.BlockSpec((tm, tn), lambda n, m: (m, n))
    out_shape = [jax.ShapeDtypeStruct((M, N), F32)]
    if with_bf:
        out_shape.append(jax.ShapeDtypeStruct((M, N), BF))
    outs = pl.pallas_call(
        functools.partial(_proj_res_kernel, ks=ks, with_bf=with_bf),
        grid=(N // tn, M // tm),
        in_specs=[pl.BlockSpec((tm, k), lambda n, m: (m, 0)) for k in ks]
        + [pl.BlockSpec((None, K, tn), lambda n, m: (li, 0, n)), ospec],
        out_specs=[ospec] * len(out_shape), out_shape=out_shape,
        scratch_shapes=[pltpu.VMEM((K, tn), BF)],
        compiler_params=_params("arbitrary", "arbitrary"), name="proj_residual",
    )(*lhs_list, w, res)
    return outs if with_bf else (outs[0], outs[0])


def _ple_kernel(hl_ref, wg_ref, p_ref, wp_ref, h_ref, o_ref, wgb_ref, wpb_ref):
    @pl.when(pl.program_id(1) == 0)
    def _():
        wgb_ref[...] = wg_ref[...].astype(BF)
        wpb_ref[...] = wp_ref[...].astype(BF)

    gate = _sigmoid(_dot(hl_ref[...].astype(BF), wgb_ref[...]))
    pp = _dot(p_ref[...].astype(BF), wpb_ref[...])
    o_ref[...] = h_ref[...] + gate * pp


def _ple(h, h_lhs, p, w_gate, w_ple, layer):
    M, N = h.shape
    K, KP = h_lhs.shape[1], p.shape[2]
    tm, tn = min(M, MM_TILE_M), MM_TILE_N_WIDE
    ospec = pl.BlockSpec((tm, tn), lambda n, m: (m, n))
    return pl.pallas_call(
        _ple_kernel, grid=(N // tn, M // tm),
        in_specs=[pl.BlockSpec((tm, K), lambda n, m: (m, 0)),
                  pl.BlockSpec((None, K, tn), lambda n, m: (layer, 0, n)),
                  pl.BlockSpec((None, tm, KP), lambda n, m: (layer, m, 0)),
                  pl.BlockSpec((None, KP, tn), lambda n, m: (layer, 0, n)), ospec],
        out_specs=ospec, out_shape=jax.ShapeDtypeStruct((M, N), F32),
        scratch_shapes=[pltpu.VMEM((K, tn), BF), pltpu.VMEM((KP, tn), BF)],
        compiler_params=_params("arbitrary", "arbitrary"), name="ple",
    )(h_lhs, w_gate, p, w_ple, h)


def _dilated_kernel(q0_ref, q1_ref, q2_ref, k_ref, v_ref, z_ref, o_ref, num_ref, m_ref, l_ref, *, S):
    blk = Q_BLOCK
    row = lax.broadcasted_iota(jnp.int32, (blk, blk), 0)
    col = lax.broadcasted_iota(jnp.int32, (blk, blk), 1)
    q_refs = (q0_ref, q1_ref, q2_ref)
    for g, (window, d) in enumerate(A_PATTERNS):
        assert window // d == blk
        nblk = S // d // blk
        q_ref = q_refs[g]

        def batch(ub, carry, g=g, d=d, nblk=nblk, q_ref=q_ref):
            units = []
            for j in range(DILATED_BATCH):
                u = ub * DILATED_BATCH + j
                c = u // nblk
                i = u - c * nblk
                start = c + d * blk * i
                rows = pl.ds(start, blk, stride=d) if d > 1 else pl.ds(pl.multiple_of(start, blk), blk)
                q = q_ref[rows, :].astype(BF)
                s_c = _dot_nt(q, k_ref[rows, :].astype(BF))
                s_p, prows = None, None
                if nblk > 1:
                    pstart = jnp.maximum(start - d * blk, c)
                    prows = pl.ds(pstart, blk, stride=d) if d > 1 else pl.ds(pl.multiple_of(pstart, blk), blk)
                    s_p = _dot_nt(q, k_ref[prows, :].astype(BF))
                units.append((i, rows, prows, s_c, s_p))
            probs = []
            for i, rows, prows, s_c, s_p in units:
                s_c = jnp.where(col <= row, s_c * SCALE_LOG2E, NEG_INF)
                if s_p is not None:
                    s_p = jnp.where((col >= row) & (i > 0), s_p * SCALE_LOG2E, NEG_INF)
                    m = jnp.max(jnp.maximum(s_c, s_p), axis=-1, keepdims=True)
                    e_c, e_p = jnp.exp2(s_c - m), jnp.exp2(s_p - m)
                    l = jnp.sum(e_c + e_p, axis=-1, keepdims=True)
                    probs.append((m, l, e_c.astype(BF), e_p.astype(BF)))
                else:
                    m = jnp.max(s_c, axis=-1, keepdims=True)
                    e_c = jnp.exp2(s_c - m)
                    probs.append((m, jnp.sum(e_c, axis=-1, keepdims=True), e_c.astype(BF), None))
            for (i, rows, prows, _, _), (m, l, e_c, e_p) in zip(units, probs):
                num = _dot(e_c, v_ref[rows, :].astype(BF))
                if e_p is not None:
                    num = num + _dot(e_p, v_ref[prows, :].astype(BF))
                num_ref[g, rows, :] = num
                m_ref[g, rows, :] = jnp.broadcast_to(m, (blk, HEAD_DIM))
                l_ref[g, rows, :] = jnp.broadcast_to(l, (blk, HEAD_DIM))
            return carry

        assert (d * nblk) % DILATED_BATCH == 0
        lax.fori_loop(0, d * nblk // DILATED_BATCH, batch, 0)

    def merge(i, carry):
        rows = pl.ds(pl.multiple_of(i * blk, blk), blk)
        ms = [m_ref[g, rows, :] for g in range(A_GROUPS)]
        m_all = jnp.maximum(jnp.maximum(ms[0], ms[1]), ms[2])
        ws = [jnp.exp2(m - m_all) for m in ms]
        num = ws[0] * num_ref[0, rows, :] + ws[1] * num_ref[1, rows, :] + ws[2] * num_ref[2, rows, :]
        den = ws[0] * l_ref[0, rows, :] + ws[1] * l_ref[1, rows, :] + ws[2] * l_ref[2, rows, :]
        o_ref[rows, :] = ((num / den) * _silu(z_ref[rows, :])).astype(o_ref.dtype)
        return carry

    lax.fori_loop(0, S // blk, merge, 0, unroll=2)


def _dilated_prompt(proj, B, S):
    H = A_HEADS
    hb = lambda off: off // HEAD_DIM

    def col_spec(base):
        return pl.BlockSpec((S, HEAD_DIM), lambda b, h: (b, base + h))

    return pl.pallas_call(
        functools.partial(_dilated_kernel, S=S), grid=(B, H),
        in_specs=[col_spec(hb(AB_Q)), col_spec(hb(AB_Q) + H), col_spec(hb(AB_Q) + 2 * H), col_spec(hb(AB_K)),
                  col_spec(hb(AB_V)), col_spec(hb(AB_ZA))],
        out_specs=col_spec(0),
        out_shape=jax.ShapeDtypeStruct((B * S, A_WIDTH), BF),
        scratch_shapes=[pltpu.VMEM((A_GROUPS, S, HEAD_DIM), F32)] * 3,
        compiler_params=_params("arbitrary", "arbitrary"), name="dilated_prompt",
    )(*([proj] * 6))


def _layer_norm(v, g):
    xc = v - jnp.mean(v, axis=-1, keepdims=True)
    return xc * lax.rsqrt(jnp.mean(xc * xc, axis=-1, keepdims=True) + NORM_EPS) * g


def _gmlp_kernel(u_ref, v_ref, z_ref, g_ref, ws_ref, bst_ref, vn_ref, o_ref):
    vn = _layer_norm(v_ref[...], g_ref[...])
    vn_ref[...] = vn
    row = lax.broadcasted_iota(jnp.int32, (B_CHUNK, B_CHUNK), 0)
    col = lax.broadcasted_iota(jnp.int32, (B_CHUNK, B_CHUNK), 1)
    for g in range(B_GROUPS):
        cols = slice(g * LANES, (g + 1) * LANES)
        w = jnp.where(row >= col, ws_ref[g], 0.0).astype(BF)
        mixed = _dot(w, vn[:, cols].astype(BF)) + bst_ref[:, g:g + 1]
        o_ref[:, cols] = ((u_ref[:, cols] * mixed) * _silu(z_ref[:, cols])).astype(o_ref.dtype)


def _gmlp_prompt(proj, ln_sgu, ws_b, bs_b, li):
    M = proj.shape[0]
    wb = B_WIDTH
    blk = lambda off: pl.BlockSpec((B_CHUNK, wb), lambda i: (i, off // wb))
    bst = jnp.swapaxes(bs_b, 1, 2)
    return pl.pallas_call(
        _gmlp_kernel, grid=(M // B_CHUNK,),
        in_specs=[blk(AB_UB), blk(AB_VB), blk(AB_ZB),
                  pl.BlockSpec((None, 1, wb), lambda i: (li, 0, 0)),
                  pl.BlockSpec((None, B_GROUPS, B_CHUNK, B_CHUNK), lambda i: (li, 0, 0, 0)),
                  pl.BlockSpec((None, B_CHUNK, B_GROUPS), lambda i: (li, 0, 0))],
        out_specs=[pl.BlockSpec((B_CHUNK, wb), lambda i: (i, 0))] * 2,
        out_shape=[jax.ShapeDtypeStruct((M, wb), F32), jax.ShapeDtypeStruct((M, wb), BF)],
        compiler_params=_params("arbitrary"), name="gmlp_prompt",
    )(proj, proj, proj, ln_sgu.reshape(-1, 1, wb), ws_b, bst)


def _compress_rows(x_ref, w_ref, wbase, h, nseg):
    first = jnp.zeros((nseg, HEAD_DIM), F32)
    second = jnp.zeros((nseg, HEAD_DIM), F32)
    for r in range(CMP_STRIDE):
        rows = x_ref[pl.ds(r, nseg, stride=CMP_STRIDE), :]
        first = first + rows * w_ref[wbase + r * C_KV_HEADS + h]
        second = second + rows * w_ref[wbase + (CMP_STRIDE + r) * C_KV_HEADS + h]
    c = first + pltpu.roll(second, nseg - 1, 0)
    seg = lax.broadcasted_iota(jnp.int32, (nseg, HEAD_DIM), 0)
    return jnp.where(seg < nseg - 1, c, 0.0)


def _compress_kernel(cwk_ref, cwv_ref, kc_ref, vc_ref, g_ref, ko_ref, vo_ref, *, li, nseg):
    h = pl.program_id(1)
    wbase = li * CMP_LEN * C_KV_HEADS
    kc = _compress_rows(kc_ref, cwk_ref, wbase, h, nseg)
    ko_ref[...] = kc * lax.rsqrt(jnp.mean(kc * kc, axis=-1, keepdims=True) + NORM_EPS) * g_ref[...]
    vo_ref[...] = _compress_rows(vc_ref, cwv_ref, wbase, h, nseg)


def _compress_prompt(proj, cw_k, cw_v, gain, li, B, S):
    nseg = S // CMP_STRIDE
    hb = lambda off: off // HEAD_DIM
    smem = pl.BlockSpec(memory_space=pltpu.SMEM)
    ospec = pl.BlockSpec((None, None, nseg, HEAD_DIM), lambda b, h: (b, h, 0, 0))
    oshape = jax.ShapeDtypeStruct((B, C_KV_HEADS, nseg, HEAD_DIM), F32)
    return pl.pallas_call(
        functools.partial(_compress_kernel, li=li, nseg=nseg), grid=(B, C_KV_HEADS),
        in_specs=[smem, smem,
                  pl.BlockSpec((S, HEAD_DIM), lambda b, h: (b, hb(C_KC) + h)),
                  pl.BlockSpec((S, HEAD_DIM), lambda b, h: (b, hb(C_VC) + h)),
                  pl.BlockSpec((1, HEAD_DIM), lambda b, h: (0, 0))],
        out_specs=[ospec, ospec], out_shape=[oshape, oshape],
        compiler_params=_params("arbitrary", "arbitrary"), name="compress_prompt",
    )(cw_k.reshape(-1), cw_v.reshape(-1), proj, proj, gain)


def _cmp_to_sel(nrows, ncols, rows_per_seg):
    n = lax.broadcasted_iota(jnp.int32, (nrows, ncols), 0) // rows_per_seg
    j = lax.broadcasted_iota(jnp.int32, (nrows, ncols), 1)
    shared = (jnp.minimum(n * CMP_STRIDE + CMP_LEN, (j + 1) * SEL_BLOCK)
              - jnp.maximum(n * CMP_STRIDE, j * SEL_BLOCK))
    return jnp.maximum(shared, 0).astype(F32) / CMP_LEN


def _cmp_to_sel_t(nrows, ncols, seg_mask):
    j = lax.broadcasted_iota(jnp.int32, (nrows, ncols), 0)
    n = lax.broadcasted_iota(jnp.int32, (nrows, ncols), 1) & seg_mask
    shared = (jnp.minimum(n * CMP_STRIDE + CMP_LEN, (j + 1) * SEL_BLOCK)
              - jnp.maximum(n * CMP_STRIDE, j * SEL_BLOCK))
    return jnp.maximum(shared, 0).astype(F32) / CMP_LEN


def _rank_select_t(score, rows):
    sc = score[:rows]
    blk = lax.broadcasted_iota(jnp.int32, sc.shape, 0)
    rank = jnp.zeros(sc.shape, F32)
    for jp in range(rows):
        other = sc[jp:jp + 1, :]
        ahead = (other > sc) | ((other == sc) & (blk > jp))
        rank = rank + jnp.where(ahead, 1.0, 0.0)
    picked = jnp.where((rank < SEL_TOP) & (sc > NEG_INF), 1.0, 0.0)
    return jnp.concatenate([picked, jnp.zeros((score.shape[0] - rows, score.shape[1]), F32)], axis=0)


def _flash_loop_t(q4, k_ref, vt_ref, n_tiles, tk, bias_fn):
    reps = q4.shape[0] // LANES
    q_heads = [q4[g * LANES:(g + 1) * LANES] for g in range(reps)]

    def body(t, state):
        k0 = pl.multiple_of(t * tk, tk)
        kt = k_ref[pl.ds(k0, tk), :]
        vt = vt_ref[t]
        bias = bias_fn(k0)
        scores = [_dot_nt(kt, q_heads[g]) for g in range(reps)]
        mid = []
        for g in range(reps):
            m, l, _ = state[g]
            s = scores[g] * SCALE_LOG2E + bias
            m_new = jnp.maximum(m, jnp.max(s, axis=0, keepdims=True))
            m_safe = jnp.where(m_new == NEG_INF, 0.0, m_new)
            alpha = jnp.exp2(m - m_safe)
            e = jnp.exp2(s - m_safe)
            mid.append((m_new, alpha * l + jnp.sum(e, axis=0, keepdims=True), alpha, e.astype(BF)))
        return tuple((m_new, l, alpha * state[g][2] + _dot(vt, e))
                     for g, (m_new, l, alpha, e) in enumerate(mid))

    one = (jnp.full((1, LANES), NEG_INF, F32), jnp.zeros((1, LANES), F32), jnp.zeros((HEAD_DIM, LANES), F32))
    res = lax.fori_loop(0, n_tiles, body, (one,) * reps)
    return jnp.concatenate([acc / l for _, l, acc in res], axis=1)


def _stack_heads(ref):
    return jnp.concatenate([ref[:, g * HEAD_DIM:(g + 1) * HEAD_DIM] for g in range(C_GQA)], axis=0)


def _nsa_kernel(qn_ref, qr_ref, kcmp_ref, vcmp_ref, ks_ref, vs_ref, kw_ref, vw_ref, g_ref, z_ref, o_ref,
                ksb_ref, kwb_ref, vst_ref, vwt_ref, vct_ref, exp_ref, c2s_ref, *, S, n_sel, sel_tk):
    qb = Q_BLOCK
    qi = pl.program_id(2)
    q0 = qi * qb
    nseg = kcmp_ref.shape[0]
    cols4 = C_GQA * qb
    win_keys = C_WIN + qb

    @pl.when(qi == 0)
    def _():
        ksb_ref[...] = ks_ref[...].astype(BF)
        kwb_ref[...] = kw_ref[...].astype(BF)
        for t in range(S // sel_tk):
            vst_ref[t] = vs_ref[t * sel_tk:(t + 1) * sel_tk, :].T.astype(BF)
        for t in range(S // qb):
            vwt_ref[t] = vw_ref[t * qb:(t + 1) * qb, :].T.astype(BF)
        vct_ref[...] = vcmp_ref[...].T.astype(BF)
        key_blk = lax.broadcasted_iota(jnp.int32, (S, LANES), 0) // SEL_BLOCK
        exp_ref[...] = jnp.where(key_blk == lax.broadcasted_iota(jnp.int32, (S, LANES), 1), 1.0, 0.0).astype(BF)
        c2s_ref[...] = _cmp_to_sel_t(LANES, C_GQA * nseg, nseg - 1).astype(BF)

    p_q = q0 + lax.broadcasted_iota(jnp.int32, (1, qb), 1)
    p_4 = q0 + (lax.broadcasted_iota(jnp.int32, (1, cols4), 1) & (qb - 1))

    qn4 = _stack_heads(qn_ref).astype(BF)
    qr4 = _stack_heads(qr_ref).astype(BF)
    w0 = pl.multiple_of(jnp.maximum(q0 - C_WIN, 0), qb)
    s_cmp = _dot_nt(kcmp_ref[...].astype(BF), qn4)
    k_win = kwb_ref[pl.ds(w0, win_keys), :]
    s_win = [_dot_nt(k_win, qr4[g * qb:(g + 1) * qb]) for g in range(C_GQA)]

    seg = lax.broadcasted_iota(jnp.int32, (nseg, cols4), 0)
    vis = seg * CMP_STRIDE + (CMP_LEN - 1) <= p_4
    s = jnp.where(vis, s_cmp * SCALE_LOG2E, NEG_INF)
    m = jnp.max(s, axis=0, keepdims=True)
    m = jnp.where(m == NEG_INF, 0.0, m)
    e = jnp.exp2(s - m)
    prob_b = (e / jnp.maximum(jnp.sum(e, axis=0, keepdims=True), TINY)).astype(BF)
    o_cmp = _dot(vct_ref[...], prob_b)

    prob_stack = jnp.concatenate([prob_b[:, g * qb:(g + 1) * qb] for g in range(C_GQA)], axis=0)
    imp = _dot(c2s_ref[...], prob_stack)
    blk = lax.broadcasted_iota(jnp.int32, (LANES, qb), 0)
    cur = p_q // SEL_BLOCK
    forced = (blk == 0) | (blk == cur) | (blk == cur - 1)
    score = jnp.where(forced, imp + FORCE_BONUS, imp)
    score = jnp.where(blk <= cur, score, NEG_INF)
    sel = _rank_select_t(score, -(-n_sel // SUBLANES) * SUBLANES).astype(BF)

    kpos = w0 + lax.broadcasted_iota(jnp.int32, (win_keys, 1), 0)
    bias = jnp.where((kpos <= p_q) & (kpos >= p_q - C_WIN), 0.0, NEG_INF)
    t0 = w0 // qb
    win_e, win_den = [], []
    for g in range(C_GQA):
        s = s_win[g] * SCALE_LOG2E + bias
        e = jnp.exp2(s - jnp.max(s, axis=0, keepdims=True))
        win_den.append(jnp.sum(e, axis=0, keepdims=True))
        win_e.append(e.astype(BF))
    o_win = []
    for g in range(C_GQA):
        acc = _dot(vwt_ref[t0], win_e[g][:qb])
        for i in range(1, win_keys // qb):
            acc = acc + _dot(vwt_ref[t0 + i], win_e[g][i * qb:(i + 1) * qb])
        o_win.append(acc / win_den[g])
    o_win = jnp.concatenate(o_win, axis=1)

    def sel_bias(k0):
        chosen = _dot(exp_ref[pl.ds(k0, sel_tk), :], sel)
        kpos = k0 + lax.broadcasted_iota(jnp.int32, (sel_tk, 1), 0)
        return jnp.where((chosen > 0.5) & (kpos <= p_q), 0.0, NEG_INF)

    o_sel = _flash_loop_t(qr4, ksb_ref, vst_ref, (q0 + qb + sel_tk - 1) // sel_tk, sel_tk, sel_bias)

    gates_t = _sigmoid(g_ref[...].T)
    for g in range(C_GQA):
        cs = slice(g * HEAD_DIM, (g + 1) * HEAD_DIM)
        o = (gates_t[g:g + 1] * o_cmp[:, cs] + gates_t[C_GQA + g:C_GQA + g + 1] * o_sel[:, cs]
             + gates_t[2 * C_GQA + g:2 * C_GQA + g + 1] * o_win[:, cs])
        o_ref[:, cs] = (o.T * _silu(z_ref[:, cs])).astype(o_ref.dtype)


def _nsa_prompt(qn, qr, kcmp, vcmp, kv, gates, z, B, S):
    nq = S // Q_BLOCK
    nseg = kcmp.shape[2]
    gw = C_GQA * HEAD_DIM
    sel_tk = 512
    assert S >= C_WIN + Q_BLOCK and S % sel_tk == 0
    hb = lambda off: off // HEAD_DIM
    qspec = pl.BlockSpec((Q_BLOCK, gw), lambda b, h, i: (b * nq + i, h))
    cspec = pl.BlockSpec((None, None, nseg, HEAD_DIM), lambda b, h, i: (b, h, 0, 0))

    def kv_spec(base):
        return pl.BlockSpec((S, HEAD_DIM), lambda b, h, i: (b, base + h))

    return pl.pallas_call(
        functools.partial(_nsa_kernel, S=S, n_sel=-(-S // SEL_BLOCK), sel_tk=sel_tk),
        grid=(B, C_KV_HEADS, nq),
        in_specs=[qspec, qspec, cspec, cspec, kv_spec(hb(C_KS)), kv_spec(hb(C_VS)), kv_spec(hb(C_KW)),
                  kv_spec(hb(C_VW)),
                  pl.BlockSpec((Q_BLOCK, LANES), lambda b, h, i: (b * nq + i, h)), qspec],
        out_specs=qspec, out_shape=jax.ShapeDtypeStruct((B * S, C_WIDTH), BF),
        scratch_shapes=[pltpu.VMEM((S, HEAD_DIM), BF), pltpu.VMEM((S, HEAD_DIM), BF),
                        pltpu.VMEM((S // sel_tk, HEAD_DIM, sel_tk), BF),
                        pltpu.VMEM((S // Q_BLOCK, HEAD_DIM, Q_BLOCK), BF),
                        pltpu.VMEM((HEAD_DIM, nseg), BF), pltpu.VMEM((S, LANES), BF),
                        pltpu.VMEM((LANES, C_GQA * nseg), BF)],
        compiler_params=_params("arbitrary", "arbitrary", "arbitrary"), name="nsa_prompt",
    )(qn, qr, kcmp, vcmp, kv, kv, kv, kv, gates, z)


def _gate_weights(w_in_c, li):
    wg = w_in_c[li, :, C_G:C_Z].reshape(D_MODEL, 3, C_KV_HEADS, C_GQA)
    wg = jnp.transpose(wg, (0, 2, 1, 3)).reshape(D_MODEL, C_KV_HEADS, 3 * C_GQA)
    wg = jnp.pad(wg, ((0, 0), (0, 0), (0, LANES - 3 * C_GQA)))
    return wg.reshape(D_MODEL, C_KV_HEADS * LANES)


def _emit_kernel(*refs, heads, n_src, skip, tm):
    srcs = refs[skip:skip + n_src]
    o_ref = refs[skip + n_src]
    for a in range(n_src):
        for h in range(heads):
            o_ref[a, pl.ds(h, tm, stride=heads), :] = srcs[a][:, h * HEAD_DIM:(h + 1) * HEAD_DIM]


def _emit_rows(srcs, heads, B, S, keep, li, n_layers, prev):
    width = heads * HEAD_DIM
    tm = EMIT_TILE
    spb, row0 = S // tm, (S - keep) // tm
    n_src = len(srcs)

    def src_spec(off):
        return pl.BlockSpec((tm, width), lambda b, i: (b * spb + row0 + i, off // width))

    in_specs = [src_spec(off) for _, off in srcs]
    args = [a for a, _ in srcs]
    aliases = {}
    if prev is not None:
        in_specs = [pl.BlockSpec(memory_space=pl.ANY)] + in_specs
        args = [prev] + args
        aliases = {0: 0}
    return pl.pallas_call(
        functools.partial(_emit_kernel, heads=heads, n_src=n_src, skip=len(aliases), tm=tm),
        grid=(B, keep // tm), in_specs=in_specs,
        out_specs=pl.BlockSpec((None, None, n_src, tm * heads, HEAD_DIM), lambda b, i: (b, li, 0, i, 0)),
        out_shape=jax.ShapeDtypeStruct((B, n_layers, n_src, keep * heads, HEAD_DIM), F32),
        input_output_aliases=aliases,
        compiler_params=_params("arbitrary", "arbitrary"), name="emit_rows",
    )(*args)


def _ab_in_proj(xn, li, prm, tables):
    w = prm["w_in_ab"]
    tiles = w.shape[2] // MM_TILE_N_WIDE
    q_tiles, k_tiles = A_GROUPS * A_WIDTH // MM_TILE_N_WIDE, A_WIDTH // MM_TILE_N_WIDE
    kinds = [TILE_NORM] * (q_tiles + k_tiles) + [TILE_RAW] * (tiles - q_tiles - k_tiles)
    gain_of_tile = [0] * q_tiles + [1] * k_tiles + [0] * (tiles - q_tiles - k_tiles)
    gains = jnp.stack([prm["qn_a"][li], prm["kn_a"][li]])[:, None]
    return _matmul_norm(xn, w, li, 0, kinds, gains, gain_of_tile, tables)


def _c_in_proj(xn, li, prm, tables):
    w = prm["w_in_c"]
    q_tiles = C_WIDTH // MM_TILE_N_WIDE
    qn, qr = _matmul_norm(xn, w, li, C_W_Q, [TILE_NORM] * q_tiles, prm["qn_c"][li][None, None],
                          [0] * q_tiles, tables, two_out=True)
    kv = _matmul_norm(xn, w, li, C_W_KV, [TILE_RAW, TILE_HALF_NORM, TILE_HALF_NORM],
                      prm["kn_c"][li, 1:3][:, None], [0, 0, 1], tables)
    z = _matmul(xn, w[li, :, C_Z:], None, 0, C_WIDTH)
    gates = _matmul(xn, _gate_weights(w, li), None, 0, C_KV_HEADS * LANES)
    return qn, qr, kv, z, gates


def _ab_layer_prompt(h, li, layer, p, prm, tables, B, S, outs, n_layers):
    xn = _rms_rows(h, prm["ln_ab"], li, BF)
    proj = _ab_in_proj(xn, li, prm, tables)
    ga = _dilated_prompt(proj, B, S)
    vn, gb = _gmlp_prompt(proj, prm["ln_sgu"], prm["ws_b"], prm["bs_b"], li)
    h1, h1b = _proj_residual([ga, gb], prm["w_out_ab"], li, h, True)
    h2 = _ple(h1, h1b, p, prm["w_ple_gate"], prm["w_ple"], layer)
    outs["a"] = _emit_rows([(proj, AB_K), (proj, AB_V)], A_HEADS, B, S, S, li, n_layers, outs.get("a"))
    chunk_start = ((S - 1) // B_CHUNK) * B_CHUNK
    return h2, vn.reshape(B, S, B_WIDTH)[:, chunk_start:]


def _c_layer_prompt(h, li, layer, p, prm, tables, B, S, outs, n_layers):
    xn = _rms_rows(h, prm["ln_c"], li, BF)
    qn, qr, kv, z, gates = _c_in_proj(xn, li, prm, tables)
    kcmp, vcmp = _compress_prompt(kv, prm["cw_k"], prm["cw_v"], prm["kn_c"][li, 0][None], li, B, S)
    lhs = _nsa_prompt(qn, qr, kcmp, vcmp, kv, gates, z, B, S)
    h1, h1b = _proj_residual([lhs], prm["w_out_c"], li, h, True)
    h2 = _ple(h1, h1b, p, prm["w_ple_gate"], prm["w_ple"], layer)
    keep = min(C_WIN, S)
    emit = lambda key, k_off, v_off, rows: _emit_rows([(kv, k_off), (kv, v_off)], C_KV_HEADS, B, S, rows, li,
                                                      n_layers, outs.get(key))
    outs["cmp"] = emit("cmp", C_KC, C_VC, S)
    outs["sel"] = emit("sel", C_KS, C_VS, S)
    outs["win"] = emit("win", C_KW, C_VW, keep)
    return h2


def _prompt_trunk(x, p, prm):
    B, S, _ = x.shape
    tables = _rope_tables(S, 0, 1)
    h = x.reshape(B * S, D_MODEL)
    depth = p.shape[0]
    n_ab, n_c = (depth + 1) // 2, depth // 2
    p = p.reshape(depth, B * S, PLE_DIM)
    outs, b_rows = {}, []
    for i in range(depth):
        li = i // 2
        if i % 2 == 0:
            h, vb = _ab_layer_prompt(h, li, i, p, prm, tables, B, S, outs, n_ab)
            b_rows.append(vb)
        else:
            h = _c_layer_prompt(h, li, i, p, prm, tables, B, S, outs, n_c)

    def rows(key, heads):
        a = outs[key]
        return a.reshape(a.shape[:3] + (a.shape[3] // heads, heads, HEAD_DIM))

    return (h.reshape(B, S, D_MODEL), rows("a", A_HEADS), jnp.stack(b_rows, axis=1),
            rows("cmp", C_KV_HEADS), rows("sel", C_KV_HEADS), rows("win", C_KV_HEADS))


def _pad_rows(x, rows):
    return jnp.concatenate([x, jnp.zeros((rows - x.shape[0], x.shape[1]), x.dtype)], axis=0)


def _bf_round(x):
    return x.astype(BF).astype(F32)


def _dilated_sample_kernel(q_ref, kn_ref, pj_ref, kp_ref, vp_ref, o_ref, *, P):
    h = pl.program_id(1)
    qs = [q_ref[pl.ds(g * A_HEADS + h, 1), :] for g in range(A_GROUPS)]
    qmat = _pad_rows(jnp.concatenate(qs, axis=0), SUBLANES)
    k_new = kn_ref[pl.ds(h, 1), :]
    v_new = pj_ref[pl.ds(AB_V // HEAD_DIM + h, 1), :]
    z = pj_ref[pl.ds(AB_ZA // HEAD_DIM + h, 1), :]
    grp = lax.broadcasted_iota(jnp.int32, (SUBLANES, P), 0)
    delta = P - lax.broadcasted_iota(jnp.int32, (SUBLANES, P), 1)
    dil = jnp.where(grp == 0, A_PATTERNS[0][1], jnp.where(grp == 1, A_PATTERNS[1][1], A_PATTERNS[2][1]))
    win = jnp.where(grp == 0, A_PATTERNS[0][0], jnp.where(grp == 1, A_PATTERNS[1][0], A_PATTERNS[2][0]))
    valid = ((delta & (dil - 1)) == 0) & (delta <= win) & (grp < A_GROUPS)
    head_rows = pl.ds(h, P, stride=A_HEADS)
    s = jnp.where(valid, _dot_nt(qmat.astype(BF), kp_ref[head_rows, :].astype(BF)) * ATTN_SCALE, NEG_INF)
    s_new = jnp.sum(_bf_round(qmat) * _bf_round(k_new), axis=-1, keepdims=True) * ATTN_SCALE
    m = jnp.maximum(jnp.max(s, axis=-1, keepdims=True), s_new)
    e = jnp.exp(s - m)
    e_new = jnp.exp(s_new - m)
    den = jnp.sum(e, axis=-1, keepdims=True) + e_new
    num = _dot(e.astype(BF), vp_ref[head_rows, :].astype(BF)) + _bf_round(e_new) * _bf_round(v_new)
    live = lax.broadcasted_iota(jnp.int32, (SUBLANES, 1), 0) < A_GROUPS
    m_all = jnp.max(jnp.where(live, m, NEG_INF), axis=0, keepdims=True)
    w = jnp.where(live, jnp.exp(m - m_all), 0.0)
    num_t = jnp.sum(w * num, axis=0, keepdims=True)
    den_t = jnp.sum(w * den, axis=0, keepdims=True)
    o_ref[pl.ds(h, 1), :] = (num_t / den_t) * _silu(z)


def _dilated_sample(q_r, k_r, proj, cache, li):
    DB, P = cache.shape[0], cache.shape[3]
    cache = cache.reshape(cache.shape[:3] + (P * A_HEADS, HEAD_DIM))
    row3 = lambda x: x.reshape(DB, x.shape[1] // HEAD_DIM, HEAD_DIM)
    full = lambda x: pl.BlockSpec((None,) + x.shape[1:], lambda b, h: (b, 0, 0))
    q3, k3, p3 = row3(q_r), row3(k_r), row3(proj)
    cspec = lambda kv: pl.BlockSpec((None, None, None, P * A_HEADS, HEAD_DIM), lambda b, h: (b, li, kv, 0, 0))
    out = pl.pallas_call(
        functools.partial(_dilated_sample_kernel, P=P), grid=(DB, A_HEADS),
        in_specs=[full(q3), full(k3), full(p3), cspec(0), cspec(1)],
        out_specs=pl.BlockSpec((None, A_HEADS, HEAD_DIM), lambda b, h: (b, 0, 0)),
        out_shape=jax.ShapeDtypeStruct((DB, A_HEADS, HEAD_DIM), F32),
        compiler_params=_params("arbitrary", "arbitrary"), name="dilated_sample",
    )(q3, k3, p3, cache, cache)
    return out.reshape(DB, A_WIDTH)


def _gmlp_sample_kernel(u_ref, v_ref, z_ref, g_ref, w0_ref, b0_ref, vn_ref, o_ref):
    vn = _layer_norm(v_ref[...], g_ref[...])
    vn_ref[...] = vn
    mixed = _bf_round(w0_ref[...]) * _bf_round(vn) + b0_ref[...]
    o_ref[...] = (u_ref[...] * mixed) * _silu(z_ref[...])


def _gmlp_sample(proj, ln_sgu, ws_b, bs_b, li):
    M, wb = proj.shape[0], B_WIDTH
    blk = lambda off: pl.BlockSpec((M, wb), lambda i: (0, off // wb))
    vec = pl.BlockSpec((1, wb), lambda i: (0, 0))
    w0 = jnp.repeat(ws_b[li, :, 0, 0], wb // B_GROUPS)[None]
    b0 = jnp.repeat(bs_b[li, :, 0], wb // B_GROUPS)[None]
    shp = jax.ShapeDtypeStruct((M, wb), F32)
    return pl.pallas_call(
        _gmlp_sample_kernel, in_specs=[blk(AB_UB), blk(AB_VB), blk(AB_ZB), vec, vec, vec],
        out_specs=[pl.BlockSpec((M, wb), lambda i: (0, 0))] * 2, out_shape=[shp, shp], grid=(1,),
        compiler_params=_params("arbitrary"), name="gmlp_sample",
    )(proj, proj, proj, ln_sgu[li][None], w0, b0)


def _page_compress_kernel(pt_ref, w_ref, *refs, pages, segs):
    o_ref = refs[pages]
    tiles = CMP_STRIDE * C_KV_HEADS // SUBLANES
    low = lax.broadcasted_iota(jnp.int32, (SUBLANES, HEAD_DIM), 0) < C_KV_HEADS
    for j in range(pages):
        page = refs[j]
        for kv in range(2):
            folded = []
            for n in range(segs):
                base = n * tiles * SUBLANES
                first = second = None
                for i in range(tiles):
                    rows = page[kv, base + i * SUBLANES:base + (i + 1) * SUBLANES, :]
                    t1 = rows * w_ref[2 * kv, i * SUBLANES:(i + 1) * SUBLANES, :]
                    t2 = rows * w_ref[2 * kv + 1, i * SUBLANES:(i + 1) * SUBLANES, :]
                    first = t1 if first is None else first + t1
                    second = t2 if second is None else second + t2
                folded.append((first + pltpu.roll(first, C_KV_HEADS, 0), second + pltpu.roll(second, C_KV_HEADS, 0)))
            for i in range(segs // 2):
                out_rows = slice((j * segs // 2 + i) * SUBLANES, (j * segs // 2 + i + 1) * SUBLANES)
                o_ref[2 * kv, out_rows, :] = jnp.where(low, folded[2 * i][0], folded[2 * i + 1][0])
                o_ref[2 * kv + 1, out_rows, :] = jnp.where(low, folded[2 * i][1], folded[2 * i + 1][1])


def _page_compress(pool, page_table, cw_k, cw_v, li, pages=8):
    DB, n_pages = page_table.shape
    page = pool.shape[3]
    assert 2 * C_KV_HEADS == SUBLANES and page % (2 * CMP_STRIDE) == 0
    pool = pool.reshape(pool.shape[:3] + (page * C_KV_HEADS, HEAD_DIM))
    segs = page // CMP_STRIDE
    rows_out = segs * C_KV_HEADS
    lanes = lambda w: jnp.broadcast_to(w.reshape(CMP_STRIDE * C_KV_HEADS, 1), (CMP_STRIDE * C_KV_HEADS, HEAD_DIM))
    w = jnp.stack([lanes(cw_k[li, :CMP_STRIDE]), lanes(cw_k[li, CMP_STRIDE:]),
                   lanes(cw_v[li, :CMP_STRIDE]), lanes(cw_v[li, CMP_STRIDE:])])

    def pspec(j):
        return pl.BlockSpec((None, None, 2, page * C_KV_HEADS, HEAD_DIM),
                            lambda b, i, pt: (pt[b * n_pages + i * pages + j], li, 0, 0, 0))

    return pl.pallas_call(
        functools.partial(_page_compress_kernel, pages=pages, segs=segs),
        grid_spec=pltpu.PrefetchScalarGridSpec(
            num_scalar_prefetch=1, grid=(DB, n_pages // pages),
            in_specs=[pl.BlockSpec(w.shape, lambda b, i, pt: (0, 0, 0))] + [pspec(j) for j in range(pages)],
            out_specs=pl.BlockSpec((None, 4, rows_out * pages, HEAD_DIM), lambda b, i, pt: (b, 0, i, 0))),
        out_shape=jax.ShapeDtypeStruct((DB, 4, n_pages * rows_out, HEAD_DIM), F32),
        compiler_params=_params("arbitrary", "arbitrary"), name="page_compress",
    )(page_table.reshape(-1), w, *([pool] * pages))


def _cmp_sample_kernel(fs_ref, q_ref, g_ref, o_ref, ix_ref, *, pos, n_sel, nj):
    rows = fs_ref.shape[1]
    row_i = lax.broadcasted_iota(jnp.int32, (rows, HEAD_DIM), 0)
    complete = row_i < rows - C_KV_HEADS
    kc = jnp.where(complete, fs_ref[0] + pltpu.roll(fs_ref[1], rows - C_KV_HEADS, 0), 0.0)
    vc = jnp.where(complete, fs_ref[2] + pltpu.roll(fs_ref[3], rows - C_KV_HEADS, 0), 0.0)
    kc = kc * lax.rsqrt(jnp.mean(kc * kc, axis=-1, keepdims=True) + NORM_EPS) * g_ref[...]
    col = lax.broadcasted_iota(jnp.int32, (C_HEADS, rows), 1)
    q_head = lax.broadcasted_iota(jnp.int32, (C_HEADS, rows), 0)
    vis = (((col // C_KV_HEADS) * CMP_STRIDE + (CMP_LEN - 1) <= pos)
           & ((col & (C_KV_HEADS - 1)) == q_head // C_GQA))
    s = jnp.where(vis, _dot_nt(q_ref[...].astype(BF), kc.astype(BF)) * ATTN_SCALE, NEG_INF)
    m = jnp.max(s, axis=-1, keepdims=True)
    m = jnp.where(m == NEG_INF, 0.0, m)
    e = jnp.exp(s - m)
    prob_b = (e / jnp.maximum(jnp.sum(e, axis=-1, keepdims=True), TINY)).astype(BF)
    o_ref[...] = _dot(prob_b, vc.astype(BF))

    imp_heads = _dot(prob_b, _cmp_to_sel(rows, nj, C_KV_HEADS).astype(BF))
    lane = lax.broadcasted_iota(jnp.int32, (1, nj), 1)
    cur = pos // SEL_BLOCK
    forced = (lane == 0) | (lane == cur) | (lane == cur - 1)
    ii = lax.broadcasted_iota(jnp.int32, (nj, nj), 0)
    jj = lax.broadcasted_iota(jnp.int32, (nj, nj), 1)
    slot = lax.broadcasted_iota(jnp.int32, (nj, LANES), 1).astype(F32)
    blk_id = lax.broadcasted_iota(jnp.int32, (nj, LANES), 0)
    for h in range(C_KV_HEADS):
        imp = jnp.sum(imp_heads[h * C_GQA:(h + 1) * C_GQA], axis=0, keepdims=True)
        score = jnp.where(forced, imp + FORCE_BONUS, imp)
        score = jnp.where((lane <= cur) & (lane < n_sel), score, NEG_INF)
        s_row = jnp.broadcast_to(score, (nj, nj))
        s_col = jnp.sum(jnp.where(ii == jj, s_row, 0.0), axis=-1, keepdims=True)
        ahead = (s_row > s_col) | ((s_row == s_col) & (jj < ii))
        rank = jnp.sum(jnp.where(ahead, 1.0, 0.0), axis=-1, keepdims=True)
        hit = rank == slot
        idx = jnp.sum(jnp.where(hit, blk_id, 0), axis=0, keepdims=True)
        ok = jnp.sum(jnp.where(hit & (s_col > NEG_INF), 1, 0), axis=0, keepdims=True)
        ix_ref[h] = jnp.concatenate([idx, ok, jnp.zeros((SUBLANES - 2, LANES), jnp.int32)], axis=0)


def _cmp_sample(fs, qn, gain, pos):
    DB, _, rows, _ = fs.shape
    total = pos + 1
    n_sel = -(-total // SEL_BLOCK)
    nj = -(-n_sel // LANES) * LANES
    q3 = qn.reshape(DB, C_HEADS, HEAD_DIM)
    return pl.pallas_call(
        functools.partial(_cmp_sample_kernel, pos=pos, n_sel=n_sel, nj=nj), grid=(DB,),
        in_specs=[pl.BlockSpec((None, 4, rows, HEAD_DIM), lambda b: (b, 0, 0, 0)),
                  pl.BlockSpec((None, C_HEADS, HEAD_DIM), lambda b: (b, 0, 0)),
                  pl.BlockSpec((1, HEAD_DIM), lambda b: (0, 0))],
        out_specs=[pl.BlockSpec((None, C_HEADS, HEAD_DIM), lambda b: (b, 0, 0)),
                   pl.BlockSpec((None, C_KV_HEADS, SUBLANES, LANES), lambda b: (b, 0, 0, 0))],
        out_shape=[jax.ShapeDtypeStruct((DB, C_HEADS, HEAD_DIM), F32),
                   jax.ShapeDtypeStruct((DB, C_KV_HEADS, SUBLANES, LANES), jnp.int32)],
        compiler_params=_params("arbitrary"), name="cmp_sample",
    )(fs, q3, gain)


def _sel_sample_kernel(pt_ref, ix_ref, ok_ref, q_ref, kn_ref, pj_ref, *refs, nb, pos, n_past):
    kv_refs = refs[:nb]
    o_ref, m_ref, l_ref, acc_ref = refs[nb:]
    b, h, i = pl.program_id(0), pl.program_id(1), pl.program_id(2)

    @pl.when(i == 0)
    def _():
        m_ref[...] = jnp.full(m_ref.shape, NEG_INF, F32)
        l_ref[...] = jnp.zeros(l_ref.shape, F32)
        acc_ref[...] = jnp.zeros(acc_ref.shape, F32)

    q8 = _pad_rows(q_ref[pl.ds(h * C_GQA, C_GQA), :], SUBLANES).astype(BF)
    k_new = kn_ref[pl.ds(h, 1), :]
    v_new = pj_ref[pl.ds(C_VS // HEAD_DIM + h, 1), :]
    first_row = lax.broadcasted_iota(jnp.int32, (SEL_BLOCK, HEAD_DIM), 0) == 0
    r = lax.broadcasted_iota(jnp.int32, (1, SEL_BLOCK), 1)
    head_rows = pl.ds(h, SEL_BLOCK, stride=C_KV_HEADS)
    scores, values = [], []
    for j in range(nb):
        slot = (b * C_KV_HEADS + h) * SEL_TOP + i * nb + j
        blk = ix_ref[slot]
        is_past = blk < n_past
        kt = jnp.where(is_past, kv_refs[j][0, head_rows, :], jnp.where(first_row, k_new, 0.0))
        vt = jnp.where(is_past, kv_refs[j][1, head_rows, :], jnp.where(first_row, v_new, 0.0))
        ok = (blk * SEL_BLOCK + r <= pos) & (ok_ref[slot] > 0)
        scores.append(jnp.where(ok, _dot_nt(q8, kt.astype(BF)) * ATTN_SCALE, NEG_INF))
        values.append(vt.astype(BF))
    m_old = m_ref[...]
    m_new = functools.reduce(jnp.maximum, [jnp.max(s, axis=-1, keepdims=True) for s in scores] + [m_old])
    m_safe = jnp.where(m_new == NEG_INF, 0.0, m_new)
    alpha = jnp.exp(m_old - m_safe)
    es = [jnp.exp(s - m_safe) for s in scores]
    l_ref[...] = alpha * l_ref[...] + functools.reduce(jnp.add, [jnp.sum(e, axis=-1, keepdims=True) for e in es])
    acc_ref[...] = alpha * acc_ref[...] + functools.reduce(
        jnp.add, [_dot(e.astype(BF), v) for e, v in zip(es, values)])
    m_ref[...] = m_new

    @pl.when(i == pl.num_programs(2) - 1)
    def _():
        o_ref[pl.ds(h * C_GQA, C_GQA), :] = (acc_ref[...] / l_ref[...])[:C_GQA]


def _sel_sample(pool, page_table, idx, ok, qr, ks_r, proj, li, pos, nb=SEL_TOP):
    DB, n_pages = page_table.shape
    page = pool.shape[3]
    bpp = page // SEL_BLOCK
    n_past = n_pages * bpp
    pool = pool.reshape(pool.shape[:3] + (page * C_KV_HEADS, HEAD_DIM))
    row3 = lambda x: x.reshape(DB, x.shape[1] // HEAD_DIM, HEAD_DIM)
    q3, k3, p3 = row3(qr), row3(ks_r), row3(proj)
    full = lambda x: pl.BlockSpec((None,) + x.shape[1:], lambda b, h, i, pt, ix, okf: (b, 0, 0))

    def bspec(j):
        def imap(b, h, i, pt, ix, okf):
            blk = jnp.clip(ix[(b * C_KV_HEADS + h) * SEL_TOP + i * nb + j], 0, n_past - 1)
            return (pt[b * n_pages + blk // bpp], li, 0, blk % bpp, 0)
        return pl.BlockSpec((None, None, 2, SEL_BLOCK * C_KV_HEADS, HEAD_DIM), imap)

    out = pl.pallas_call(
        functools.partial(_sel_sample_kernel, nb=nb, pos=pos, n_past=n_past),
        grid_spec=pltpu.PrefetchScalarGridSpec(
            num_scalar_prefetch=3, grid=(DB, C_KV_HEADS, SEL_TOP // nb),
            in_specs=[full(q3), full(k3), full(p3)] + [bspec(j) for j in range(nb)],
            out_specs=pl.BlockSpec((None, C_HEADS, HEAD_DIM), lambda b, h, i, pt, ix, okf: (b, 0, 0)),
            scratch_shapes=[pltpu.VMEM((SUBLANES, 1), F32), pltpu.VMEM((SUBLANES, 1), F32),
                            pltpu.VMEM((SUBLANES, HEAD_DIM), F32)]),
        out_shape=jax.ShapeDtypeStruct((DB, C_HEADS, HEAD_DIM), F32),
        compiler_params=_params("arbitrary", "arbitrary", "arbitrary"), name="sel_sample",
    )(page_table.reshape(-1), idx, ok, q3, k3, p3, *([pool] * nb))
    return out


def _win_sample_kernel(q_ref, kn_ref, pj_ref, kp_ref, vp_ref, oc_ref, os_ref, g_ref, z_ref, o_ref):
    h = pl.program_id(1)
    heads = pl.ds(h * C_GQA, C_GQA)
    q8 = _pad_rows(q_ref[heads, :], SUBLANES)
    k_new = kn_ref[pl.ds(h, 1), :]
    v_new = pj_ref[pl.ds(C_VW // HEAD_DIM + h, 1), :]
    head_rows = pl.ds(h, kp_ref.shape[0] // C_KV_HEADS, stride=C_KV_HEADS)
    s = _dot_nt(q8.astype(BF), kp_ref[head_rows, :].astype(BF)) * ATTN_SCALE
    s_new = jnp.sum(_bf_round(q8) * _bf_round(k_new), axis=-1, keepdims=True) * ATTN_SCALE
    m = jnp.maximum(jnp.max(s, axis=-1, keepdims=True), s_new)
    e = jnp.exp(s - m)
    e_new = jnp.exp(s_new - m)
    den = jnp.sum(e, axis=-1, keepdims=True) + e_new
    p_past = (e / den).astype(BF)
    o_win = _dot(p_past, vp_ref[head_rows, :].astype(BF)) + _bf_round(e_new / den) * _bf_round(v_new)
    g = g_ref[pl.ds(h, 1), :]
    for a in range(C_GQA):
        gate = [_sigmoid(g[:, br * C_GQA + a:br * C_GQA + a + 1]) for br in range(3)]
        row = pl.ds(h * C_GQA + a, 1)
        o = gate[0] * oc_ref[row, :] + gate[1] * os_ref[row, :] + gate[2] * o_win[a:a + 1]
        o_ref[row, :] = o * _silu(z_ref[row, :])


def _win_sample(cache, qr, kw_r, proj, o_cmp, o_sel, gates, z, li):
    DB, W = cache.shape[0], cache.shape[3]
    assert W <= C_WIN
    cache = cache.reshape(cache.shape[:3] + (W * C_KV_HEADS, HEAD_DIM))
    row3 = lambda x: x.reshape(DB, x.shape[1] // HEAD_DIM, HEAD_DIM)
    full = lambda x: pl.BlockSpec((None,) + x.shape[1:], lambda b, h: (b, 0, 0))
    ins = [row3(qr), row3(kw_r), row3(proj)]
    tail = [o_cmp, o_sel, row3(gates), row3(z)]
    cspec = lambda kv: pl.BlockSpec((None, None, None, W * C_KV_HEADS, HEAD_DIM), lambda b, h: (b, li, kv, 0, 0))
    out = pl.pallas_call(
        _win_sample_kernel, grid=(DB, C_KV_HEADS),
        in_specs=[full(x) for x in ins] + [cspec(0), cspec(1)] + [full(x) for x in tail],
        out_specs=pl.BlockSpec((None, C_HEADS, HEAD_DIM), lambda b, h: (b, 0, 0)),
        out_shape=jax.ShapeDtypeStruct((DB, C_HEADS, HEAD_DIM), F32),
        compiler_params=_params("arbitrary", "arbitrary"), name="win_sample",
    )(*ins, cache, cache, *tail)
    return out.reshape(DB, C_WIDTH)


def _ab_layer_sample(h, li, layer, p, prm, tables, cache_a):
    DB = h.shape[0]
    xn = _rms_rows(h, prm["ln_ab"], li, F32)
    proj = _ab_in_proj(xn, li, prm, tables)
    q_r = proj[:, AB_Q:AB_Q + A_GROUPS * A_WIDTH]
    k_r = proj[:, AB_K:AB_K + A_WIDTH]
    ga = _dilated_sample(q_r, k_r, proj, cache_a, li)
    vn, gb = _gmlp_sample(proj, prm["ln_sgu"], prm["ws_b"], prm["bs_b"], li)
    h1, _ = _proj_residual([ga, gb], prm["w_out_ab"], li, h, False)
    h2 = _ple(h1, h1, p, prm["w_ple_gate"], prm["w_ple"], layer)
    kv = jnp.stack([k_r.reshape(DB, 1, A_HEADS, HEAD_DIM),
                    proj[:, AB_V:AB_V + A_WIDTH].reshape(DB, 1, A_HEADS, HEAD_DIM)], axis=1)
    return h2, kv, vn.reshape(DB, 1, B_WIDTH)


def _c_layer_sample(h, li, layer, p, prm, tables, past, pos):
    DB = h.shape[0]
    xn = _rms_rows(h, prm["ln_c"], li, F32)
    qn, qr, proj, z, gates = _c_in_proj(xn, li, prm, tables)
    ks_r = proj[:, C_KS:C_KS + C_KV_WIDTH]
    kw_r = proj[:, C_KW:C_KW + C_KV_WIDTH]
    fs = _page_compress(past["cmp"], past["page_table"], prm["cw_k"], prm["cw_v"], li)
    o_cmp, picks = _cmp_sample(fs, qn, prm["kn_c"][li, 0][None], pos)
    idx = picks[:, :, 0, :SEL_TOP].reshape(-1)
    ok = picks[:, :, 1, :SEL_TOP].reshape(-1)
    o_sel = _sel_sample(past["sel"], past["page_table"], idx, ok, qr, ks_r, proj, li, pos)
    lhs = _win_sample(past["win"], qr, kw_r, proj, o_cmp, o_sel, gates, z, li)
    h1, _ = _proj_residual([lhs], prm["w_out_c"], li, h, False)
    h2 = _ple(h1, h1, p, prm["w_ple_gate"], prm["w_ple"], layer)
    kvs = (DB, 1, C_KV_HEADS, HEAD_DIM)
    col = lambda off: proj[:, off:off + C_KV_WIDTH].reshape(kvs)
    cmp_rows = jnp.stack([col(C_KC), col(C_VC)], axis=1)
    sel_rows = jnp.stack([ks_r.reshape(kvs), col(C_VS)], axis=1)
    win_rows = jnp.stack([kw_r.reshape(kvs), col(C_VW)], axis=1)
    return h2, cmp_rows, sel_rows, win_rows


def _sample_trunk(x, p, past, prm):
    DB, S, _ = x.shape
    assert S == 1
    pos = past["page_table"].shape[1] * past["cmp"].shape[3]
    assert pos % CMP_STRIDE == 0 and past["cmp"].shape[3] % SEL_BLOCK == 0
    tables = _rope_tables(DB, pos, 0)
    h = x.reshape(DB, D_MODEL)
    a_rows, b_rows, cmp_rows, sel_rows, win_rows = [], [], [], [], []
    pi = p.reshape(p.shape[0], DB, PLE_DIM)
    for i in range(p.shape[0]):
        li = i // 2
        if i % 2 == 0:
            h, kv, vb = _ab_layer_sample(h, li, i, pi, prm, tables, past["a"])
            a_rows.append(kv)
            b_rows.append(vb)
        else:
            h, c, s, w = _c_layer_sample(h, li, i, pi, prm, tables, past, pos)
            cmp_rows.append(c)
            sel_rows.append(s)
            win_rows.append(w)
    st = lambda xs: jnp.stack(xs, axis=1)
    return h.reshape(DB, S, D_MODEL), st(a_rows), st(b_rows), st(cmp_rows), st(sel_rows), st(win_rows)


def kernel(x_prompt, x_sample, cache_a_kv, cache_c_cmp_kv, cache_c_sel_kv, cache_c_win_kv, page_table,
           p_prompt, p_sample, ln_ab, w_in_ab, qn_a, kn_a, ln_sgu, ws_b, bs_b, w_out_ab,
           ln_c, w_in_c, qn_c, kn_c, cw_k, cw_v, w_out_c, w_ple, w_ple_gate):
    prm = dict(ln_ab=ln_ab, w_in_ab=w_in_ab, qn_a=qn_a, kn_a=kn_a, ln_sgu=ln_sgu, ws_b=ws_b, bs_b=bs_b,
               w_out_ab=w_out_ab, ln_c=ln_c, w_in_c=w_in_c, qn_c=qn_c, kn_c=kn_c, cw_k=cw_k, cw_v=cw_v,
               w_out_c=w_out_c, w_ple=w_ple, w_ple_gate=w_ple_gate)
    y_p, a_p, b_p, cmp_p, sel_p, win_p = _prompt_trunk(x_prompt, p_prompt, prm)
    past = dict(a=cache_a_kv, cmp=cache_c_cmp_kv, sel=cache_c_sel_kv, win=cache_c_win_kv, page_table=page_table)
    y_s, a_s, b_s, cmp_s, sel_s, win_s = _sample_trunk(x_sample, p_sample, past, prm)
    return (y_p, y_s, a_p, a_s, b_p, b_s, cmp_p, cmp_s, sel_p, sel_s, win_p, win_s)
```

```python
import functools
import math

import jax
import jax.numpy as jnp
from jax import lax
from jax.experimental import pallas as pl
from jax.experimental.pallas import tpu as pltpu

F32 = jnp.float32
BF = jnp.bfloat16

D_MODEL = 2048
HEAD_DIM = 128
ROPE_DIM = HEAD_DIM // 4
ROPE_HALF = ROPE_DIM // 2
ROPE_THETA = 500000.0
NORM_EPS = 1e-6
Q_BLOCK = 128
PLE_DIM = 256
ATTN_SCALE = HEAD_DIM ** -0.5
SCALE_LOG2E = ATTN_SCALE * math.log2(math.e)
TINY = 1e-30
A_HEADS = D_MODEL // (2 * HEAD_DIM)
A_PATTERNS = ((128, 1), (512, 4), (2048, 16))
A_GROUPS = len(A_PATTERNS)
A_WIDTH = A_HEADS * HEAD_DIM
B_CHUNK = 128
B_WIDTH = D_MODEL - A_WIDTH
B_GROUPS = 8
C_HEADS = D_MODEL // HEAD_DIM
C_KV_HEADS = 4
C_GQA = C_HEADS // C_KV_HEADS
C_WIDTH = C_HEADS * HEAD_DIM
C_KV_WIDTH = C_KV_HEADS * HEAD_DIM
CMP_LEN = 32
CMP_STRIDE = 16
SEL_BLOCK = 64
SEL_TOP = 16
C_WIN = 512
FORCE_BONUS = 1000.0
NEG_INF = float("-inf")

LANES = 128
SUBLANES = 8
MM_TILE_M = 1024
MM_TILE_N = 512
MM_TILE_N_WIDE = 1024
ROW_TILE = 512
EMIT_TILE = 256
SAMPLE_ROWS = 16
VMEM_LIMIT = 56 * 1024 * 1024
DILATED_BATCH = 8

AB_Q, AB_K, AB_V, AB_ZA, AB_UB, AB_VB, AB_ZB = 0, 3072, 4096, 5120, 6144, 7168, 8192
C_W_Q, C_W_KV, C_G, C_Z = 0, 2048, 5120, 5168
C_KC, C_VC, C_KS, C_VS, C_KW, C_VW = 0, 512, 1024, 1536, 2048, 2560


def _params(*sem):
    return pltpu.CompilerParams(dimension_semantics=sem, vmem_limit_bytes=VMEM_LIMIT)


def _dot(a, b):
    return jnp.dot(a, b, preferred_element_type=F32)


def _dot_nt(a, b):
    return lax.dot_general(a, b, (((1,), (1,)), ((), ())), preferred_element_type=F32)


def _sigmoid(x):
    return jax.nn.sigmoid(x)


def _silu(x):
    return x * _sigmoid(x)


def _rope_table_kernel(cos_ref, sa_ref, sb_ref, *, start, step):
    rows = cos_ref.shape[0]
    r = lax.broadcasted_iota(jnp.int32, (rows, HEAD_DIM), 0)
    lane = lax.broadcasted_iota(jnp.int32, (rows, HEAD_DIM), 1)
    pos = (start + r * step).astype(F32)
    j = (lane & (ROPE_HALF - 1)).astype(F32)
    inv = jnp.exp(-math.log(ROPE_THETA) * j / ROPE_HALF)
    ang = pos * inv
    c, s = jnp.cos(ang), jnp.sin(ang)
    in_rope = lane < ROPE_DIM
    cos_ref[...] = jnp.where(in_rope, c, 1.0)
    sa_ref[...] = jnp.where(in_rope & (lane >= ROPE_HALF), s, 0.0)
    sb_ref[...] = jnp.where(lane < ROPE_HALF, -s, 0.0)


def _rope_tables(rows, start, step):
    shp = jax.ShapeDtypeStruct((rows, HEAD_DIM), F32)
    return pl.pallas_call(
        functools.partial(_rope_table_kernel, start=start, step=step),
        out_shape=(shp, shp, shp), name="rope_tables")()


def _rms_kernel(x_ref, g_ref, o_ref):
    x = x_ref[...]
    y = x * lax.rsqrt(jnp.mean(x * x, axis=-1, keepdims=True) + NORM_EPS)
    o_ref[...] = (y * g_ref[...]).astype(o_ref.dtype)


def _rms_rows(x, gains, li, out_dtype):
    M, D = x.shape
    tm = min(M, ROW_TILE)
    return pl.pallas_call(
        _rms_kernel, grid=(M // tm,),
        in_specs=[pl.BlockSpec((tm, D), lambda i: (i, 0)),
                  pl.BlockSpec((None, 1, D), lambda i: (li, 0, 0))],
        out_specs=pl.BlockSpec((tm, D), lambda i: (i, 0)),
        out_shape=jax.ShapeDtypeStruct((M, D), out_dtype),
        compiler_params=_params("arbitrary"), name="rms_rows",
    )(x, gains.reshape(gains.shape[0], 1, D))


TILE_RAW, TILE_NORM, TILE_HALF_NORM = 0, 1, 2


def _mm_norm_kernel(x_ref, w_ref, g_ref, cos_ref, sa_ref, sb_ref, *rest, kinds, two_out):
    o_ref, wb_ref = rest[0], rest[-1]
    n = pl.program_id(0)

    @pl.when(pl.program_id(1) == 0)
    def _():
        wb_ref[...] = w_ref[...].astype(BF)

    acc = _dot(x_ref[...].astype(BF), wb_ref[...])
    heads = acc.shape[1] // HEAD_DIM

    def store(kind):
        normed = {TILE_RAW: 0, TILE_NORM: heads, TILE_HALF_NORM: heads // 2}[kind]
        for j in range(heads):
            cols = slice(j * HEAD_DIM, (j + 1) * HEAD_DIM)
            a = acc[:, cols]
            if j >= normed:
                o_ref[:, cols] = a
                continue
            y = a * lax.rsqrt(jnp.mean(a * a, axis=-1, keepdims=True) + NORM_EPS) * g_ref[...]
            if two_out:
                rest[1][:, cols] = y
            up = pltpu.roll(y, ROPE_HALF, 1)
            down = pltpu.roll(y, HEAD_DIM - ROPE_HALF, 1)
            o_ref[:, cols] = y * cos_ref[...] + up * sa_ref[...] + down * sb_ref[...]

    distinct = sorted(set(kinds))
    if len(distinct) == 1:
        store(distinct[0])
    else:
        for kind in distinct:
            hit = functools.reduce(jnp.logical_or, [n == t for t, k in enumerate(kinds) if k == kind])
            pl.when(hit)(functools.partial(store, kind))


def _matmul_norm(x, w, li, col0, kinds, gains, gain_of_tile, tables, two_out=False):
    M, K = x.shape
    tm, tn = min(M, MM_TILE_M), MM_TILE_N_WIDE
    assert col0 % tn == 0 and (not two_out or set(kinds) == {TILE_NORM})
    ncols = tn * len(kinds)
    cos, sa, sb = tables
    nt = cos.shape[0] // tm

    def gain_index(n, m):
        idx = 0
        for t, g in enumerate(gain_of_tile):
            idx = jnp.where(n == t, g, idx)
        return (idx, 0, 0)

    tspec = pl.BlockSpec((tm, HEAD_DIM), lambda n, m: (m % nt, 0))
    ospec = pl.BlockSpec((tm, tn), lambda n, m: (m, n))
    n_out = 2 if two_out else 1
    outs = pl.pallas_call(
        functools.partial(_mm_norm_kernel, kinds=tuple(kinds), two_out=two_out),
        grid=(len(kinds), M // tm),
        in_specs=[pl.BlockSpec((tm, K), lambda n, m: (m, 0)),
                  pl.BlockSpec((None, K, tn), lambda n, m: (li, 0, n + col0 // tn)),
                  pl.BlockSpec((None, 1, HEAD_DIM), gain_index), tspec, tspec, tspec],
        out_specs=[ospec] * n_out, out_shape=[jax.ShapeDtypeStruct((M, ncols), F32)] * n_out,
        scratch_shapes=[pltpu.VMEM((K, tn), BF)],
        compiler_params=_params("arbitrary", "arbitrary"), name="matmul_norm",
    )(x, w, gains, cos, sa, sb)
    return (outs[1], outs[0]) if two_out else outs[0]


def _mm_kernel(x_ref, w_ref, o_ref, wb_ref):
    @pl.when(pl.program_id(1) == 0)
    def _():
        wb_ref[...] = w_ref[...].astype(BF)

    o_ref[...] = _dot(x_ref[...].astype(BF), wb_ref[...])


def _matmul(x, w, li, col0, ncols):
    M, K = x.shape
    tm = min(M, MM_TILE_M)
    tn = MM_TILE_N_WIDE if ncols % MM_TILE_N_WIDE == 0 and col0 % MM_TILE_N_WIDE == 0 else MM_TILE_N
    if li is None:
        wspec = pl.BlockSpec((K, tn), lambda n, m: (0, n + col0 // tn))
    else:
        wspec = pl.BlockSpec((None, K, tn), lambda n, m: (li, 0, n + col0 // tn))
    return pl.pallas_call(
        _mm_kernel, grid=(ncols // tn, M // tm),
        in_specs=[pl.BlockSpec((tm, K), lambda n, m: (m, 0)), wspec],
        out_specs=pl.BlockSpec((tm, tn), lambda n, m: (m, n)),
        out_shape=jax.ShapeDtypeStruct((M, ncols), F32),
        scratch_shapes=[pltpu.VMEM((K, tn), BF)],
        compiler_params=_params("arbitrary", "arbitrary"), name="matmul",
    )(x, w)


def _proj_res_kernel(*refs, ks, with_bf):
    n = len(ks)
    lhs = refs[:n]
    w_ref, res_ref, o_ref = refs[n:n + 3]
    wb_ref = refs[-1]

    @pl.when(pl.program_id(1) == 0)
    def _():
        wb_ref[...] = w_ref[...].astype(BF)

    acc = res_ref[...]
    off = 0
    for r, k in zip(lhs, ks):
        acc = acc + _dot(r[...].astype(BF), wb_ref[off:off + k, :])
        off += k
    o_ref[...] = acc
    if with_bf:
        refs[n + 3][...] = acc.astype(BF)


def _proj_residual(lhs_list, w, li, res, with_bf):
    M, N = res.shape
    ks = tuple(a.shape[1] for a in lhs_list)
    K = sum(ks)
    tm, tn = min(M, MM_TILE_M), MM_TILE_N_WIDE
    ospec = pl.BlockSpec((tm, tn), lambda n, m: (m, n))
    out_shape = [jax.ShapeDtypeStruct((M, N), F32)]
    if with_bf:
        out_shape.append(jax.ShapeDtypeStruct((M, N), BF))
    outs = pl.pallas_call(
        functools.partial(_proj_res_kernel, ks=ks, with_bf=with_bf),
        grid=(N // tn, M // tm),
        in_specs=[pl.BlockSpec((tm, k), lambda n, m: (m, 0)) for k in ks]
        + [pl.BlockSpec((None, K, tn), lambda n, m: (li, 0, n)), ospec],
        out_specs=[ospec] * len(out_shape), out_shape=out_shape,
        scratch_shapes=[pltpu.VMEM((K, tn), BF)],
        compiler_params=_params("arbitrary", "arbitrary"), name="proj_residual",
    )(*lhs_list, w, res)
    return outs if with_bf else (outs[0], outs[0])


def _ple_kernel(hl_ref, wg_ref, p_ref, wp_ref, h_ref, o_ref, wgb_ref, wpb_ref):
    @pl.when(pl.program_id(1) == 0)
    def _():
        wgb_ref[...] = wg_ref[...].astype(BF)
        wpb_ref[...] = wp_ref[...].astype(BF)

    gate = _sigmoid(_dot(hl_ref[...].astype(BF), wgb_ref[...]))
    pp = _dot(p_ref[...].astype(BF), wpb_ref[...])
    o_ref[...] = h_ref[...] + gate * pp


def _ple(h, h_lhs, p, w_gate, w_ple, layer):
    M, N = h.shape
    K, KP = h_lhs.shape[1], p.shape[2]
    tm, tn = min(M, MM_TILE_M), MM_TILE_N_WIDE
    ospec = pl.BlockSpec((tm, tn), lambda n, m: (m, n))
    return pl.pallas_call(
        _ple_kernel, grid=(N // tn, M // tm),
        in_specs=[pl.BlockSpec((tm, K), lambda n, m: (m, 0)),
                  pl.BlockSpec((None, K, tn), lambda n, m: (layer, 0, n)),
                  pl.BlockSpec((None, tm, KP), lambda n, m: (layer, m, 0)),
                  pl.BlockSpec((None, KP, tn), lambda n, m: (layer, 0, n)), ospec],
        out_specs=ospec, out_shape=jax.ShapeDtypeStruct((M, N), F32),
        scratch_shapes=[pltpu.VMEM((K, tn), BF), pltpu.VMEM((KP, tn), BF)],
        compiler_params=_params("arbitrary", "arbitrary"), name="ple",
    )(h_lhs, w_gate, p, w_ple, h)


def _dilated_kernel(q0_ref, q1_ref, q2_ref, k_ref, v_ref, z_ref, o_ref, num_ref, m_ref, l_ref, *, S):
    blk = Q_BLOCK
    row = lax.broadcasted_iota(jnp.int32, (blk, blk), 0)
    col = lax.broadcasted_iota(jnp.int32, (blk, blk), 1)
    q_refs = (q0_ref, q1_ref, q2_ref)
    for g, (window, d) in enumerate(A_PATTERNS):
        assert window // d == blk
        nblk = S // d // blk
        q_ref = q_refs[g]

        def batch(ub, carry, g=g, d=d, nblk=nblk, q_ref=q_ref):
            units = []
            for j in range(DILATED_BATCH):
                u = ub * DILATED_BATCH + j
                c = u // nblk
                i = u - c * nblk
                start = c + d * blk * i
                rows = pl.ds(start, blk, stride=d) if d > 1 else pl.ds(pl.multiple_of(start, blk), blk)
                q = q_ref[rows, :].astype(BF)
                s_c = _dot_nt(q, k_ref[rows, :].astype(BF))
                s_p, prows = None, None
                if nblk > 1:
                    pstart = jnp.maximum(start - d * blk, c)
                    prows = pl.ds(pstart, blk, stride=d) if d > 1 else pl.ds(pl.multiple_of(pstart, blk), blk)
                    s_p = _dot_nt(q, k_ref[prows, :].astype(BF))
                units.append((i, rows, prows, s_c, s_p))
            probs = []
            for i, rows, prows, s_c, s_p in units:
                s_c = jnp.where(col <= row, s_c * SCALE_LOG2E, NEG_INF)
                if s_p is not None:
                    s_p = jnp.where((col >= row) & (i > 0), s_p * SCALE_LOG2E, NEG_INF)
                    m = jnp.max(jnp.maximum(s_c, s_p), axis=-1, keepdims=True)
                    e_c, e_p = jnp.exp2(s_c - m), jnp.exp2(s_p - m)
                    l = jnp.sum(e_c + e_p, axis=-1, keepdims=True)
                    probs.append((m, l, e_c.astype(BF), e_p.astype(BF)))
                else:
                    m = jnp.max(s_c, axis=-1, keepdims=True)
                    e_c = jnp.exp2(s_c - m)
                    probs.append((m, jnp.sum(e_c, axis=-1, keepdims=True), e_c.astype(BF), None))
            for (i, rows, prows, _, _), (m, l, e_c, e_p) in zip(units, probs):
                num = _dot(e_c, v_ref[rows, :].astype(BF))
                if e_p is not None:
                    num = num + _dot(e_p, v_ref[prows, :].astype(BF))
                num_ref[g, rows, :] = num
                m_ref[g, rows, :] = jnp.broadcast_to(m, (blk, HEAD_DIM))
                l_ref[g, rows, :] = jnp.broadcast_to(l, (blk, HEAD_DIM))
            return carry

        assert (d * nblk) % DILATED_BATCH == 0
        lax.fori_loop(0, d * nblk // DILATED_BATCH, batch, 0)

    def merge(i, carry):
        rows = pl.ds(pl.multiple_of(i * blk, blk), blk)
        ms = [m_ref[g, rows, :] for g in range(A_GROUPS)]
        m_all = jnp.maximum(jnp.maximum(ms[0], ms[1]), ms[2])
        ws = [jnp.exp2(m - m_all) for m in ms]
        num = ws[0] * num_ref[0, rows, :] + ws[1] * num_ref[1, rows, :] + ws[2] * num_ref[2, rows, :]
        den = ws[0] * l_ref[0, rows, :] + ws[1] * l_ref[1, rows, :] + ws[2] * l_ref[2, rows, :]
        o_ref[rows, :] = ((num / den) * _silu(z_ref[rows, :])).astype(o_ref.dtype)
        return carry

    lax.fori_loop(0, S // blk, merge, 0, unroll=2)


def _dilated_prompt(proj, B, S):
    H = A_HEADS
    hb = lambda off: off // HEAD_DIM

    def col_spec(base):
        return pl.BlockSpec((S, HEAD_DIM), lambda b, h: (b, base + h))

    return pl.pallas_call(
        functools.partial(_dilated_kernel, S=S), grid=(B, H),
        in_specs=[col_spec(hb(AB_Q)), col_spec(hb(AB_Q) + H), col_spec(hb(AB_Q) + 2 * H), col_spec(hb(AB_K)),
                  col_spec(hb(AB_V)), col_spec(hb(AB_ZA))],
        out_specs=col_spec(0),
        out_shape=jax.ShapeDtypeStruct((B * S, A_WIDTH), BF),
        scratch_shapes=[pltpu.VMEM((A_GROUPS, S, HEAD_DIM), F32)] * 3,
        compiler_params=_params("arbitrary", "arbitrary"), name="dilated_prompt",
    )(*([proj] * 6))


def _layer_norm(v, g):
    xc = v - jnp.mean(v, axis=-1, keepdims=True)
    return xc * lax.rsqrt(jnp.mean(xc * xc, axis=-1, keepdims=True) + NORM_EPS) * g


def _gmlp_kernel(u_ref, v_ref, z_ref, g_ref, ws_ref, bst_ref, vn_ref, o_ref):
    vn = _layer_norm(v_ref[...], g_ref[...])
    vn_ref[...] = vn
    row = lax.broadcasted_iota(jnp.int32, (B_CHUNK, B_CHUNK), 0)
    col = lax.broadcasted_iota(jnp.int32, (B_CHUNK, B_CHUNK), 1)
    for g in range(B_GROUPS):
        cols = slice(g * LANES, (g + 1) * LANES)
        w = jnp.where(row >= col, ws_ref[g], 0.0).astype(BF)
        mixed = _dot(w, vn[:, cols].astype(BF)) + bst_ref[:, g:g + 1]
        o_ref[:, cols] = ((u_ref[:, cols] * mixed) * _silu(z_ref[:, cols])).astype(o_ref.dtype)


def _gmlp_prompt(proj, ln_sgu, ws_b, bs_b, li):
    M = proj.shape[0]
    wb = B_WIDTH
    blk = lambda off: pl.BlockSpec((B_CHUNK, wb), lambda i: (i, off // wb))
    bst = jnp.swapaxes(bs_b, 1, 2)
    return pl.pallas_call(
        _gmlp_kernel, grid=(M // B_CHUNK,),
        in_specs=[blk(AB_UB), blk(AB_VB), blk(AB_ZB),
                  pl.BlockSpec((None, 1, wb), lambda i: (li, 0, 0)),
                  pl.BlockSpec((None, B_GROUPS, B_CHUNK, B_CHUNK), lambda i: (li, 0, 0, 0)),
                  pl.BlockSpec((None, B_CHUNK, B_GROUPS), lambda i: (li, 0, 0))],
        out_specs=[pl.BlockSpec((B_CHUNK, wb), lambda i: (i, 0))] * 2,
        out_shape=[jax.ShapeDtypeStruct((M, wb), F32), jax.ShapeDtypeStruct((M, wb), BF)],
        compiler_params=_params("arbitrary"), name="gmlp_prompt",
    )(proj, proj, proj, ln_sgu.reshape(-1, 1, wb), ws_b, bst)


def _compress_rows(x_ref, w_ref, wbase, h, nseg):
    first = jnp.zeros((nseg, HEAD_DIM), F32)
    second = jnp.zeros((nseg, HEAD_DIM), F32)
    for r in range(CMP_STRIDE):
        rows = x_ref[pl.ds(r, nseg, stride=CMP_STRIDE), :]
        first = first + rows * w_ref[wbase + r * C_KV_HEADS + h]
        second = second + rows * w_ref[wbase + (CMP_STRIDE + r) * C_KV_HEADS + h]
    c = first + pltpu.roll(second, nseg - 1, 0)
    seg = lax.broadcasted_iota(jnp.int32, (nseg, HEAD_DIM), 0)
    return jnp.where(seg < nseg - 1, c, 0.0)


def _compress_kernel(cwk_ref, cwv_ref, kc_ref, vc_ref, g_ref, ko_ref, vo_ref, *, li, nseg):
    h = pl.program_id(1)
    wbase = li * CMP_LEN * C_KV_HEADS
    kc = _compress_rows(kc_ref, cwk_ref, wbase, h, nseg)
    ko_ref[...] = kc * lax.rsqrt(jnp.mean(kc * kc, axis=-1, keepdims=True) + NORM_EPS) * g_ref[...]
    vo_ref[...] = _compress_rows(vc_ref, cwv_ref, wbase, h, nseg)


def _compress_prompt(proj, cw_k, cw_v, gain, li, B, S):
    nseg = S // CMP_STRIDE
    hb = lambda off: off // HEAD_DIM
    smem = pl.BlockSpec(memory_space=pltpu.SMEM)
    ospec = pl.BlockSpec((None, None, nseg, HEAD_DIM), lambda b, h: (b, h, 0, 0))
    oshape = jax.ShapeDtypeStruct((B, C_KV_HEADS, nseg, HEAD_DIM), F32)
    return pl.pallas_call(
        functools.partial(_compress_kernel, li=li, nseg=nseg), grid=(B, C_KV_HEADS),
        in_specs=[smem, smem,
                  pl.BlockSpec((S, HEAD_DIM), lambda b, h: (b, hb(C_KC) + h)),
                  pl.BlockSpec((S, HEAD_DIM), lambda b, h: (b, hb(C_VC) + h)),
                  pl.BlockSpec((1, HEAD_DIM), lambda b, h: (0, 0))],
        out_specs=[ospec, ospec], out_shape=[oshape, oshape],
        compiler_params=_params("arbitrary", "arbitrary"), name="compress_prompt",
    )(cw_k.reshape(-1), cw_v.reshape(-1), proj, proj, gain)


def _cmp_to_sel(nrows, ncols, rows_per_seg):
    n = lax.broadcasted_iota(jnp.int32, (nrows, ncols), 0) // rows_per_seg
    j = lax.broadcasted_iota(jnp.int32, (nrows, ncols), 1)
    shared = (jnp.minimum(n * CMP_STRIDE + CMP_LEN, (j + 1) * SEL_BLOCK)
              - jnp.maximum(n * CMP_STRIDE, j * SEL_BLOCK))
    return jnp.maximum(shared, 0).astype(F32) / CMP_LEN


def _cmp_to_sel_t(nrows, ncols, seg_mask):
    j = lax.broadcasted_iota(jnp.int32, (nrows, ncols), 0)
    n = lax.broadcasted_iota(jnp.int32, (nrows, ncols), 1) & seg_mask
    shared = (jnp.minimum(n * CMP_STRIDE + CMP_LEN, (j + 1) * SEL_BLOCK)
              - jnp.maximum(n * CMP_STRIDE, j * SEL_BLOCK))
    return jnp.maximum(shared, 0).astype(F32) / CMP_LEN


def _rank_select_t(score, rows):
    sc = score[:rows]
    blk = lax.broadcasted_iota(jnp.int32, sc.shape, 0)
    rank = jnp.zeros(sc.shape, F32)
    for jp in range(rows):
        other = sc[jp:jp + 1, :]
        ahead = (other > sc) | ((other == sc) & (blk > jp))
        rank = rank + jnp.where(ahead, 1.0, 0.0)
    picked = jnp.where((rank < SEL_TOP) & (sc > NEG_INF), 1.0, 0.0)
    return jnp.concatenate([picked, jnp.zeros((score.shape[0] - rows, score.shape[1]), F32)], axis=0)


def _flash_loop_t(q4, k_ref, vt_ref, n_tiles, tk, bias_fn):
    reps = q4.shape[0] // LANES
    q_heads = [q4[g * LANES:(g + 1) * LANES] for g in range(reps)]

    def body(t, state):
        k0 = pl.multiple_of(t * tk, tk)
        kt = k_ref[pl.ds(k0, tk), :]
        vt = vt_ref[t]
        bias = bias_fn(k0)
        scores = [_dot_nt(kt, q_heads[g]) for g in range(reps)]
        mid = []
        for g in range(reps):
            m, l, _ = state[g]
            s = scores[g] * SCALE_LOG2E + bias
            m_new = jnp.maximum(m, jnp.max(s, axis=0, keepdims=True))
            m_safe = jnp.where(m_new == NEG_INF, 0.0, m_new)
            alpha = jnp.exp2(m - m_safe)
            e = jnp.exp2(s - m_safe)
            mid.append((m_new, alpha * l + jnp.sum(e, axis=0, keepdims=True), alpha, e.astype(BF)))
        return tuple((m_new, l, alpha * state[g][2] + _dot(vt, e))
                     for g, (m_new, l, alpha, e) in enumerate(mid))

    one = (jnp.full((1, LANES), NEG_INF, F32), jnp.zeros((1, LANES), F32), jnp.zeros((HEAD_DIM, LANES), F32))
    res = lax.fori_loop(0, n_tiles, body, (one,) * reps)
    return jnp.concatenate([acc / l for _, l, acc in res], axis=1)


def _stack_heads(ref):
    return jnp.concatenate([ref[:, g * HEAD_DIM:(g + 1) * HEAD_DIM] for g in range(C_GQA)], axis=0)


def _nsa_kernel(qn_ref, qr_ref, kcmp_ref, vcmp_ref, ks_ref, vs_ref, kw_ref, vw_ref, g_ref, z_ref, o_ref,
                ksb_ref, kwb_ref, vst_ref, vwt_ref, vct_ref, exp_ref, c2s_ref, *, S, n_sel, sel_tk):
    qb = Q_BLOCK
    qi = pl.program_id(2)
    q0 = qi * qb
    nseg = kcmp_ref.shape[0]
    cols4 = C_GQA * qb
    win_keys = C_WIN + qb

    @pl.when(qi == 0)
    def _():
        ksb_ref[...] = ks_ref[...].astype(BF)
        kwb_ref[...] = kw_ref[...].astype(BF)
        for t in range(S // sel_tk):
            vst_ref[t] = vs_ref[t * sel_tk:(t + 1) * sel_tk, :].T.astype(BF)
        for t in range(S // qb):
            vwt_ref[t] = vw_ref[t * qb:(t + 1) * qb, :].T.astype(BF)
        vct_ref[...] = vcmp_ref[...].T.astype(BF)
        key_blk = lax.broadcasted_iota(jnp.int32, (S, LANES), 0) // SEL_BLOCK
        exp_ref[...] = jnp.where(key_blk == lax.broadcasted_iota(jnp.int32, (S, LANES), 1), 1.0, 0.0).astype(BF)
        c2s_ref[...] = _cmp_to_sel_t(LANES, C_GQA * nseg, nseg - 1).astype(BF)

    p_q = q0 + lax.broadcasted_iota(jnp.int32, (1, qb), 1)
    p_4 = q0 + (lax.broadcasted_iota(jnp.int32, (1, cols4), 1) & (qb - 1))

    qn4 = _stack_heads(qn_ref).astype(BF)
    qr4 = _stack_heads(qr_ref).astype(BF)
    w0 = pl.multiple_of(jnp.maximum(q0 - C_WIN, 0), qb)
    s_cmp = _dot_nt(kcmp_ref[...].astype(BF), qn4)
    k_win = kwb_ref[pl.ds(w0, win_keys), :]
    s_win = [_dot_nt(k_win, qr4[g * qb:(g + 1) * qb]) for g in range(C_GQA)]

    seg = lax.broadcasted_iota(jnp.int32, (nseg, cols4), 0)
    vis = seg * CMP_STRIDE + (CMP_LEN - 1) <= p_4
    s = jnp.where(vis, s_cmp * SCALE_LOG2E, NEG_INF)
    m = jnp.max(s, axis=0, keepdims=True)
    m = jnp.where(m == NEG_INF, 0.0, m)
    e = jnp.exp2(s - m)
    prob_b = (e / jnp.maximum(jnp.sum(e, axis=0, keepdims=True), TINY)).astype(BF)
    o_cmp = _dot(vct_ref[...], prob_b)

    prob_stack = jnp.concatenate([prob_b[:, g * qb:(g + 1) * qb] for g in range(C_GQA)], axis=0)
    imp = _dot(c2s_ref[...], prob_stack)
    blk = lax.broadcasted_iota(jnp.int32, (LANES, qb), 0)
    cur = p_q // SEL_BLOCK
    forced = (blk == 0) | (blk == cur) | (blk == cur - 1)
    score = jnp.where(forced, imp + FORCE_BONUS, imp)
    score = jnp.where(blk <= cur, score, NEG_INF)
    sel = _rank_select_t(score, -(-n_sel // SUBLANES) * SUBLANES).astype(BF)

    kpos = w0 + lax.broadcasted_iota(jnp.int32, (win_keys, 1), 0)
    bias = jnp.where((kpos <= p_q) & (kpos >= p_q - C_WIN), 0.0, NEG_INF)
    t0 = w0 // qb
    win_e, win_den = [], []
    for g in range(C_GQA):
        s = s_win[g] * SCALE_LOG2E + bias
        e = jnp.exp2(s - jnp.max(s, axis=0, keepdims=True))
        win_den.append(jnp.sum(e, axis=0, keepdims=True))
        win_e.append(e.astype(BF))
    o_win = []
    for g in range(C_GQA):
        acc = _dot(vwt_ref[t0], win_e[g][:qb])
        for i in range(1, win_keys // qb):
            acc = acc + _dot(vwt_ref[t0 + i], win_e[g][i * qb:(i + 1) * qb])
        o_win.append(acc / win_den[g])
    o_win = jnp.concatenate(o_win, axis=1)

    def sel_bias(k0):
        chosen = _dot(exp_ref[pl.ds(k0, sel_tk), :], sel)
        kpos = k0 + lax.broadcasted_iota(jnp.int32, (sel_tk, 1), 0)
        return jnp.where((chosen > 0.5) & (kpos <= p_q), 0.0, NEG_INF)

    o_sel = _flash_loop_t(qr4, ksb_ref, vst_ref, (q0 + qb + sel_tk - 1) // sel_tk, sel_tk, sel_bias)

    gates_t = _sigmoid(g_ref[...].T)
    for g in range(C_GQA):
        cs = slice(g * HEAD_DIM, (g + 1) * HEAD_DIM)
        o = (gates_t[g:g + 1] * o_cmp[:, cs] + gates_t[C_GQA + g:C_GQA + g + 1] * o_sel[:, cs]
             + gates_t[2 * C_GQA + g:2 * C_GQA + g + 1] * o_win[:, cs])
        o_ref[:, cs] = (o.T * _silu(z_ref[:, cs])).astype(o_ref.dtype)


def _nsa_prompt(qn, qr, kcmp, vcmp, kv, gates, z, B, S):
    nq = S // Q_BLOCK
    nseg = kcmp.shape[2]
    gw = C_GQA * HEAD_DIM
    sel_tk = 512
    assert S >= C_WIN + Q_BLOCK and S % sel_tk == 0
    hb = lambda off: off // HEAD_DIM
    qspec = pl.BlockSpec((Q_BLOCK, gw), lambda b, h, i: (b * nq + i, h))
    cspec = pl.BlockSpec((None, None, nseg, HEAD_DIM), lambda b, h, i: (b, h, 0, 0))

    def kv_spec(base):
        return pl.BlockSpec((S, HEAD_DIM), lambda b, h, i: (b, base + h))

    return pl.pallas_call(
        functools.partial(_nsa_kernel, S=S, n_sel=-(-S // SEL_BLOCK), sel_tk=sel_tk),
        grid=(B, C_KV_HEADS, nq),
        in_specs=[qspec, qspec, cspec, cspec, kv_spec(hb(C_KS)), kv_spec(hb(C_VS)), kv_spec(hb(C_KW)),
                  kv_spec(hb(C_VW)),
                  pl.BlockSpec((Q_BLOCK, LANES), lambda b, h, i: (b * nq + i, h)), qspec],
        out_specs=qspec, out_shape=jax.ShapeDtypeStruct((B * S, C_WIDTH), BF),
        scratch_shapes=[pltpu.VMEM((S, HEAD_DIM), BF), pltpu.VMEM((S, HEAD_DIM), BF),
                        pltpu.VMEM((S // sel_tk, HEAD_DIM, sel_tk), BF),
                        pltpu.VMEM((S // Q_BLOCK, HEAD_DIM, Q_BLOCK), BF),
                        pltpu.VMEM((HEAD_DIM, nseg), BF), pltpu.VMEM((S, LANES), BF),
                        pltpu.VMEM((LANES, C_GQA * nseg), BF)],
        compiler_params=_params("arbitrary", "arbitrary", "arbitrary"), name="nsa_prompt",
    )(qn, qr, kcmp, vcmp, kv, kv, kv, kv, gates, z)


def _gate_weights(w_in_c, li):
    wg = w_in_c[li, :, C_G:C_Z].reshape(D_MODEL, 3, C_KV_HEADS, C_GQA)
    wg = jnp.transpose(wg, (0, 2, 1, 3)).reshape(D_MODEL, C_KV_HEADS, 3 * C_GQA)
    wg = jnp.pad(wg, ((0, 0), (0, 0), (0, LANES - 3 * C_GQA)))
    return wg.reshape(D_MODEL, C_KV_HEADS * LANES)


def _emit_kernel(*refs, heads, n_src, n_layers, tm):
    o_ref = refs[n_layers * n_src]
    layer = pl.program_id(1)
    for li in range(n_layers):
        def copy(li=li):
            for a in range(n_src):
                src = refs[li * n_src + a]
                for h in range(heads):
                    o_ref[a, pl.ds(h, tm, stride=heads), :] = src[:, h * HEAD_DIM:(h + 1) * HEAD_DIM]

        pl.when(layer == li)(copy)


def _emit_rows(layer_srcs, heads, B, S, keep):
    width = heads * HEAD_DIM
    tm = EMIT_TILE
    spb, row0 = S // tm, (S - keep) // tm
    n_layers, n_src = len(layer_srcs), len(layer_srcs[0])

    def src_spec(li, off):
        return pl.BlockSpec((tm, width),
                            lambda b, l, i: (jnp.where(l == li, b * spb + row0 + i, 0), off // width))

    return pl.pallas_call(
        functools.partial(_emit_kernel, heads=heads, n_src=n_src, n_layers=n_layers, tm=tm),
        grid=(B, n_layers, keep // tm),
        in_specs=[src_spec(li, off) for li, srcs in enumerate(layer_srcs) for _, off in srcs],
        out_specs=pl.BlockSpec((None, None, n_src, tm * heads, HEAD_DIM), lambda b, l, i: (b, l, 0, i, 0)),
        out_shape=jax.ShapeDtypeStruct((B, n_layers, n_src, keep * heads, HEAD_DIM), F32),
        compiler_params=_params("arbitrary", "arbitrary", "arbitrary"), name="emit_rows",
    )(*[a for srcs in layer_srcs for a, _ in srcs])


def _ab_in_proj(xn, li, prm, tables):
    w = prm["w_in_ab"]
    tiles = w.shape[2] // MM_TILE_N_WIDE
    q_tiles, k_tiles = A_GROUPS * A_WIDTH // MM_TILE_N_WIDE, A_WIDTH // MM_TILE_N_WIDE
    kinds = [TILE_NORM] * (q_tiles + k_tiles) + [TILE_RAW] * (tiles - q_tiles - k_tiles)
    gain_of_tile = [0] * q_tiles + [1] * k_tiles + [0] * (tiles - q_tiles - k_tiles)
    gains = jnp.stack([prm["qn_a"][li], prm["kn_a"][li]])[:, None]
    return _matmul_norm(xn, w, li, 0, kinds, gains, gain_of_tile, tables)


def _c_in_proj(xn, li, prm, tables):
    w = prm["w_in_c"]
    q_tiles = C_WIDTH // MM_TILE_N_WIDE
    qn, qr = _matmul_norm(xn, w, li, C_W_Q, [TILE_NORM] * q_tiles, prm["qn_c"][li][None, None],
                          [0] * q_tiles, tables, two_out=True)
    kv = _matmul_norm(xn, w, li, C_W_KV, [TILE_RAW, TILE_HALF_NORM, TILE_HALF_NORM],
                      prm["kn_c"][li, 1:3][:, None], [0, 0, 1], tables)
    z = _matmul(xn, w[li, :, C_Z:], None, 0, C_WIDTH)
    gates = _matmul(xn, _gate_weights(w, li), None, 0, C_KV_HEADS * LANES)
    return qn, qr, kv, z, gates


def _ab_layer_prompt(h, li, layer, p, prm, tables, B, S):
    xn = _rms_rows(h, prm["ln_ab"], li, BF)
    proj = _ab_in_proj(xn, li, prm, tables)
    ga = _dilated_prompt(proj, B, S)
    vn, gb = _gmlp_prompt(proj, prm["ln_sgu"], prm["ws_b"], prm["bs_b"], li)
    h1, h1b = _proj_residual([ga, gb], prm["w_out_ab"], li, h, True)
    h2 = _ple(h1, h1b, p, prm["w_ple_gate"], prm["w_ple"], layer)
    chunk_start = ((S - 1) // B_CHUNK) * B_CHUNK
    return h2, proj, vn.reshape(B, S, B_WIDTH)[:, chunk_start:]


def _c_layer_prompt(h, li, layer, p, prm, tables, B, S):
    xn = _rms_rows(h, prm["ln_c"], li, BF)
    qn, qr, kv, z, gates = _c_in_proj(xn, li, prm, tables)
    kcmp, vcmp = _compress_prompt(kv, prm["cw_k"], prm["cw_v"], prm["kn_c"][li, 0][None], li, B, S)
    lhs = _nsa_prompt(qn, qr, kcmp, vcmp, kv, gates, z, B, S)
    h1, h1b = _proj_residual([lhs], prm["w_out_c"], li, h, True)
    h2 = _ple(h1, h1b, p, prm["w_ple_gate"], prm["w_ple"], layer)
    return h2, kv


def _prompt_trunk(x, p, prm):
    B, S, _ = x.shape
    tables = _rope_tables(S, 0, 1)
    h = x.reshape(B * S, D_MODEL)
    depth = p.shape[0]
    p = p.reshape(depth, B * S, PLE_DIM)
    projs, kvs, b_rows = [], [], []
    for i in range(depth):
        li = i // 2
        if i % 2 == 0:
            h, proj, vb = _ab_layer_prompt(h, li, i, p, prm, tables, B, S)
            projs.append(proj)
            b_rows.append(vb)
        else:
            h, kv = _c_layer_prompt(h, li, i, p, prm, tables, B, S)
            kvs.append(kv)

    def rows(arrays, k_off, v_off, heads, keep):
        a = _emit_rows([[(x, k_off), (x, v_off)] for x in arrays], heads, B, S, keep)
        return a.reshape(a.shape[:3] + (keep, heads, HEAD_DIM))

    return (h.reshape(B, S, D_MODEL), rows(projs, AB_K, AB_V, A_HEADS, S), jnp.stack(b_rows, axis=1),
            rows(kvs, C_KC, C_VC, C_KV_HEADS, S), rows(kvs, C_KS, C_VS, C_KV_HEADS, S),
            rows(kvs, C_KW, C_VW, C_KV_HEADS, min(C_WIN, S)))


def _pad_rows(x, rows):
    return jnp.concatenate([x, jnp.zeros((rows - x.shape[0], x.shape[1]), x.dtype)], axis=0)


def _bf_round(x):
    return x.astype(BF).astype(F32)


def _dilated_sample_kernel(q_ref, kn_ref, pj_ref, kp_ref, vp_ref, o_ref, *, P):
    h = pl.program_id(1)
    qs = [q_ref[pl.ds(g * A_HEADS + h, 1), :] for g in range(A_GROUPS)]
    qmat = _pad_rows(jnp.concatenate(qs, axis=0), SUBLANES)
    k_new = kn_ref[pl.ds(h, 1), :]
    v_new = pj_ref[pl.ds(AB_V // HEAD_DIM + h, 1), :]
    z = pj_ref[pl.ds(AB_ZA // HEAD_DIM + h, 1), :]
    grp = lax.broadcasted_iota(jnp.int32, (SUBLANES, P), 0)
    delta = P - lax.broadcasted_iota(jnp.int32, (SUBLANES, P), 1)
    dil = jnp.where(grp == 0, A_PATTERNS[0][1], jnp.where(grp == 1, A_PATTERNS[1][1], A_PATTERNS[2][1]))
    win = jnp.where(grp == 0, A_PATTERNS[0][0], jnp.where(grp == 1, A_PATTERNS[1][0], A_PATTERNS[2][0]))
    valid = ((delta & (dil - 1)) == 0) & (delta <= win) & (grp < A_GROUPS)
    head_rows = pl.ds(h, P, stride=A_HEADS)
    s = jnp.where(valid, _dot_nt(qmat.astype(BF), kp_ref[head_rows, :].astype(BF)) * ATTN_SCALE, NEG_INF)
    s_new = jnp.sum(_bf_round(qmat) * _bf_round(k_new), axis=-1, keepdims=True) * ATTN_SCALE
    m = jnp.maximum(jnp.max(s, axis=-1, keepdims=True), s_new)
    e = jnp.exp(s - m)
    e_new = jnp.exp(s_new - m)
    den = jnp.sum(e, axis=-1, keepdims=True) + e_new
    num = _dot(e.astype(BF), vp_ref[head_rows, :].astype(BF)) + _bf_round(e_new) * _bf_round(v_new)
    live = lax.broadcasted_iota(jnp.int32, (SUBLANES, 1), 0) < A_GROUPS
    m_all = jnp.max(jnp.where(live, m, NEG_INF), axis=0, keepdims=True)
    w = jnp.where(live, jnp.exp(m - m_all), 0.0)
    num_t = jnp.sum(w * num, axis=0, keepdims=True)
    den_t = jnp.sum(w * den, axis=0, keepdims=True)
    o_ref[pl.ds(h, 1), :] = (num_t / den_t) * _silu(z)


def _dilated_sample(q_r, k_r, proj, cache, li):
    DB, P = cache.shape[0], cache.shape[3]
    cache = cache.reshape(cache.shape[:3] + (P * A_HEADS, HEAD_DIM))
    row3 = lambda x: x.reshape(DB, x.shape[1] // HEAD_DIM, HEAD_DIM)
    full = lambda x: pl.BlockSpec((None,) + x.shape[1:], lambda b, h: (b, 0, 0))
    q3, k3, p3 = row3(q_r), row3(k_r), row3(proj)
    cspec = lambda kv: pl.BlockSpec((None, None, None, P * A_HEADS, HEAD_DIM), lambda b, h: (b, li, kv, 0, 0))
    out = pl.pallas_call(
        functools.partial(_dilated_sample_kernel, P=P), grid=(DB, A_HEADS),
        in_specs=[full(q3), full(k3), full(p3), cspec(0), cspec(1)],
        out_specs=pl.BlockSpec((None, A_HEADS, HEAD_DIM), lambda b, h: (b, 0, 0)),
        out_shape=jax.ShapeDtypeStruct((DB, A_HEADS, HEAD_DIM), F32),
        compiler_params=_params("arbitrary", "arbitrary"), name="dilated_sample",
    )(q3, k3, p3, cache, cache)
    return out.reshape(DB, A_WIDTH)


def _gmlp_sample_kernel(u_ref, v_ref, z_ref, g_ref, w0_ref, b0_ref, vn_ref, o_ref):
    vn = _layer_norm(v_ref[...], g_ref[...])
    vn_ref[...] = vn
    mixed = _bf_round(w0_ref[...]) * _bf_round(vn) + b0_ref[...]
    o_ref[...] = (u_ref[...] * mixed) * _silu(z_ref[...])


def _gmlp_sample(proj, ln_sgu, ws_b, bs_b, li):
    M, wb = proj.shape[0], B_WIDTH
    blk = lambda off: pl.BlockSpec((M, wb), lambda i: (0, off // wb))
    vec = pl.BlockSpec((1, wb), lambda i: (0, 0))
    w0 = jnp.repeat(ws_b[li, :, 0, 0], wb // B_GROUPS)[None]
    b0 = jnp.repeat(bs_b[li, :, 0], wb // B_GROUPS)[None]
    shp = jax.ShapeDtypeStruct((M, wb), F32)
    return pl.pallas_call(
        _gmlp_sample_kernel, in_specs=[blk(AB_UB), blk(AB_VB), blk(AB_ZB), vec, vec, vec],
        out_specs=[pl.BlockSpec((M, wb), lambda i: (0, 0))] * 2, out_shape=[shp, shp], grid=(1,),
        compiler_params=_params("arbitrary"), name="gmlp_sample",
    )(proj, proj, proj, ln_sgu[li][None], w0, b0)


def _page_compress_kernel(pt_ref, w_ref, *refs, pages, segs):
    o_ref = refs[pages]
    tiles = CMP_STRIDE * C_KV_HEADS // SUBLANES
    low = lax.broadcasted_iota(jnp.int32, (SUBLANES, HEAD_DIM), 0) < C_KV_HEADS
    for j in range(pages):
        page = refs[j]
        for kv in range(2):
            folded = []
            for n in range(segs):
                base = n * tiles * SUBLANES
                first = second = None
                for i in range(tiles):
                    rows = page[kv, base + i * SUBLANES:base + (i + 1) * SUBLANES, :]
                    t1 = rows * w_ref[2 * kv, i * SUBLANES:(i + 1) * SUBLANES, :]
                    t2 = rows * w_ref[2 * kv + 1, i * SUBLANES:(i + 1) * SUBLANES, :]
                    first = t1 if first is None else first + t1
                    second = t2 if second is None else second + t2
                folded.append((first + pltpu.roll(first, C_KV_HEADS, 0), second + pltpu.roll(second, C_KV_HEADS, 0)))
            for i in range(segs // 2):
                out_rows = slice((j * segs // 2 + i) * SUBLANES, (j * segs // 2 + i + 1) * SUBLANES)
                o_ref[2 * kv, out_rows, :] = jnp.where(low, folded[2 * i][0], folded[2 * i + 1][0])
                o_ref[2 * kv + 1, out_rows, :] = jnp.where(low, folded[2 * i][1], folded[2 * i + 1][1])


def _page_compress(pool, page_table, cw_k, cw_v, li, pages=16):
    DB, n_pages = page_table.shape
    page = pool.shape[3]
    assert 2 * C_KV_HEADS == SUBLANES and page % (2 * CMP_STRIDE) == 0
    pool = pool.reshape(pool.shape[:3] + (page * C_KV_HEADS, HEAD_DIM))
    segs = page // CMP_STRIDE
    rows_out = segs * C_KV_HEADS
    lanes = lambda w: jnp.broadcast_to(w.reshape(CMP_STRIDE * C_KV_HEADS, 1), (CMP_STRIDE * C_KV_HEADS, HEAD_DIM))
    w = jnp.stack([lanes(cw_k[li, :CMP_STRIDE]), lanes(cw_k[li, CMP_STRIDE:]),
                   lanes(cw_v[li, :CMP_STRIDE]), lanes(cw_v[li, CMP_STRIDE:])])

    def pspec(j):
        return pl.BlockSpec((None, None, 2, page * C_KV_HEADS, HEAD_DIM),
                            lambda b, i, pt: (pt[b * n_pages + i * pages + j], li, 0, 0, 0))

    return pl.pallas_call(
        functools.partial(_page_compress_kernel, pages=pages, segs=segs),
        grid_spec=pltpu.PrefetchScalarGridSpec(
            num_scalar_prefetch=1, grid=(DB, n_pages // pages),
            in_specs=[pl.BlockSpec(w.shape, lambda b, i, pt: (0, 0, 0))] + [pspec(j) for j in range(pages)],
            out_specs=pl.BlockSpec((None, 4, rows_out * pages, HEAD_DIM), lambda b, i, pt: (b, 0, i, 0))),
        out_shape=jax.ShapeDtypeStruct((DB, 4, n_pages * rows_out, HEAD_DIM), F32),
        compiler_params=_params("arbitrary", "arbitrary"), name="page_compress",
    )(page_table.reshape(-1), w, *([pool] * pages))


def _cmp_sample_kernel(fs_ref, q_ref, g_ref, o_ref, ix_ref, *, pos, n_sel, nj):
    rows = fs_ref.shape[1]
    row_i = lax.broadcasted_iota(jnp.int32, (rows, HEAD_DIM), 0)
    complete = row_i < rows - C_KV_HEADS
    kc = jnp.where(complete, fs_ref[0] + pltpu.roll(fs_ref[1], rows - C_KV_HEADS, 0), 0.0)
    vc = jnp.where(complete, fs_ref[2] + pltpu.roll(fs_ref[3], rows - C_KV_HEADS, 0), 0.0)
    kc = kc * lax.rsqrt(jnp.mean(kc * kc, axis=-1, keepdims=True) + NORM_EPS) * g_ref[...]
    col = lax.broadcasted_iota(jnp.int32, (C_HEADS, rows), 1)
    q_head = lax.broadcasted_iota(jnp.int32, (C_HEADS, rows), 0)
    vis = (((col // C_KV_HEADS) * CMP_STRIDE + (CMP_LEN - 1) <= pos)
           & ((col & (C_KV_HEADS - 1)) == q_head // C_GQA))
    s = jnp.where(vis, _dot_nt(q_ref[...].astype(BF), kc.astype(BF)) * ATTN_SCALE, NEG_INF)
    m = jnp.max(s, axis=-1, keepdims=True)
    m = jnp.where(m == NEG_INF, 0.0, m)
    e = jnp.exp(s - m)
    prob_b = (e / jnp.maximum(jnp.sum(e, axis=-1, keepdims=True), TINY)).astype(BF)
    o_ref[...] = _dot(prob_b, vc.astype(BF))

    imp_heads = _dot(prob_b, _cmp_to_sel(rows, nj, C_KV_HEADS).astype(BF))
    lane = lax.broadcasted_iota(jnp.int32, (1, nj), 1)
    cur = pos // SEL_BLOCK
    forced = (lane == 0) | (lane == cur) | (lane == cur - 1)
    ii = lax.broadcasted_iota(jnp.int32, (nj, nj), 0)
    jj = lax.broadcasted_iota(jnp.int32, (nj, nj), 1)
    slot = lax.broadcasted_iota(jnp.int32, (nj, LANES), 1).astype(F32)
    blk_id = lax.broadcasted_iota(jnp.int32, (nj, LANES), 0)
    for h in range(C_KV_HEADS):
        imp = jnp.sum(imp_heads[h * C_GQA:(h + 1) * C_GQA], axis=0, keepdims=True)
        score = jnp.where(forced, imp + FORCE_BONUS, imp)
        score = jnp.where((lane <= cur) & (lane < n_sel), score, NEG_INF)
        s_row = jnp.broadcast_to(score, (nj, nj))
        s_col = jnp.sum(jnp.where(ii == jj, s_row, 0.0), axis=-1, keepdims=True)
        ahead = (s_row > s_col) | ((s_row == s_col) & (jj < ii))
        rank = jnp.sum(jnp.where(ahead, 1.0, 0.0), axis=-1, keepdims=True)
        hit = rank == slot
        idx = jnp.sum(jnp.where(hit, blk_id, 0), axis=0, keepdims=True)
        ok = jnp.sum(jnp.where(hit & (s_col > NEG_INF), 1, 0), axis=0, keepdims=True)
        ix_ref[h] = jnp.concatenate([idx, ok, jnp.zeros((SUBLANES - 2, LANES), jnp.int32)], axis=0)


def _cmp_sample(fs, qn, gain, pos):
    DB, _, rows, _ = fs.shape
    total = pos + 1
    n_sel = -(-total // SEL_BLOCK)
    nj = -(-n_sel // LANES) * LANES
    q3 = qn.reshape(DB, C_HEADS, HEAD_DIM)
    return pl.pallas_call(
        functools.partial(_cmp_sample_kernel, pos=pos, n_sel=n_sel, nj=nj), grid=(DB,),
        in_specs=[pl.BlockSpec((None, 4, rows, HEAD_DIM), lambda b: (b, 0, 0, 0)),
                  pl.BlockSpec((None, C_HEADS, HEAD_DIM), lambda b: (b, 0, 0)),
                  pl.BlockSpec((1, HEAD_DIM), lambda b: (0, 0))],
        out_specs=[pl.BlockSpec((None, C_HEADS, HEAD_DIM), lambda b: (b, 0, 0)),
                   pl.BlockSpec((None, C_KV_HEADS, SUBLANES, LANES), lambda b: (b, 0, 0, 0))],
        out_shape=[jax.ShapeDtypeStruct((DB, C_HEADS, HEAD_DIM), F32),
                   jax.ShapeDtypeStruct((DB, C_KV_HEADS, SUBLANES, LANES), jnp.int32)],
        compiler_params=_params("arbitrary"), name="cmp_sample",
    )(fs, q3, gain)


def _sel_sample_kernel(pt_ref, ix_ref, ok_ref, q_ref, kn_ref, pj_ref, *refs, nb, pos, n_past):
    kv_refs = refs[:nb]
    o_ref, m_ref, l_ref, acc_ref = refs[nb:]
    b, h, i = pl.program_id(0), pl.program_id(1), pl.program_id(2)

    @pl.when(i == 0)
    def _():
        m_ref[...] = jnp.full(m_ref.shape, NEG_INF, F32)
        l_ref[...] = jnp.zeros(l_ref.shape, F32)
        acc_ref[...] = jnp.zeros(acc_ref.shape, F32)

    q8 = _pad_rows(q_ref[pl.ds(h * C_GQA, C_GQA), :], SUBLANES).astype(BF)
    k_new = kn_ref[pl.ds(h, 1), :]
    v_new = pj_ref[pl.ds(C_VS // HEAD_DIM + h, 1), :]
    first_row = lax.broadcasted_iota(jnp.int32, (SEL_BLOCK, HEAD_DIM), 0) == 0
    r = lax.broadcasted_iota(jnp.int32, (1, SEL_BLOCK), 1)
    head_rows = pl.ds(h, SEL_BLOCK, stride=C_KV_HEADS)
    scores, values = [], []
    for j in range(nb):
        slot = (b * C_KV_HEADS + h) * SEL_TOP + i * nb + j
        blk = ix_ref[slot]
        is_past = blk < n_past
        kt = jnp.where(is_past, kv_refs[j][0, head_rows, :], jnp.where(first_row, k_new, 0.0))
        vt = jnp.where(is_past, kv_refs[j][1, head_rows, :], jnp.where(first_row, v_new, 0.0))
        ok = (blk * SEL_BLOCK + r <= pos) & (ok_ref[slot] > 0)
        scores.append(jnp.where(ok, _dot_nt(q8, kt.astype(BF)) * ATTN_SCALE, NEG_INF))
        values.append(vt.astype(BF))
    m_old = m_ref[...]
    m_new = functools.reduce(jnp.maximum, [jnp.max(s, axis=-1, keepdims=True) for s in scores] + [m_old])
    m_safe = jnp.where(m_new == NEG_INF, 0.0, m_new)
    alpha = jnp.exp(m_old - m_safe)
    es = [jnp.exp(s - m_safe) for s in scores]
    l_ref[...] = alpha * l_ref[...] + functools.reduce(jnp.add, [jnp.sum(e, axis=-1, keepdims=True) for e in es])
    acc_ref[...] = alpha * acc_ref[...] + functools.reduce(
        jnp.add, [_dot(e.astype(BF), v) for e, v in zip(es, values)])
    m_ref[...] = m_new

    @pl.when(i == pl.num_programs(2) - 1)
    def _():
        o_ref[pl.ds(h * C_GQA, C_GQA), :] = (acc_ref[...] / l_ref[...])[:C_GQA]


def _sel_sample(pool, page_table, idx, ok, qr, ks_r, proj, li, pos, nb=SEL_TOP):
    DB, n_pages = page_table.shape
    page = pool.shape[3]
    bpp = page // SEL_BLOCK
    n_past = n_pages * bpp
    pool = pool.reshape(pool.shape[:3] + (page * C_KV_HEADS, HEAD_DIM))
    row3 = lambda x: x.reshape(DB, x.shape[1] // HEAD_DIM, HEAD_DIM)
    q3, k3, p3 = row3(qr), row3(ks_r), row3(proj)
    full = lambda x: pl.BlockSpec((None,) + x.shape[1:], lambda b, h, i, pt, ix, okf: (b, 0, 0))

    def bspec(j):
        def imap(b, h, i, pt, ix, okf):
            blk = jnp.clip(ix[(b * C_KV_HEADS + h) * SEL_TOP + i * nb + j], 0, n_past - 1)
            return (pt[b * n_pages + blk // bpp], li, 0, blk % bpp, 0)
        return pl.BlockSpec((None, None, 2, SEL_BLOCK * C_KV_HEADS, HEAD_DIM), imap)

    out = pl.pallas_call(
        functools.partial(_sel_sample_kernel, nb=nb, pos=pos, n_past=n_past),
        grid_spec=pltpu.PrefetchScalarGridSpec(
            num_scalar_prefetch=3, grid=(DB, C_KV_HEADS, SEL_TOP // nb),
            in_specs=[full(q3), full(k3), full(p3)] + [bspec(j) for j in range(nb)],
            out_specs=pl.BlockSpec((None, C_HEADS, HEAD_DIM), lambda b, h, i, pt, ix, okf: (b, 0, 0)),
            scratch_shapes=[pltpu.VMEM((SUBLANES, 1), F32), pltpu.VMEM((SUBLANES, 1), F32),
                            pltpu.VMEM((SUBLANES, HEAD_DIM), F32)]),
        out_shape=jax.ShapeDtypeStruct((DB, C_HEADS, HEAD_DIM), F32),
        compiler_params=_params("arbitrary", "arbitrary", "arbitrary"), name="sel_sample",
    )(page_table.reshape(-1), idx, ok, q3, k3, p3, *([pool] * nb))
    return out


def _win_sample_kernel(q_ref, kn_ref, pj_ref, kp_ref, vp_ref, oc_ref, os_ref, g_ref, z_ref, o_ref):
    h = pl.program_id(1)
    heads = pl.ds(h * C_GQA, C_GQA)
    q8 = _pad_rows(q_ref[heads, :], SUBLANES)
    k_new = kn_ref[pl.ds(h, 1), :]
    v_new = pj_ref[pl.ds(C_VW // HEAD_DIM + h, 1), :]
    head_rows = pl.ds(h, kp_ref.shape[0] // C_KV_HEADS, stride=C_KV_HEADS)
    s = _dot_nt(q8.astype(BF), kp_ref[head_rows, :].astype(BF)) * ATTN_SCALE
    s_new = jnp.sum(_bf_round(q8) * _bf_round(k_new), axis=-1, keepdims=True) * ATTN_SCALE
    m = jnp.maximum(jnp.max(s, axis=-1, keepdims=True), s_new)
    e = jnp.exp(s - m)
    e_new = jnp.exp(s_new - m)
    den = jnp.sum(e, axis=-1, keepdims=True) + e_new
    p_past = (e / den).astype(BF)
    o_win = _dot(p_past, vp_ref[head_rows, :].astype(BF)) + _bf_round(e_new / den) * _bf_round(v_new)
    g = g_ref[pl.ds(h, 1), :]
    for a in range(C_GQA):
        gate = [_sigmoid(g[:, br * C_GQA + a:br * C_GQA + a + 1]) for br in range(3)]
        row = pl.ds(h * C_GQA + a, 1)
        o = gate[0] * oc_ref[row, :] + gate[1] * os_ref[row, :] + gate[2] * o_win[a:a + 1]
        o_ref[row, :] = o * _silu(z_ref[row, :])


def _win_sample(cache, qr, kw_r, proj, o_cmp, o_sel, gates, z, li):
    DB, W = cache.shape[0], cache.shape[3]
    assert W <= C_WIN
    cache = cache.reshape(cache.shape[:3] + (W * C_KV_HEADS, HEAD_DIM))
    row3 = lambda x: x.reshape(DB, x.shape[1] // HEAD_DIM, HEAD_DIM)
    full = lambda x: pl.BlockSpec((None,) + x.shape[1:], lambda b, h: (b, 0, 0))
    ins = [row3(qr), row3(kw_r), row3(proj)]
    tail = [o_cmp, o_sel, row3(gates), row3(z)]
    cspec = lambda kv: pl.BlockSpec((None, None, None, W * C_KV_HEADS, HEAD_DIM), lambda b, h: (b, li, kv, 0, 0))
    out = pl.pallas_call(
        _win_sample_kernel, grid=(DB, C_KV_HEADS),
        in_specs=[full(x) for x in ins] + [cspec(0), cspec(1)] + [full(x) for x in tail],
        out_specs=pl.BlockSpec((None, C_HEADS, HEAD_DIM), lambda b, h: (b, 0, 0)),
        out_shape=jax.ShapeDtypeStruct((DB, C_HEADS, HEAD_DIM), F32),
        compiler_params=_params("arbitrary", "arbitrary"), name="win_sample",
    )(*ins, cache, cache, *tail)
    return out.reshape(DB, C_WIDTH)


def _ab_layer_sample(h, li, layer, p, prm, tables, cache_a):
    DB = h.shape[0]
    xn = _rms_rows(h, prm["ln_ab"], li, F32)
    proj = _ab_in_proj(xn, li, prm, tables)
    q_r = proj[:, AB_Q:AB_Q + A_GROUPS * A_WIDTH]
    k_r = proj[:, AB_K:AB_K + A_WIDTH]
    ga = _dilated_sample(q_r, k_r, proj, cache_a, li)
    vn, gb = _gmlp_sample(proj, prm["ln_sgu"], prm["ws_b"], prm["bs_b"], li)
    h1, _ = _proj_residual([ga, gb], prm["w_out_ab"], li, h, False)
    h2 = _ple(h1, h1, p, prm["w_ple_gate"], prm["w_ple"], layer)
    kv = jnp.stack([k_r.reshape(DB, 1, A_HEADS, HEAD_DIM),
                    proj[:, AB_V:AB_V + A_WIDTH].reshape(DB, 1, A_HEADS, HEAD_DIM)], axis=1)
    return h2, kv, vn.reshape(DB, 1, B_WIDTH)


def _c_layer_sample(h, li, layer, p, prm, tables, past, pos):
    DB = h.shape[0]
    xn = _rms_rows(h, prm["ln_c"], li, F32)
    qn, qr, proj, z, gates = _c_in_proj(xn, li, prm, tables)
    ks_r = proj[:, C_KS:C_KS + C_KV_WIDTH]
    kw_r = proj[:, C_KW:C_KW + C_KV_WIDTH]
    fs = _page_compress(past["cmp"], past["page_table"], prm["cw_k"], prm["cw_v"], li)
    o_cmp, picks = _cmp_sample(fs, qn, prm["kn_c"][li, 0][None], pos)
    idx = picks[:, :, 0, :SEL_TOP].reshape(-1)
    ok = picks[:, :, 1, :SEL_TOP].reshape(-1)
    o_sel = _sel_sample(past["sel"], past["page_table"], idx, ok, qr, ks_r, proj, li, pos)
    lhs = _win_sample(past["win"], qr, kw_r, proj, o_cmp, o_sel, gates, z, li)
    h1, _ = _proj_residual([lhs], prm["w_out_c"], li, h, False)
    h2 = _ple(h1, h1, p, prm["w_ple_gate"], prm["w_ple"], layer)
    kvs = (DB, 1, C_KV_HEADS, HEAD_DIM)
    col = lambda off: proj[:, off:off + C_KV_WIDTH].reshape(kvs)
    cmp_rows = jnp.stack([col(C_KC), col(C_VC)], axis=1)
    sel_rows = jnp.stack([ks_r.reshape(kvs), col(C_VS)], axis=1)
    win_rows = jnp.stack([kw_r.reshape(kvs), col(C_VW)], axis=1)
    return h2, cmp_rows, sel_rows, win_rows


def _sample_trunk(x, p, past, prm):
    DB, S, _ = x.shape
    assert S == 1
    pos = past["page_table"].shape[1] * past["cmp"].shape[3]
    assert pos % CMP_STRIDE == 0 and past["cmp"].shape[3] % SEL_BLOCK == 0
    tables = _rope_tables(DB, pos, 0)
    h = x.reshape(DB, D_MODEL)
    a_rows, b_rows, cmp_rows, sel_rows, win_rows = [], [], [], [], []
    pi = p.reshape(p.shape[0], DB, PLE_DIM)
    for i in range(p.shape[0]):
        li = i // 2
        if i % 2 == 0:
            h, kv, vb = _ab_layer_sample(h, li, i, pi, prm, tables, past["a"])
            a_rows.append(kv)
            b_rows.append(vb)
        else:
            h, c, s, w = _c_layer_sample(h, li, i, pi, prm, tables, past, pos)
            cmp_rows.append(c)
            sel_rows.append(s)
            win_rows.append(w)
    st = lambda xs: jnp.stack(xs, axis=1)
    return h.reshape(DB, S, D_MODEL), st(a_rows), st(b_rows), st(cmp_rows), st(sel_rows), st(win_rows)


def kernel(x_prompt, x_sample, cache_a_kv, cache_c_cmp_kv, cache_c_sel_kv, cache_c_win_kv, page_table,
           p_prompt, p_sample, ln_ab, w_in_ab, qn_a, kn_a, ln_sgu, ws_b, bs_b, w_out_ab,
           ln_c, w_in_c, qn_c, kn_c, cw_k, cw_v, w_out_c, w_ple, w_ple_gate):
    prm = dict(ln_ab=ln_ab, w_in_ab=w_in_ab, qn_a=qn_a, kn_a=kn_a, ln_sgu=ln_sgu, ws_b=ws_b, bs_b=bs_b,
               w_out_ab=w_out_ab, ln_c=ln_c, w_in_c=w_in_c, qn_c=qn_c, kn_c=kn_c, cw_k=cw_k, cw_v=cw_v,
               w_out_c=w_out_c, w_ple=w_ple, w_ple_gate=w_ple_gate)
    y_p, a_p, b_p, cmp_p, sel_p, win_p = _prompt_trunk(x_prompt, p_prompt, prm)
    past = dict(a=cache_a_kv, cmp=cache_c_cmp_kv, sel=cache_c_sel_kv, win=cache_c_win_kv, page_table=page_table)
    y_s, a_s, b_s, cmp_s, sel_s, win_s = _sample_trunk(x_sample, p_sample, past, prm)
    return (y_p, y_s, a_p, a_s, b_p, b_s, cmp_p, cmp_s, sel_p, sel_s, win_p, win_s)
```

```python
import functools
import math

import jax
import jax.numpy as jnp
from jax import lax
from jax.experimental import pallas as pl
from jax.experimental.pallas import tpu as pltpu

F32 = jnp.float32
BF = jnp.bfloat16

D_MODEL = 2048
HEAD_DIM = 128
ROPE_DIM = HEAD_DIM // 4
ROPE_HALF = ROPE_DIM // 2
ROPE_THETA = 500000.0
NORM_EPS = 1e-6
Q_BLOCK = 128
PLE_DIM = 256
ATTN_SCALE = HEAD_DIM ** -0.5
SCALE_LOG2E = ATTN_SCALE * math.log2(math.e)
TINY = 1e-30
A_HEADS = D_MODEL // (2 * HEAD_DIM)
A_PATTERNS = ((128, 1), (512, 4), (2048, 16))
A_GROUPS = len(A_PATTERNS)
A_WIDTH = A_HEADS * HEAD_DIM
B_CHUNK = 128
B_WIDTH = D_MODEL - A_WIDTH
B_GROUPS = 8
C_HEADS = D_MODEL // HEAD_DIM
C_KV_HEADS = 4
C_GQA = C_HEADS // C_KV_HEADS
C_WIDTH = C_HEADS * HEAD_DIM
C_KV_WIDTH = C_KV_HEADS * HEAD_DIM
CMP_LEN = 32
CMP_STRIDE = 16
SEL_BLOCK = 64
SEL_TOP = 16
C_WIN = 512
FORCE_BONUS = 1000.0
NEG_INF = float("-inf")

LANES = 128
SUBLANES = 8
MM_TILE_M = 1024
MM_TILE_N = 512
MM_TILE_N_WIDE = 1024
ROW_TILE = 512
EMIT_TILE = 256
SAMPLE_ROWS = 16
VMEM_LIMIT = 56 * 1024 * 1024
DILATED_BATCH = 16

AB_Q, AB_K, AB_V, AB_ZA, AB_UB, AB_VB, AB_ZB = 0, 3072, 4096, 5120, 6144, 7168, 8192
C_W_Q, C_W_KV, C_G, C_Z = 0, 2048, 5120, 5168
C_KC, C_VC, C_KS, C_VS, C_KW, C_VW = 0, 512, 1024, 1536, 2048, 2560


def _params(*sem):
    return pltpu.CompilerParams(dimension_semantics=sem, vmem_limit_bytes=VMEM_LIMIT)


def _dot(a, b):
    return jnp.dot(a, b, preferred_element_type=F32)


def _dot_nt(a, b):
    return lax.dot_general(a, b, (((1,), (1,)), ((), ())), preferred_element_type=F32)


def _sigmoid(x):
    return jax.nn.sigmoid(x)


def _silu(x):
    return x * _sigmoid(x)


def _rope_table_kernel(cos_ref, sa_ref, sb_ref, *, start, step):
    rows = cos_ref.shape[0]
    r = lax.broadcasted_iota(jnp.int32, (rows, HEAD_DIM), 0)
    lane = lax.broadcasted_iota(jnp.int32, (rows, HEAD_DIM), 1)
    pos = (start + r * step).astype(F32)
    j = (lane & (ROPE_HALF - 1)).astype(F32)
    inv = jnp.exp(-math.log(ROPE_THETA) * j / ROPE_HALF)
    ang = pos * inv
    c, s = jnp.cos(ang), jnp.sin(ang)
    in_rope = lane < ROPE_DIM
    cos_ref[...] = jnp.where(in_rope, c, 1.0)
    sa_ref[...] = jnp.where(in_rope & (lane >= ROPE_HALF), s, 0.0)
    sb_ref[...] = jnp.where(lane < ROPE_HALF, -s, 0.0)


def _rope_tables(rows, start, step):
    shp = jax.ShapeDtypeStruct((rows, HEAD_DIM), F32)
    return pl.pallas_call(
        functools.partial(_rope_table_kernel, start=start, step=step),
        out_shape=(shp, shp, shp), name="rope_tables")()


def _rms_kernel(x_ref, g_ref, o_ref):
    x = x_ref[...]
    y = x * lax.rsqrt(jnp.mean(x * x, axis=-1, keepdims=True) + NORM_EPS)
    o_ref[...] = (y * g_ref[...]).astype(o_ref.dtype)


def _rms_rows(x, gains, li, out_dtype):
    M, D = x.shape
    tm = min(M, ROW_TILE)
    return pl.pallas_call(
        _rms_kernel, grid=(M // tm,),
        in_specs=[pl.BlockSpec((tm, D), lambda i: (i, 0)),
                  pl.BlockSpec((None, 1, D), lambda i: (li, 0, 0))],
        out_specs=pl.BlockSpec((tm, D), lambda i: (i, 0)),
        out_shape=jax.ShapeDtypeStruct((M, D), out_dtype),
        compiler_params=_params("arbitrary"), name="rms_rows",
    )(x, gains.reshape(gains.shape[0], 1, D))


TILE_RAW, TILE_NORM, TILE_HALF_NORM = 0, 1, 2


def _mm_norm_kernel(x_ref, w_ref, g_ref, cos_ref, sa_ref, sb_ref, *rest, kinds, two_out):
    o_ref, wb_ref = rest[0], rest[-1]
    n = pl.program_id(0)

    @pl.when(pl.program_id(1) == 0)
    def _():
        wb_ref[...] = w_ref[...].astype(BF)

    acc = _dot(x_ref[...].astype(BF), wb_ref[...])
    heads = acc.shape[1] // HEAD_DIM

    def store(kind):
        normed = {TILE_RAW: 0, TILE_NORM: heads, TILE_HALF_NORM: heads // 2}[kind]
        for j in range(heads):
            cols = slice(j * HEAD_DIM, (j + 1) * HEAD_DIM)
            a = acc[:, cols]
            if j >= normed:
                o_ref[:, cols] = a
                continue
            y = a * lax.rsqrt(jnp.mean(a * a, axis=-1, keepdims=True) + NORM_EPS) * g_ref[...]
            if two_out:
                rest[1][:, cols] = y
            up = pltpu.roll(y, ROPE_HALF, 1)
            down = pltpu.roll(y, HEAD_DIM - ROPE_HALF, 1)
            o_ref[:, cols] = y * cos_ref[...] + up * sa_ref[...] + down * sb_ref[...]

    distinct = sorted(set(kinds))
    if len(distinct) == 1:
        store(distinct[0])
    else:
        for kind in distinct:
            hit = functools.reduce(jnp.logical_or, [n == t for t, k in enumerate(kinds) if k == kind])
            pl.when(hit)(functools.partial(store, kind))


def _matmul_norm(x, w, li, col0, kinds, gains, gain_of_tile, tables, two_out=False):
    M, K = x.shape
    tm, tn = min(M, MM_TILE_M), MM_TILE_N_WIDE
    assert col0 % tn == 0 and (not two_out or set(kinds) == {TILE_NORM})
    ncols = tn * len(kinds)
    cos, sa, sb = tables
    nt = cos.shape[0] // tm

    def gain_index(n, m):
        idx = 0
        for t, g in enumerate(gain_of_tile):
            idx = jnp.where(n == t, g, idx)
        return (idx, 0, 0)

    tspec = pl.BlockSpec((tm, HEAD_DIM), lambda n, m: (m % nt, 0))
    ospec = pl.BlockSpec((tm, tn), lambda n, m: (m, n))
    n_out = 2 if two_out else 1
    outs = pl.pallas_call(
        functools.partial(_mm_norm_kernel, kinds=tuple(kinds), two_out=two_out),
        grid=(len(kinds), M // tm),
        in_specs=[pl.BlockSpec((tm, K), lambda n, m: (m, 0)),
                  pl.BlockSpec((None, K, tn), lambda n, m: (li, 0, n + col0 // tn)),
                  pl.BlockSpec((None, 1, HEAD_DIM), gain_index), tspec, tspec, tspec],
        out_specs=[ospec] * n_out, out_shape=[jax.ShapeDtypeStruct((M, ncols), F32)] * n_out,
        scratch_shapes=[pltpu.VMEM((K, tn), BF)],
        compiler_params=_params("arbitrary", "arbitrary"), name="matmul_norm",
    )(x, w, gains, cos, sa, sb)
    return (outs[1], outs[0]) if two_out else outs[0]


def _mm_kernel(x_ref, w_ref, o_ref, wb_ref):
    @pl.when(pl.program_id(1) == 0)
    def _():
        wb_ref[...] = w_ref[...].astype(BF)

    o_ref[...] = _dot(x_ref[...].astype(BF), wb_ref[...])


def _matmul(x, w, li, col0, ncols):
    M, K = x.shape
    tm = min(M, MM_TILE_M)
    tn = MM_TILE_N_WIDE if ncols % MM_TILE_N_WIDE == 0 and col0 % MM_TILE_N_WIDE == 0 else MM_TILE_N
    if li is None:
        wspec = pl.BlockSpec((K, tn), lambda n, m: (0, n + col0 // tn))
    else:
        wspec = pl.BlockSpec((None, K, tn), lambda n, m: (li, 0, n + col0 // tn))
    return pl.pallas_call(
        _mm_kernel, grid=(ncols // tn, M // tm),
        in_specs=[pl.BlockSpec((tm, K), lambda n, m: (m, 0)), wspec],
        out_specs=pl.BlockSpec((tm, tn), lambda n, m: (m, n)),
        out_shape=jax.ShapeDtypeStruct((M, ncols), F32),
        scratch_shapes=[pltpu.VMEM((K, tn), BF)],
        compiler_params=_params("arbitrary", "arbitrary"), name="matmul",
    )(x, w)


def _proj_res_kernel(*refs, ks, with_bf):
    n = len(ks)
    lhs = refs[:n]
    w_ref, res_ref, o_ref = refs[n:n + 3]
    wb_ref = refs[-1]

    @pl.when(pl.program_id(1) == 0)
    def _():
        wb_ref[...] = w_ref[...].astype(BF)

    acc = res_ref[...]
    off = 0
    for r, k in zip(lhs, ks):
        acc = acc + _dot(r[...].astype(BF), wb_ref[off:off + k, :])
        off += k
    o_ref[...] = acc
    if with_bf:
        refs[n + 3][...] = acc.astype(BF)


def _proj_residual(lhs_list, w, li, res, with_bf):
    M, N = res.shape
    ks = tuple(a.shape[1] for a in lhs_list)
    K = sum(ks)
    tm, tn = min(M, MM_TILE_M), MM_TILE_N_WIDE
    ospec = pl.BlockSpec((tm, tn), lambda n, m: (m, n))
    out_shape = [jax.ShapeDtypeStruct((M, N), F32)]
    if with_bf:
        out_shape.append(jax.ShapeDtypeStruct((M, N), BF))
    outs = pl.pallas_call(
        functools.partial(_proj_res_kernel, ks=ks, with_bf=with_bf),
        grid=(N // tn, M // tm),
        in_specs=[pl.BlockSpec((tm, k), lambda n, m: (m, 0)) for k in ks]
        + [pl.BlockSpec((None, K, tn), lambda n, m: (li, 0, n)), ospec],
        out_specs=[ospec] * len(out_shape), out_shape=out_shape,
        scratch_shapes=[pltpu.VMEM((K, tn), BF)],
        compiler_params=_params("arbitrary", "arbitrary"), name="proj_residual",
    )(*lhs_list, w, res)
    return outs if with_bf else (outs[0], outs[0])


def _ple_kernel(hl_ref, wg_ref, p_ref, wp_ref, h_ref, o_ref, wgb_ref, wpb_ref):
    @pl.when(pl.program_id(1) == 0)
    def _():
        wgb_ref[...] = wg_ref[...].astype(BF)
        wpb_ref[...] = wp_ref[...].astype(BF)

    gate = _sigmoid(_dot(hl_ref[...].astype(BF), wgb_ref[...]))
    pp = _dot(p_ref[...].astype(BF), wpb_ref[...])
    o_ref[...] = h_ref[...] + gate * pp


def _ple(h, h_lhs, p, w_gate, w_ple, layer):
    M, N = h.shape
    K, KP = h_lhs.shape[1], p.shape[2]
    tm, tn = min(M, MM_TILE_M), MM_TILE_N_WIDE
    ospec = pl.BlockSpec((tm, tn), lambda n, m: (m, n))
    return pl.pallas_call(
        _ple_kernel, grid=(N // tn, M // tm),
        in_specs=[pl.BlockSpec((tm, K), lambda n, m: (m, 0)),
                  pl.BlockSpec((None, K, tn), lambda n, m: (layer, 0, n)),
                  pl.BlockSpec((None, tm, KP), lambda n, m: (layer, m, 0)),
                  pl.BlockSpec((None, KP, tn), lambda n, m: (layer, 0, n)), ospec],
        out_specs=ospec, out_shape=jax.ShapeDtypeStruct((M, N), F32),
        scratch_shapes=[pltpu.VMEM((K, tn), BF), pltpu.VMEM((KP, tn), BF)],
        compiler_params=_params("arbitrary", "arbitrary"), name="ple",
    )(h_lhs, w_gate, p, w_ple, h)


def _dilated_kernel(q0_ref, q1_ref, q2_ref, k_ref, v_ref, z_ref, o_ref, num_ref, m_ref, l_ref, *, S):
    blk = Q_BLOCK
    row = lax.broadcasted_iota(jnp.int32, (blk, blk), 0)
    col = lax.broadcasted_iota(jnp.int32, (blk, blk), 1)
    q_refs = (q0_ref, q1_ref, q2_ref)
    for g, (window, d) in enumerate(A_PATTERNS):
        assert window // d == blk
        nblk = S // d // blk
        q_ref = q_refs[g]

        def batch(ub, carry, g=g, d=d, nblk=nblk, q_ref=q_ref):
            units = []
            for j in range(DILATED_BATCH):
                u = ub * DILATED_BATCH + j
                c = u // nblk
                i = u - c * nblk
                start = c + d * blk * i
                rows = pl.ds(start, blk, stride=d) if d > 1 else pl.ds(pl.multiple_of(start, blk), blk)
                q = q_ref[rows, :].astype(BF)
                s_c = _dot_nt(q, k_ref[rows, :].astype(BF))
                s_p, prows = None, None
                if nblk > 1:
                    pstart = jnp.maximum(start - d * blk, c)
                    prows = pl.ds(pstart, blk, stride=d) if d > 1 else pl.ds(pl.multiple_of(pstart, blk), blk)
                    s_p = _dot_nt(q, k_ref[prows, :].astype(BF))
                units.append((i, rows, prows, s_c, s_p))
            probs = []
            for i, rows, prows, s_c, s_p in units:
                s_c = jnp.where(col <= row, s_c * SCALE_LOG2E, NEG_INF)
                if s_p is not None:
                    s_p = jnp.where((col >= row) & (i > 0), s_p * SCALE_LOG2E, NEG_INF)
                    m = jnp.max(jnp.maximum(s_c, s_p), axis=-1, keepdims=True)
                    e_c, e_p = jnp.exp2(s_c - m), jnp.exp2(s_p - m)
                    l = jnp.sum(e_c + e_p, axis=-1, keepdims=True)
                    probs.append((m, l, e_c.astype(BF), e_p.astype(BF)))
                else:
                    m = jnp.max(s_c, axis=-1, keepdims=True)
                    e_c = jnp.exp2(s_c - m)
                    probs.append((m, jnp.sum(e_c, axis=-1, keepdims=True), e_c.astype(BF), None))
            for (i, rows, prows, _, _), (m, l, e_c, e_p) in zip(units, probs):
                num = _dot(e_c, v_ref[rows, :].astype(BF))
                if e_p is not None:
                    num = num + _dot(e_p, v_ref[prows, :].astype(BF))
                num_ref[g, rows, :] = num
                m_ref[g, rows, :] = jnp.broadcast_to(m, (blk, HEAD_DIM))
                l_ref[g, rows, :] = jnp.broadcast_to(l, (blk, HEAD_DIM))
            return carry

        assert (d * nblk) % DILATED_BATCH == 0
        lax.fori_loop(0, d * nblk // DILATED_BATCH, batch, 0)

    def merge(i, carry):
        rows = pl.ds(pl.multiple_of(i * blk, blk), blk)
        ms = [m_ref[g, rows, :] for g in range(A_GROUPS)]
        m_all = jnp.maximum(jnp.maximum(ms[0], ms[1]), ms[2])
        ws = [jnp.exp2(m - m_all) for m in ms]
        num = ws[0] * num_ref[0, rows, :] + ws[1] * num_ref[1, rows, :] + ws[2] * num_ref[2, rows, :]
        den = ws[0] * l_ref[0, rows, :] + ws[1] * l_ref[1, rows, :] + ws[2] * l_ref[2, rows, :]
        o_ref[rows, :] = ((num / den) * _silu(z_ref[rows, :])).astype(o_ref.dtype)
        return carry

    lax.fori_loop(0, S // blk, merge, 0, unroll=2)


def _dilated_prompt(proj, B, S):
    H = A_HEADS
    hb = lambda off: off // HEAD_DIM

    def col_spec(base):
        return pl.BlockSpec((S, HEAD_DIM), lambda b, h: (b, base + h))

    return pl.pallas_call(
        functools.partial(_dilated_kernel, S=S), grid=(B, H),
        in_specs=[col_spec(hb(AB_Q)), col_spec(hb(AB_Q) + H), col_spec(hb(AB_Q) + 2 * H), col_spec(hb(AB_K)),
                  col_spec(hb(AB_V)), col_spec(hb(AB_ZA))],
        out_specs=col_spec(0),
        out_shape=jax.ShapeDtypeStruct((B * S, A_WIDTH), BF),
        scratch_shapes=[pltpu.VMEM((A_GROUPS, S, HEAD_DIM), F32)] * 3,
        compiler_params=_params("arbitrary", "arbitrary"), name="dilated_prompt",
    )(*([proj] * 6))


def _layer_norm(v, g):
    xc = v - jnp.mean(v, axis=-1, keepdims=True)
    return xc * lax.rsqrt(jnp.mean(xc * xc, axis=-1, keepdims=True) + NORM_EPS) * g


def _gmlp_kernel(u_ref, v_ref, z_ref, g_ref, ws_ref, bst_ref, vn_ref, o_ref):
    vn = _layer_norm(v_ref[...], g_ref[...])
    vn_ref[...] = vn
    row = lax.broadcasted_iota(jnp.int32, (B_CHUNK, B_CHUNK), 0)
    col = lax.broadcasted_iota(jnp.int32, (B_CHUNK, B_CHUNK), 1)
    for g in range(B_GROUPS):
        cols = slice(g * LANES, (g + 1) * LANES)
        w = jnp.where(row >= col, ws_ref[g], 0.0).astype(BF)
        mixed = _dot(w, vn[:, cols].astype(BF)) + bst_ref[:, g:g + 1]
        o_ref[:, cols] = ((u_ref[:, cols] * mixed) * _silu(z_ref[:, cols])).astype(o_ref.dtype)


def _gmlp_prompt(proj, ln_sgu, ws_b, bs_b, li):
    M = proj.shape[0]
    wb = B_WIDTH
    blk = lambda off: pl.BlockSpec((B_CHUNK, wb), lambda i: (i, off // wb))
    bst = jnp.swapaxes(bs_b, 1, 2)
    return pl.pallas_call(
        _gmlp_kernel, grid=(M // B_CHUNK,),
        in_specs=[blk(AB_UB), blk(AB_VB), blk(AB_ZB),
                  pl.BlockSpec((None, 1, wb), lambda i: (li, 0, 0)),
                  pl.BlockSpec((None, B_GROUPS, B_CHUNK, B_CHUNK), lambda i: (li, 0, 0, 0)),
                  pl.BlockSpec((None, B_CHUNK, B_GROUPS), lambda i: (li, 0, 0))],
        out_specs=[pl.BlockSpec((B_CHUNK, wb), lambda i: (i, 0))] * 2,
        out_shape=[jax.ShapeDtypeStruct((M, wb), F32), jax.ShapeDtypeStruct((M, wb), BF)],
        compiler_params=_params("arbitrary"), name="gmlp_prompt",
    )(proj, proj, proj, ln_sgu.reshape(-1, 1, wb), ws_b, bst)


def _compress_rows(x_ref, w_ref, wbase, h, nseg):
    first = jnp.zeros((nseg, HEAD_DIM), F32)
    second = jnp.zeros((nseg, HEAD_DIM), F32)
    for r in range(CMP_STRIDE):
        rows = x_ref[pl.ds(r, nseg, stride=CMP_STRIDE), :]
        first = first + rows * w_ref[wbase + r * C_KV_HEADS + h]
        second = second + rows * w_ref[wbase + (CMP_STRIDE + r) * C_KV_HEADS + h]
    c = first + pltpu.roll(second, nseg - 1, 0)
    seg = lax.broadcasted_iota(jnp.int32, (nseg, HEAD_DIM), 0)
    return jnp.where(seg < nseg - 1, c, 0.0)


def _compress_kernel(cwk_ref, cwv_ref, kc_ref, vc_ref, g_ref, ko_ref, vo_ref, *, li, nseg):
    h = pl.program_id(1)
    wbase = li * CMP_LEN * C_KV_HEADS
    kc = _compress_rows(kc_ref, cwk_ref, wbase, h, nseg)
    ko_ref[...] = kc * lax.rsqrt(jnp.mean(kc * kc, axis=-1, keepdims=True) + NORM_EPS) * g_ref[...]
    vo_ref[...] = _compress_rows(vc_ref, cwv_ref, wbase, h, nseg)


def _compress_prompt(proj, cw_k, cw_v, gain, li, B, S):
    nseg = S // CMP_STRIDE
    hb = lambda off: off // HEAD_DIM
    smem = pl.BlockSpec(memory_space=pltpu.SMEM)
    ospec = pl.BlockSpec((None, None, nseg, HEAD_DIM), lambda b, h: (b, h, 0, 0))
    oshape = jax.ShapeDtypeStruct((B, C_KV_HEADS, nseg, HEAD_DIM), F32)
    return pl.pallas_call(
        functools.partial(_compress_kernel, li=li, nseg=nseg), grid=(B, C_KV_HEADS),
        in_specs=[smem, smem,
                  pl.BlockSpec((S, HEAD_DIM), lambda b, h: (b, hb(C_KC) + h)),
                  pl.BlockSpec((S, HEAD_DIM), lambda b, h: (b, hb(C_VC) + h)),
                  pl.BlockSpec((1, HEAD_DIM), lambda b, h: (0, 0))],
        out_specs=[ospec, ospec], out_shape=[oshape, oshape],
        compiler_params=_params("arbitrary", "arbitrary"), name="compress_prompt",
    )(cw_k.reshape(-1), cw_v.reshape(-1), proj, proj, gain)


def _cmp_to_sel(nrows, ncols, rows_per_seg):
    n = lax.broadcasted_iota(jnp.int32, (nrows, ncols), 0) // rows_per_seg
    j = lax.broadcasted_iota(jnp.int32, (nrows, ncols), 1)
    shared = (jnp.minimum(n * CMP_STRIDE + CMP_LEN, (j + 1) * SEL_BLOCK)
              - jnp.maximum(n * CMP_STRIDE, j * SEL_BLOCK))
    return jnp.maximum(shared, 0).astype(F32) / CMP_LEN


def _cmp_to_sel_t(nrows, ncols, seg_mask):
    j = lax.broadcasted_iota(jnp.int32, (nrows, ncols), 0)
    n = lax.broadcasted_iota(jnp.int32, (nrows, ncols), 1) & seg_mask
    shared = (jnp.minimum(n * CMP_STRIDE + CMP_LEN, (j + 1) * SEL_BLOCK)
              - jnp.maximum(n * CMP_STRIDE, j * SEL_BLOCK))
    return jnp.maximum(shared, 0).astype(F32) / CMP_LEN


def _rank_select_t(score, rows):
    sc = score[:rows]
    blk = lax.broadcasted_iota(jnp.int32, sc.shape, 0)
    rank = jnp.zeros(sc.shape, F32)
    for jp in range(rows):
        other = sc[jp:jp + 1, :]
        ahead = (other > sc) | ((other == sc) & (blk > jp))
        rank = rank + jnp.where(ahead, 1.0, 0.0)
    picked = jnp.where((rank < SEL_TOP) & (sc > NEG_INF), 1.0, 0.0)
    return jnp.concatenate([picked, jnp.zeros((score.shape[0] - rows, score.shape[1]), F32)], axis=0)


def _flash_loop_t(q4, k_ref, vt_ref, n_tiles, tk, bias_fn):
    reps = q4.shape[0] // LANES
    q_heads = [q4[g * LANES:(g + 1) * LANES] for g in range(reps)]

    def body(t, state):
        k0 = pl.multiple_of(t * tk, tk)
        kt = k_ref[pl.ds(k0, tk), :]
        vt = vt_ref[t]
        bias = bias_fn(k0)
        scores = [_dot_nt(kt, q_heads[g]) for g in range(reps)]
        mid = []
        for g in range(reps):
            m, l, _ = state[g]
            s = scores[g] * SCALE_LOG2E + bias
            m_new = jnp.maximum(m, jnp.max(s, axis=0, keepdims=True))
            m_safe = jnp.where(m_new == NEG_INF, 0.0, m_new)
            alpha = jnp.exp2(m - m_safe)
            e = jnp.exp2(s - m_safe)
            mid.append((m_new, alpha * l + jnp.sum(e, axis=0, keepdims=True), alpha, e.astype(BF)))
        return tuple((m_new, l, alpha * state[g][2] + _dot(vt, e))
                     for g, (m_new, l, alpha, e) in enumerate(mid))

    one = (jnp.full((1, LANES), NEG_INF, F32), jnp.zeros((1, LANES), F32), jnp.zeros((HEAD_DIM, LANES), F32))
    res = lax.fori_loop(0, n_tiles, body, (one,) * reps)
    return jnp.concatenate([acc / l for _, l, acc in res], axis=1)


def _stack_heads(ref):
    return jnp.concatenate([ref[:, g * HEAD_DIM:(g + 1) * HEAD_DIM] for g in range(C_GQA)], axis=0)


def _nsa_kernel(qn_ref, qr_ref, kcmp_ref, vcmp_ref, ks_ref, vs_ref, kw_ref, vw_ref, g_ref, z_ref, o_ref,
                ksb_ref, kwb_ref, vst_ref, vwt_ref, vct_ref, exp_ref, c2s_ref, *, S, n_sel, sel_tk):
    qb = Q_BLOCK
    qi = pl.program_id(2)
    q0 = qi * qb
    nseg = kcmp_ref.shape[0]
    cols4 = C_GQA * qb
    win_keys = C_WIN + qb

    @pl.when(qi == 0)
    def _():
        ksb_ref[...] = ks_ref[...].astype(BF)
        kwb_ref[...] = kw_ref[...].astype(BF)
        for t in range(S // sel_tk):
            vst_ref[t] = vs_ref[t * sel_tk:(t + 1) * sel_tk, :].T.astype(BF)
        for t in range(S // qb):
            vwt_ref[t] = vw_ref[t * qb:(t + 1) * qb, :].T.astype(BF)
        vct_ref[...] = vcmp_ref[...].T.astype(BF)
        key_blk = lax.broadcasted_iota(jnp.int32, (S, LANES), 0) // SEL_BLOCK
        exp_ref[...] = jnp.where(key_blk == lax.broadcasted_iota(jnp.int32, (S, LANES), 1), 1.0, 0.0).astype(BF)
        c2s_ref[...] = _cmp_to_sel_t(LANES, C_GQA * nseg, nseg - 1).astype(BF)

    p_q = q0 + lax.broadcasted_iota(jnp.int32, (1, qb), 1)
    p_4 = q0 + (lax.broadcasted_iota(jnp.int32, (1, cols4), 1) & (qb - 1))

    qn4 = _stack_heads(qn_ref).astype(BF)
    qr4 = _stack_heads(qr_ref).astype(BF)
    w0 = pl.multiple_of(jnp.maximum(q0 - C_WIN, 0), qb)
    s_cmp = _dot_nt(kcmp_ref[...].astype(BF), qn4)
    k_win = kwb_ref[pl.ds(w0, win_keys), :]
    s_win = [_dot_nt(k_win, qr4[g * qb:(g + 1) * qb]) for g in range(C_GQA)]

    seg = lax.broadcasted_iota(jnp.int32, (nseg, cols4), 0)
    vis = seg * CMP_STRIDE + (CMP_LEN - 1) <= p_4
    s = jnp.where(vis, s_cmp * SCALE_LOG2E, NEG_INF)
    m = jnp.max(s, axis=0, keepdims=True)
    m = jnp.where(m == NEG_INF, 0.0, m)
    e = jnp.exp2(s - m)
    prob_b = (e / jnp.maximum(jnp.sum(e, axis=0, keepdims=True), TINY)).astype(BF)
    o_cmp = _dot(vct_ref[...], prob_b)

    prob_stack = jnp.concatenate([prob_b[:, g * qb:(g + 1) * qb] for g in range(C_GQA)], axis=0)
    imp = _dot(c2s_ref[...], prob_stack)
    blk = lax.broadcasted_iota(jnp.int32, (LANES, qb), 0)
    cur = p_q // SEL_BLOCK
    forced = (blk == 0) | (blk == cur) | (blk == cur - 1)
    score = jnp.where(forced, imp + FORCE_BONUS, imp)
    score = jnp.where(blk <= cur, score, NEG_INF)
    sel = _rank_select_t(score, -(-n_sel // SUBLANES) * SUBLANES).astype(BF)

    kpos = w0 + lax.broadcasted_iota(jnp.int32, (win_keys, 1), 0)
    bias = jnp.where((kpos <= p_q) & (kpos >= p_q - C_WIN), 0.0, NEG_INF)
    t0 = w0 // qb
    win_e, win_den = [], []
    for g in range(C_GQA):
        s = s_win[g] * SCALE_LOG2E + bias
        e = jnp.exp2(s - jnp.max(s, axis=0, keepdims=True))
        win_den.append(jnp.sum(e, axis=0, keepdims=True))
        win_e.append(e.astype(BF))
    o_win = []
    for g in range(C_GQA):
        acc = _dot(vwt_ref[t0], win_e[g][:qb])
        for i in range(1, win_keys // qb):
            acc = acc + _dot(vwt_ref[t0 + i], win_e[g][i * qb:(i + 1) * qb])
        o_win.append(acc / win_den[g])
    o_win = jnp.concatenate(o_win, axis=1)

    def sel_bias(k0):
        chosen = _dot(exp_ref[pl.ds(k0, sel_tk), :], sel)
        kpos = k0 + lax.broadcasted_iota(jnp.int32, (sel_tk, 1), 0)
        return jnp.where((chosen > 0.5) & (kpos <= p_q), 0.0, NEG_INF)

    o_sel = _flash_loop_t(qr4, ksb_ref, vst_ref, (q0 + qb + sel_tk - 1) // sel_tk, sel_tk, sel_bias)

    gates_t = _sigmoid(g_ref[...].T)
    for g in range(C_GQA):
        cs = slice(g * HEAD_DIM, (g + 1) * HEAD_DIM)
        o = (gates_t[g:g + 1] * o_cmp[:, cs] + gates_t[C_GQA + g:C_GQA + g + 1] * o_sel[:, cs]
             + gates_t[2 * C_GQA + g:2 * C_GQA + g + 1] * o_win[:, cs])
        o_ref[:, cs] = (o.T * _silu(z_ref[:, cs])).astype(o_ref.dtype)


def _nsa_prompt(qn, qr, kcmp, vcmp, kv, gates, z, B, S):
    nq = S // Q_BLOCK
    nseg = kcmp.shape[2]
    gw = C_GQA * HEAD_DIM
    sel_tk = 512
    assert S >= C_WIN + Q_BLOCK and S % sel_tk == 0
    hb = lambda off: off // HEAD_DIM
    qspec = pl.BlockSpec((Q_BLOCK, gw), lambda b, h, i: (b * nq + i, h))
    cspec = pl.BlockSpec((None, None, nseg, HEAD_DIM), lambda b, h, i: (b, h, 0, 0))

    def kv_spec(base):
        return pl.BlockSpec((S, HEAD_DIM), lambda b, h, i: (b, base + h))

    return pl.pallas_call(
        functools.partial(_nsa_kernel, S=S, n_sel=-(-S // SEL_BLOCK), sel_tk=sel_tk),
        grid=(B, C_KV_HEADS, nq),
        in_specs=[qspec, qspec, cspec, cspec, kv_spec(hb(C_KS)), kv_spec(hb(C_VS)), kv_spec(hb(C_KW)),
                  kv_spec(hb(C_VW)),
                  pl.BlockSpec((Q_BLOCK, LANES), lambda b, h, i: (b * nq + i, h)), qspec],
        out_specs=qspec, out_shape=jax.ShapeDtypeStruct((B * S, C_WIDTH), BF),
        scratch_shapes=[pltpu.VMEM((S, HEAD_DIM), BF), pltpu.VMEM((S, HEAD_DIM), BF),
                        pltpu.VMEM((S // sel_tk, HEAD_DIM, sel_tk), BF),
                        pltpu.VMEM((S // Q_BLOCK, HEAD_DIM, Q_BLOCK), BF),
                        pltpu.VMEM((HEAD_DIM, nseg), BF), pltpu.VMEM((S, LANES), BF),
                        pltpu.VMEM((LANES, C_GQA * nseg), BF)],
        compiler_params=_params("arbitrary", "arbitrary", "arbitrary"), name="nsa_prompt",
    )(qn, qr, kcmp, vcmp, kv, kv, kv, kv, gates, z)


def _gate_weights(w_in_c, li):
    wg = w_in_c[li, :, C_G:C_Z].reshape(D_MODEL, 3, C_KV_HEADS, C_GQA)
    wg = jnp.transpose(wg, (0, 2, 1, 3)).reshape(D_MODEL, C_KV_HEADS, 3 * C_GQA)
    wg = jnp.pad(wg, ((0, 0), (0, 0), (0, LANES - 3 * C_GQA)))
    return wg.reshape(D_MODEL, C_KV_HEADS * LANES)


def _emit_kernel(*refs, heads, n_src, n_layers, tm):
    o_ref = refs[n_layers * n_src]
    layer = pl.program_id(1)
    for li in range(n_layers):
        def copy(li=li):
            for a in range(n_src):
                src = refs[li * n_src + a]
                for h in range(heads):
                    o_ref[a, pl.ds(h, tm, stride=heads), :] = src[:, h * HEAD_DIM:(h + 1) * HEAD_DIM]

        pl.when(layer == li)(copy)


def _emit_rows(layer_srcs, heads, B, S, keep):
    width = heads * HEAD_DIM
    tm = EMIT_TILE
    spb, row0 = S // tm, (S - keep) // tm
    n_layers, n_src = len(layer_srcs), len(layer_srcs[0])

    def src_spec(li, off):
        return pl.BlockSpec((tm, width),
                            lambda b, l, i: (jnp.where(l == li, b * spb + row0 + i, 0), off // width))

    return pl.pallas_call(
        functools.partial(_emit_kernel, heads=heads, n_src=n_src, n_layers=n_layers, tm=tm),
        grid=(B, n_layers, keep // tm),
        in_specs=[src_spec(li, off) for li, srcs in enumerate(layer_srcs) for _, off in srcs],
        out_specs=pl.BlockSpec((None, None, n_src, tm * heads, HEAD_DIM), lambda b, l, i: (b, l, 0, i, 0)),
        out_shape=jax.ShapeDtypeStruct((B, n_layers, n_src, keep * heads, HEAD_DIM), F32),
        compiler_params=_params("arbitrary", "arbitrary", "arbitrary"), name="emit_rows",
    )(*[a for srcs in layer_srcs for a, _ in srcs])


def _ab_in_proj(xn, li, prm, tables):
    w = prm["w_in_ab"]
    tiles = w.shape[2] // MM_TILE_N_WIDE
    q_tiles, k_tiles = A_GROUPS * A_WIDTH // MM_TILE_N_WIDE, A_WIDTH // MM_TILE_N_WIDE
    kinds = [TILE_NORM] * (q_tiles + k_tiles) + [TILE_RAW] * (tiles - q_tiles - k_tiles)
    gain_of_tile = [0] * q_tiles + [1] * k_tiles + [0] * (tiles - q_tiles - k_tiles)
    gains = jnp.stack([prm["qn_a"][li], prm["kn_a"][li]])[:, None]
    return _matmul_norm(xn, w, li, 0, kinds, gains, gain_of_tile, tables)


def _c_in_proj(xn, li, prm, tables):
    w = prm["w_in_c"]
    q_tiles = C_WIDTH // MM_TILE_N_WIDE
    qn, qr = _matmul_norm(xn, w, li, C_W_Q, [TILE_NORM] * q_tiles, prm["qn_c"][li][None, None],
                          [0] * q_tiles, tables, two_out=True)
    kv = _matmul_norm(xn, w, li, C_W_KV, [TILE_RAW, TILE_HALF_NORM, TILE_HALF_NORM],
                      prm["kn_c"][li, 1:3][:, None], [0, 0, 1], tables)
    z = _matmul(xn, w[li, :, C_Z:], None, 0, C_WIDTH)
    gates = _matmul(xn, _gate_weights(w, li), None, 0, C_KV_HEADS * LANES)
    return qn, qr, kv, z, gates


def _ab_layer_prompt(h, li, layer, p, prm, tables, B, S):
    xn = _rms_rows(h, prm["ln_ab"], li, BF)
    proj = _ab_in_proj(xn, li, prm, tables)
    ga = _dilated_prompt(proj, B, S)
    vn, gb = _gmlp_prompt(proj, prm["ln_sgu"], prm["ws_b"], prm["bs_b"], li)
    h1, h1b = _proj_residual([ga, gb], prm["w_out_ab"], li, h, True)
    h2 = _ple(h1, h1b, p, prm["w_ple_gate"], prm["w_ple"], layer)
    chunk_start = ((S - 1) // B_CHUNK) * B_CHUNK
    return h2, proj, vn.reshape(B, S, B_WIDTH)[:, chunk_start:]


def _c_layer_prompt(h, li, layer, p, prm, tables, B, S):
    xn = _rms_rows(h, prm["ln_c"], li, BF)
    qn, qr, kv, z, gates = _c_in_proj(xn, li, prm, tables)
    kcmp, vcmp = _compress_prompt(kv, prm["cw_k"], prm["cw_v"], prm["kn_c"][li, 0][None], li, B, S)
    lhs = _nsa_prompt(qn, qr, kcmp, vcmp, kv, gates, z, B, S)
    h1, h1b = _proj_residual([lhs], prm["w_out_c"], li, h, True)
    h2 = _ple(h1, h1b, p, prm["w_ple_gate"], prm["w_ple"], layer)
    return h2, kv


def _prompt_trunk(x, p, prm):
    B, S, _ = x.shape
    tables = _rope_tables(S, 0, 1)
    h = x.reshape(B * S, D_MODEL)
    depth = p.shape[0]
    p = p.reshape(depth, B * S, PLE_DIM)
    projs, kvs, b_rows = [], [], []
    for i in range(depth):
        li = i // 2
        if i % 2 == 0:
            h, proj, vb = _ab_layer_prompt(h, li, i, p, prm, tables, B, S)
            projs.append(proj)
            b_rows.append(vb)
        else:
            h, kv = _c_layer_prompt(h, li, i, p, prm, tables, B, S)
            kvs.append(kv)

    def rows(arrays, k_off, v_off, heads, keep):
        a = _emit_rows([[(x, k_off), (x, v_off)] for x in arrays], heads, B, S, keep)
        return a.reshape(a.shape[:3] + (keep, heads, HEAD_DIM))

    return (h.reshape(B, S, D_MODEL), rows(projs, AB_K, AB_V, A_HEADS, S), jnp.stack(b_rows, axis=1),
            rows(kvs, C_KC, C_VC, C_KV_HEADS, S), rows(kvs, C_KS, C_VS, C_KV_HEADS, S),
            rows(kvs, C_KW, C_VW, C_KV_HEADS, min(C_WIN, S)))


def _pad_rows(x, rows):
    return jnp.concatenate([x, jnp.zeros((rows - x.shape[0], x.shape[1]), x.dtype)], axis=0)


def _bf_round(x):
    return x.astype(BF).astype(F32)


def _dilated_sample_kernel(q_ref, kn_ref, pj_ref, kp_ref, vp_ref, o_ref, *, P):
    h = pl.program_id(1)
    qs = [q_ref[pl.ds(g * A_HEADS + h, 1), :] for g in range(A_GROUPS)]
    qmat = _pad_rows(jnp.concatenate(qs, axis=0), SUBLANES)
    k_new = kn_ref[pl.ds(h, 1), :]
    v_new = pj_ref[pl.ds(AB_V // HEAD_DIM + h, 1), :]
    z = pj_ref[pl.ds(AB_ZA // HEAD_DIM + h, 1), :]
    grp = lax.broadcasted_iota(jnp.int32, (SUBLANES, P), 0)
    delta = P - lax.broadcasted_iota(jnp.int32, (SUBLANES, P), 1)
    dil = jnp.where(grp == 0, A_PATTERNS[0][1], jnp.where(grp == 1, A_PATTERNS[1][1], A_PATTERNS[2][1]))
    win = jnp.where(grp == 0, A_PATTERNS[0][0], jnp.where(grp == 1, A_PATTERNS[1][0], A_PATTERNS[2][0]))
    valid = ((delta & (dil - 1)) == 0) & (delta <= win) & (grp < A_GROUPS)
    head_rows = pl.ds(h, P, stride=A_HEADS)
    s = jnp.where(valid, _dot_nt(qmat.astype(BF), kp_ref[head_rows, :].astype(BF)) * ATTN_SCALE, NEG_INF)
    s_new = jnp.sum(_bf_round(qmat) * _bf_round(k_new), axis=-1, keepdims=True) * ATTN_SCALE
    m = jnp.maximum(jnp.max(s, axis=-1, keepdims=True), s_new)
    e = jnp.exp(s - m)
    e_new = jnp.exp(s_new - m)
    den = jnp.sum(e, axis=-1, keepdims=True) + e_new
    num = _dot(e.astype(BF), vp_ref[head_rows, :].astype(BF)) + _bf_round(e_new) * _bf_round(v_new)
    live = lax.broadcasted_iota(jnp.int32, (SUBLANES, 1), 0) < A_GROUPS
    m_all = jnp.max(jnp.where(live, m, NEG_INF), axis=0, keepdims=True)
    w = jnp.where(live, jnp.exp(m - m_all), 0.0)
    num_t = jnp.sum(w * num, axis=0, keepdims=True)
    den_t = jnp.sum(w * den, axis=0, keepdims=True)
    o_ref[pl.ds(h, 1), :] = (num_t / den_t) * _silu(z)


def _dilated_sample(q_r, k_r, proj, cache, li):
    DB, P = cache.shape[0], cache.shape[3]
    cache = cache.reshape(cache.shape[:3] + (P * A_HEADS, HEAD_DIM))
    row3 = lambda x: x.reshape(DB, x.shape[1] // HEAD_DIM, HEAD_DIM)
    full = lambda x: pl.BlockSpec((None,) + x.shape[1:], lambda b, h: (b, 0, 0))
    q3, k3, p3 = row3(q_r), row3(k_r), row3(proj)
    cspec = lambda kv: pl.BlockSpec((None, None, None, P * A_HEADS, HEAD_DIM), lambda b, h: (b, li, kv, 0, 0))
    out = pl.pallas_call(
        functools.partial(_dilated_sample_kernel, P=P), grid=(DB, A_HEADS),
        in_specs=[full(q3), full(k3), full(p3), cspec(0), cspec(1)],
        out_specs=pl.BlockSpec((None, A_HEADS, HEAD_DIM), lambda b, h: (b, 0, 0)),
        out_shape=jax.ShapeDtypeStruct((DB, A_HEADS, HEAD_DIM), F32),
        compiler_params=_params("arbitrary", "arbitrary"), name="dilated_sample",
    )(q3, k3, p3, cache, cache)
    return out.reshape(DB, A_WIDTH)


def _gmlp_sample_kernel(u_ref, v_ref, z_ref, g_ref, w0_ref, b0_ref, vn_ref, o_ref):
    vn = _layer_norm(v_ref[...], g_ref[...])
    vn_ref[...] = vn
    mixed = _bf_round(w0_ref[...]) * _bf_round(vn) + b0_ref[...]
    o_ref[...] = (u_ref[...] * mixed) * _silu(z_ref[...])


def _gmlp_sample(proj, ln_sgu, ws_b, bs_b, li):
    M, wb = proj.shape[0], B_WIDTH
    blk = lambda off: pl.BlockSpec((M, wb), lambda i: (0, off // wb))
    vec = pl.BlockSpec((1, wb), lambda i: (0, 0))
    w0 = jnp.repeat(ws_b[li, :, 0, 0], wb // B_GROUPS)[None]
    b0 = jnp.repeat(bs_b[li, :, 0], wb // B_GROUPS)[None]
    shp = jax.ShapeDtypeStruct((M, wb), F32)
    return pl.pallas_call(
        _gmlp_sample_kernel, in_specs=[blk(AB_UB), blk(AB_VB), blk(AB_ZB), vec, vec, vec],
        out_specs=[pl.BlockSpec((M, wb), lambda i: (0, 0))] * 2, out_shape=[shp, shp], grid=(1,),
        compiler_params=_params("arbitrary"), name="gmlp_sample",
    )(proj, proj, proj, ln_sgu[li][None], w0, b0)


def _page_compress_kernel(pt_ref, w_ref, *refs, pages, segs):
    o_ref = refs[pages]
    tiles = CMP_STRIDE * C_KV_HEADS // SUBLANES
    low = lax.broadcasted_iota(jnp.int32, (SUBLANES, HEAD_DIM), 0) < C_KV_HEADS
    for j in range(pages):
        page = refs[j]
        for kv in range(2):
            folded = []
            for n in range(segs):
                base = n * tiles * SUBLANES
                first = second = None
                for i in range(tiles):
                    rows = page[kv, base + i * SUBLANES:base + (i + 1) * SUBLANES, :]
                    t1 = rows * w_ref[2 * kv, i * SUBLANES:(i + 1) * SUBLANES, :]
                    t2 = rows * w_ref[2 * kv + 1, i * SUBLANES:(i + 1) * SUBLANES, :]
                    first = t1 if first is None else first + t1
                    second = t2 if second is None else second + t2
                folded.append((first + pltpu.roll(first, C_KV_HEADS, 0), second + pltpu.roll(second, C_KV_HEADS, 0)))
            for i in range(segs // 2):
                out_rows = slice((j * segs // 2 + i) * SUBLANES, (j * segs // 2 + i + 1) * SUBLANES)
                o_ref[2 * kv, out_rows, :] = jnp.where(low, folded[2 * i][0], folded[2 * i + 1][0])
                o_ref[2 * kv + 1, out_rows, :] = jnp.where(low, folded[2 * i][1], folded[2 * i + 1][1])


def _page_compress(pool, page_table, cw_k, cw_v, li, pages=16):
    DB, n_pages = page_table.shape
    page = pool.shape[3]
    assert 2 * C_KV_HEADS == SUBLANES and page % (2 * CMP_STRIDE) == 0
    pool = pool.reshape(pool.shape[:3] + (page * C_KV_HEADS, HEAD_DIM))
    segs = page // CMP_STRIDE
    rows_out = segs * C_KV_HEADS
    lanes = lambda w: jnp.broadcast_to(w.reshape(CMP_STRIDE * C_KV_HEADS, 1), (CMP_STRIDE * C_KV_HEADS, HEAD_DIM))
    w = jnp.stack([lanes(cw_k[li, :CMP_STRIDE]), lanes(cw_k[li, CMP_STRIDE:]),
                   lanes(cw_v[li, :CMP_STRIDE]), lanes(cw_v[li, CMP_STRIDE:])])

    def pspec(j):
        return pl.BlockSpec((None, None, 2, page * C_KV_HEADS, HEAD_DIM),
                            lambda b, i, pt: (pt[b * n_pages + i * pages + j], li, 0, 0, 0))

    return pl.pallas_call(
        functools.partial(_page_compress_kernel, pages=pages, segs=segs),
        grid_spec=pltpu.PrefetchScalarGridSpec(
            num_scalar_prefetch=1, grid=(DB, n_pages // pages),
            in_specs=[pl.BlockSpec(w.shape, lambda b, i, pt: (0, 0, 0))] + [pspec(j) for j in range(pages)],
            out_specs=pl.BlockSpec((None, 4, rows_out * pages, HEAD_DIM), lambda b, i, pt: (b, 0, i, 0))),
        out_shape=jax.ShapeDtypeStruct((DB, 4, n_pages * rows_out, HEAD_DIM), F32),
        compiler_params=_params("arbitrary", "arbitrary"), name="page_compress",
    )(page_table.reshape(-1), w, *([pool] * pages))


def _cmp_sample_kernel(fs_ref, q_ref, g_ref, o_ref, ix_ref, *, pos, n_sel, nj):
    rows = fs_ref.shape[1]
    row_i = lax.broadcasted_iota(jnp.int32, (rows, HEAD_DIM), 0)
    complete = row_i < rows - C_KV_HEADS
    kc = jnp.where(complete, fs_ref[0] + pltpu.roll(fs_ref[1], rows - C_KV_HEADS, 0), 0.0)
    vc = jnp.where(complete, fs_ref[2] + pltpu.roll(fs_ref[3], rows - C_KV_HEADS, 0), 0.0)
    kc = kc * lax.rsqrt(jnp.mean(kc * kc, axis=-1, keepdims=True) + NORM_EPS) * g_ref[...]
    col = lax.broadcasted_iota(jnp.int32, (C_HEADS, rows), 1)
    q_head = lax.broadcasted_iota(jnp.int32, (C_HEADS, rows), 0)
    vis = (((col // C_KV_HEADS) * CMP_STRIDE + (CMP_LEN - 1) <= pos)
           & ((col & (C_KV_HEADS - 1)) == q_head // C_GQA))
    s = jnp.where(vis, _dot_nt(q_ref[...].astype(BF), kc.astype(BF)) * ATTN_SCALE, NEG_INF)
    m = jnp.max(s, axis=-1, keepdims=True)
    m = jnp.where(m == NEG_INF, 0.0, m)
    e = jnp.exp(s - m)
    prob_b = (e / jnp.maximum(jnp.sum(e, axis=-1, keepdims=True), TINY)).astype(BF)
    o_ref[...] = _dot(prob_b, vc.astype(BF))

    imp_heads = _dot(prob_b, _cmp_to_sel(rows, nj, C_KV_HEADS).astype(BF))
    lane = lax.broadcasted_iota(jnp.int32, (1, nj), 1)
    cur = pos // SEL_BLOCK
    forced = (lane == 0) | (lane == cur) | (lane == cur - 1)
    ii = lax.broadcasted_iota(jnp.int32, (nj, nj), 0)
    jj = lax.broadcasted_iota(jnp.int32, (nj, nj), 1)
    slot = lax.broadcasted_iota(jnp.int32, (nj, LANES), 1).astype(F32)
    blk_id = lax.broadcasted_iota(jnp.int32, (nj, LANES), 0)
    for h in range(C_KV_HEADS):
        imp = jnp.sum(imp_heads[h * C_GQA:(h + 1) * C_GQA], axis=0, keepdims=True)
        score = jnp.where(forced, imp + FORCE_BONUS, imp)
        score = jnp.where((lane <= cur) & (lane < n_sel), score, NEG_INF)
        s_row = jnp.broadcast_to(score, (nj, nj))
        s_col = jnp.sum(jnp.where(ii == jj, s_row, 0.0), axis=-1, keepdims=True)
        ahead = (s_row > s_col) | ((s_row == s_col) & (jj < ii))
        rank = jnp.sum(jnp.where(ahead, 1.0, 0.0), axis=-1, keepdims=True)
        hit = rank == slot
        idx = jnp.sum(jnp.where(hit, blk_id, 0), axis=0, keepdims=True)
        ok = jnp.sum(jnp.where(hit & (s_col > NEG_INF), 1, 0), axis=0, keepdims=True)
        ix_ref[h] = jnp.concatenate([idx, ok, jnp.zeros((SUBLANES - 2, LANES), jnp.int32)], axis=0)


def _cmp_sample(fs, qn, gain, pos):
    DB, _, rows, _ = fs.shape
    total = pos + 1
    n_sel = -(-total // SEL_BLOCK)
    nj = -(-n_sel // LANES) * LANES
    q3 = qn.reshape(DB, C_HEADS, HEAD_DIM)
    return pl.pallas_call(
        functools.partial(_cmp_sample_kernel, pos=pos, n_sel=n_sel, nj=nj), grid=(DB,),
        in_specs=[pl.BlockSpec((None, 4, rows, HEAD_DIM), lambda b: (b, 0, 0, 0)),
                  pl.BlockSpec((None, C_HEADS, HEAD_DIM), lambda b: (b, 0, 0)),
                  pl.BlockSpec((1, HEAD_DIM), lambda b: (0, 0))],
        out_specs=[pl.BlockSpec((None, C_HEADS, HEAD_DIM), lambda b: (b, 0, 0)),
                   pl.BlockSpec((None, C_KV_HEADS, SUBLANES, LANES), lambda b: (b, 0, 0, 0))],
        out_shape=[jax.ShapeDtypeStruct((DB, C_HEADS, HEAD_DIM), F32),
                   jax.ShapeDtypeStruct((DB, C_KV_HEADS, SUBLANES, LANES), jnp.int32)],
        compiler_params=_params("arbitrary"), name="cmp_sample",
    )(fs, q3, gain)


def _sel_sample_kernel(pt_ref, ix_ref, ok_ref, q_ref, kn_ref, pj_ref, *refs, nb, pos, n_past):
    kv_refs = refs[:nb]
    o_ref, m_ref, l_ref, acc_ref = refs[nb:]
    b, h, i = pl.program_id(0), pl.program_id(1), pl.program_id(2)

    @pl.when(i == 0)
    def _():
        m_ref[...] = jnp.full(m_ref.shape, NEG_INF, F32)
        l_ref[...] = jnp.zeros(l_ref.shape, F32)
        acc_ref[...] = jnp.zeros(acc_ref.shape, F32)

    q8 = _pad_rows(q_ref[pl.ds(h * C_GQA, C_GQA), :], SUBLANES).astype(BF)
    k_new = kn_ref[pl.ds(h, 1), :]
    v_new = pj_ref[pl.ds(C_VS // HEAD_DIM + h, 1), :]
    first_row = lax.broadcasted_iota(jnp.int32, (SEL_BLOCK, HEAD_DIM), 0) == 0
    r = lax.broadcasted_iota(jnp.int32, (1, SEL_BLOCK), 1)
    head_rows = pl.ds(h, SEL_BLOCK, stride=C_KV_HEADS)
    scores, values = [], []
    for j in range(nb):
        slot = (b * C_KV_HEADS + h) * SEL_TOP + i * nb + j
        blk = ix_ref[slot]
        is_past = blk < n_past
        kt = jnp.where(is_past, kv_refs[j][0, head_rows, :], jnp.where(first_row, k_new, 0.0))
        vt = jnp.where(is_past, kv_refs[j][1, head_rows, :], jnp.where(first_row, v_new, 0.0))
        ok = (blk * SEL_BLOCK + r <= pos) & (ok_ref[slot] > 0)
        scores.append(jnp.where(ok, _dot_nt(q8, kt.astype(BF)) * ATTN_SCALE, NEG_INF))
        values.append(vt.astype(BF))
    m_old = m_ref[...]
    m_new = functools.reduce(jnp.maximum, [jnp.max(s, axis=-1, keepdims=True) for s in scores] + [m_old])
    m_safe = jnp.where(m_new == NEG_INF, 0.0, m_new)
    alpha = jnp.exp(m_old - m_safe)
    es = [jnp.exp(s - m_safe) for s in scores]
    l_ref[...] = alpha * l_ref[...] + functools.reduce(jnp.add, [jnp.sum(e, axis=-1, keepdims=True) for e in es])
    acc_ref[...] = alpha * acc_ref[...] + functools.reduce(
        jnp.add, [_dot(e.astype(BF), v) for e, v in zip(es, values)])
    m_ref[...] = m_new

    @pl.when(i == pl.num_programs(2) - 1)
    def _():
        o_ref[pl.ds(h * C_GQA, C_GQA), :] = (acc_ref[...] / l_ref[...])[:C_GQA]


def _sel_sample(pool, page_table, idx, ok, qr, ks_r, proj, li, pos, nb=SEL_TOP):
    DB, n_pages = page_table.shape
    page = pool.shape[3]
    bpp = page // SEL_BLOCK
    n_past = n_pages * bpp
    pool = pool.reshape(pool.shape[:3] + (page * C_KV_HEADS, HEAD_DIM))
    row3 = lambda x: x.reshape(DB, x.shape[1] // HEAD_DIM, HEAD_DIM)
    q3, k3, p3 = row3(qr), row3(ks_r), row3(proj)
    full = lambda x: pl.BlockSpec((None,) + x.shape[1:], lambda b, h, i, pt, ix, okf: (b, 0, 0))

    def bspec(j):
        def imap(b, h, i, pt, ix, okf):
            blk = jnp.clip(ix[(b * C_KV_HEADS + h) * SEL_TOP + i * nb + j], 0, n_past - 1)
            return (pt[b * n_pages + blk // bpp], li, 0, blk % bpp, 0)
        return pl.BlockSpec((None, None, 2, SEL_BLOCK * C_KV_HEADS, HEAD_DIM), imap)

    out = pl.pallas_call(
        functools.partial(_sel_sample_kernel, nb=nb, pos=pos, n_past=n_past),
        grid_spec=pltpu.PrefetchScalarGridSpec(
            num_scalar_prefetch=3, grid=(DB, C_KV_HEADS, SEL_TOP // nb),
            in_specs=[full(q3), full(k3), full(p3)] + [bspec(j) for j in range(nb)],
            out_specs=pl.BlockSpec((None, C_HEADS, HEAD_DIM), lambda b, h, i, pt, ix, okf: (b, 0, 0)),
            scratch_shapes=[pltpu.VMEM((SUBLANES, 1), F32), pltpu.VMEM((SUBLANES, 1), F32),
                            pltpu.VMEM((SUBLANES, HEAD_DIM), F32)]),
        out_shape=jax.ShapeDtypeStruct((DB, C_HEADS, HEAD_DIM), F32),
        compiler_params=_params("arbitrary", "arbitrary", "arbitrary"), name="sel_sample",
    )(page_table.reshape(-1), idx, ok, q3, k3, p3, *([pool] * nb))
    return out


def _win_sample_kernel(q_ref, kn_ref, pj_ref, kp_ref, vp_ref, oc_ref, os_ref, g_ref, z_ref, o_ref):
    h = pl.program_id(1)
    heads = pl.ds(h * C_GQA, C_GQA)
    q8 = _pad_rows(q_ref[heads, :], SUBLANES)
    k_new = kn_ref[pl.ds(h, 1), :]
    v_new = pj_ref[pl.ds(C_VW // HEAD_DIM + h, 1), :]
    head_rows = pl.ds(h, kp_ref.shape[0] // C_KV_HEADS, stride=C_KV_HEADS)
    s = _dot_nt(q8.astype(BF), kp_ref[head_rows, :].astype(BF)) * ATTN_SCALE
    s_new = jnp.sum(_bf_round(q8) * _bf_round(k_new), axis=-1, keepdims=True) * ATTN_SCALE
    m = jnp.maximum(jnp.max(s, axis=-1, keepdims=True), s_new)
    e = jnp.exp(s - m)
    e_new = jnp.exp(s_new - m)
    den = jnp.sum(e, axis=-1, keepdims=True) + e_new
    p_past = (e / den).astype(BF)
    o_win = _dot(p_past, vp_ref[head_rows, :].astype(BF)) + _bf_round(e_new / den) * _bf_round(v_new)
    g = g_ref[pl.ds(h, 1), :]
    for a in range(C_GQA):
        gate = [_sigmoid(g[:, br * C_GQA + a:br * C_GQA + a + 1]) for br in range(3)]
        row = pl.ds(h * C_GQA + a, 1)
        o = gate[0] * oc_ref[row, :] + gate[1] * os_ref[row, :] + gate[2] * o_win[a:a + 1]
        o_ref[row, :] = o * _silu(z_ref[row, :])


def _win_sample(cache, qr, kw_r, proj, o_cmp, o_sel, gates, z, li):
    DB, W = cache.shape[0], cache.shape[3]
    assert W <= C_WIN
    cache = cache.reshape(cache.shape[:3] + (W * C_KV_HEADS, HEAD_DIM))
    row3 = lambda x: x.reshape(DB, x.shape[1] // HEAD_DIM, HEAD_DIM)
    full = lambda x: pl.BlockSpec((None,) + x.shape[1:], lambda b, h: (b, 0, 0))
    ins = [row3(qr), row3(kw_r), row3(proj)]
    tail = [o_cmp, o_sel, row3(gates), row3(z)]
    cspec = lambda kv: pl.BlockSpec((None, None, None, W * C_KV_HEADS, HEAD_DIM), lambda b, h: (b, li, kv, 0, 0))
    out = pl.pallas_call(
        _win_sample_kernel, grid=(DB, C_KV_HEADS),
        in_specs=[full(x) for x in ins] + [cspec(0), cspec(1)] + [full(x) for x in tail],
        out_specs=pl.BlockSpec((None, C_HEADS, HEAD_DIM), lambda b, h: (b, 0, 0)),
        out_shape=jax.ShapeDtypeStruct((DB, C_HEADS, HEAD_DIM), F32),
        compiler_params=_params("arbitrary", "arbitrary"), name="win_sample",
    )(*ins, cache, cache, *tail)
    return out.reshape(DB, C_WIDTH)


def _ab_layer_sample(h, li, layer, p, prm, tables, cache_a):
    DB = h.shape[0]
    xn = _rms_rows(h, prm["ln_ab"], li, F32)
    proj = _ab_in_proj(xn, li, prm, tables)
    q_r = proj[:, AB_Q:AB_Q + A_GROUPS * A_WIDTH]
    k_r = proj[:, AB_K:AB_K + A_WIDTH]
    ga = _dilated_sample(q_r, k_r, proj, cache_a, li)
    vn, gb = _gmlp_sample(proj, prm["ln_sgu"], prm["ws_b"], prm["bs_b"], li)
    h1, _ = _proj_residual([ga, gb], prm["w_out_ab"], li, h, False)
    h2 = _ple(h1, h1, p, prm["w_ple_gate"], prm["w_ple"], layer)
    kv = jnp.stack([k_r.reshape(DB, 1, A_HEADS, HEAD_DIM),
                    proj[:, AB_V:AB_V + A_WIDTH].reshape(DB, 1, A_HEADS, HEAD_DIM)], axis=1)
    return h2, kv, vn.reshape(DB, 1, B_WIDTH)


def _c_layer_sample(h, li, layer, p, prm, tables, past, pos):
    DB = h.shape[0]
    xn = _rms_rows(h, prm["ln_c"], li, F32)
    qn, qr, proj, z, gates = _c_in_proj(xn, li, prm, tables)
    ks_r = proj[:, C_KS:C_KS + C_KV_WIDTH]
    kw_r = proj[:, C_KW:C_KW + C_KV_WIDTH]
    fs = _page_compress(past["cmp"], past["page_table"], prm["cw_k"], prm["cw_v"], li)
    o_cmp, picks = _cmp_sample(fs, qn, prm["kn_c"][li, 0][None], pos)
    idx = picks[:, :, 0, :SEL_TOP].reshape(-1)
    ok = picks[:, :, 1, :SEL_TOP].reshape(-1)
    o_sel = _sel_sample(past["sel"], past["page_table"], idx, ok, qr, ks_r, proj, li, pos)
    lhs = _win_sample(past["win"], qr, kw_r, proj, o_cmp, o_sel, gates, z, li)
    h1, _ = _proj_residual([lhs], prm["w_out_c"], li, h, False)
    h2 = _ple(h1, h1, p, prm["w_ple_gate"], prm["w_ple"], layer)
    kvs = (DB, 1, C_KV_HEADS, HEAD_DIM)
    col = lambda off: proj[:, off:off + C_KV_WIDTH].reshape(kvs)
    cmp_rows = jnp.stack([col(C_KC), col(C_VC)], axis=1)
    sel_rows = jnp.stack([ks_r.reshape(kvs), col(C_VS)], axis=1)
    win_rows = jnp.stack([kw_r.reshape(kvs), col(C_VW)], axis=1)
    return h2, cmp_rows, sel_rows, win_rows


def _sample_trunk(x, p, past, prm):
    DB, S, _ = x.shape
    assert S == 1
    pos = past["page_table"].shape[1] * past["cmp"].shape[3]
    assert pos % CMP_STRIDE == 0 and past["cmp"].shape[3] % SEL_BLOCK == 0
    tables = _rope_tables(DB, pos, 0)
    h = x.reshape(DB, D_MODEL)
    a_rows, b_rows, cmp_rows, sel_rows, win_rows = [], [], [], [], []
    pi = p.reshape(p.shape[0], DB, PLE_DIM)
    for i in range(p.shape[0]):
        li = i // 2
        if i % 2 == 0:
            h, kv, vb = _ab_layer_sample(h, li, i, pi, prm, tables, past["a"])
            a_rows.append(kv)
            b_rows.append(vb)
        else:
            h, c, s, w = _c_layer_sample(h, li, i, pi, prm, tables, past, pos)
            cmp_rows.append(c)
            sel_rows.append(s)
            win_rows.append(w)
    st = lambda xs: jnp.stack(xs, axis=1)
    return h.reshape(DB, S, D_MODEL), st(a_rows), st(b_rows), st(cmp_rows), st(sel_rows), st(win_rows)


def kernel(x_prompt, x_sample, cache_a_kv, cache_c_cmp_kv, cache_c_sel_kv, cache_c_win_kv, page_table,
           p_prompt, p_sample, ln_ab, w_in_ab, qn_a, kn_a, ln_sgu, ws_b, bs_b, w_out_ab,
           ln_c, w_in_c, qn_c, kn_c, cw_k, cw_v, w_out_c, w_ple, w_ple_gate):
    prm = dict(ln_ab=ln_ab, w_in_ab=w_in_ab, qn_a=qn_a, kn_a=kn_a, ln_sgu=ln_sgu, ws_b=ws_b, bs_b=bs_b,
               w_out_ab=w_out_ab, ln_c=ln_c, w_in_c=w_in_c, qn_c=qn_c, kn_c=kn_c, cw_k=cw_k, cw_v=cw_v,
               w_out_c=w_out_c, w_ple=w_ple, w_ple_gate=w_ple_gate)
    y_p, a_p, b_p, cmp_p, sel_p, win_p = _prompt_trunk(x_prompt, p_prompt, prm)
    past = dict(a=cache_a_kv, cmp=cache_c_cmp_kv, sel=cache_c_sel_kv, win=cache_c_win_kv, page_table=page_table)
    y_s, a_s, b_s, cmp_s, sel_s, win_s = _sample_trunk(x_sample, p_sample, past, prm)
    return (y_p, y_s, a_p, a_s, b_p, b_s, cmp_p, cmp_s, sel_p, sel_s, win_p, win_s)
```
